```python
import math
import jax, jax.numpy as jnp
from jax import lax
import numpy as np

D_MODEL = 4096
BATCH = 2
SEQ = 4096
DEPTH = 2

N_BRANCHES = 4
BRANCH_WIDTH = D_MODEL // N_BRANCHES
HG_HEADS = 8
HG_DK = BRANCH_WIDTH // HG_HEADS
HG_DV = BRANCH_WIDTH // HG_HEADS
HG_CHUNK = 64
DS_HEADS = 8
DS_HEAD_DIM = BRANCH_WIDTH // DS_HEADS
DS_Q_LORA = 768
DS_KV_LORA = 512
DS_IDX_HEADS = 16
DS_IDX_DIM = 64
DS_TOPK_MAX = 256
DS_QBLOCK = 128
MB_HEADS = 16
MB_HEAD_DIM = BRANCH_WIDTH // MB_HEADS
MB_D_INNER = BRANCH_WIDTH
MB_STATE = 128
MB_GROUPS = 2
MB_CONV = 4
MB_CHUNK = 128
MB_CONV_DIM = MB_D_INNER + 2 * MB_GROUPS * MB_STATE
FX_HEADS = 8
FX_HEAD_DIM = BRANCH_WIDTH // FX_HEADS
FX_QBLOCK = 128
D_FF = 11008
FFN_CONV = 3
EPS = 1e-6

IN_SIZES = (BRANCH_WIDTH, BRANCH_WIDTH, BRANCH_WIDTH, BRANCH_WIDTH,
            DS_Q_LORA, DS_KV_LORA, DS_IDX_DIM, DS_IDX_HEADS,
            MB_D_INNER, MB_CONV_DIM, MB_HEADS,
            BRANCH_WIDTH, BRANCH_WIDTH, BRANCH_WIDTH, FX_HEADS)
D_IN = sum(IN_SIZES)

kernel_name = 'hybrid_hgrn2_dsa_mamba2_fox_gated_merge'


def rms_norm(x, gain):
    xf = x.astype(jnp.float32)
    y = xf * lax.rsqrt(jnp.mean(xf * xf, axis=-1, keepdims=True) + EPS)
    return (y * gain.astype(jnp.float32)).astype(x.dtype)


def causal_dwconv(x, w):
    K, S = w.shape[0], x.shape[1]
    xp = jnp.pad(x, ((0, 0), (K - 1, 0), (0, 0)))
    w = w.astype(x.dtype)
    return sum(xp[:, k:k + S] * w[k] for k in range(K))


def split_columns(p):
    offsets = [int(o) for o in np.cumsum(IN_SIZES)[:-1]]
    return jnp.split(p, offsets, axis=-1)


def hgrn_lower_bounds(logits):
    p = jax.nn.softmax(logits.astype(jnp.float32), axis=0)
    return jnp.cumsum(p, axis=0) - p[0]


def hgrn2_mixer(q, f_logit, i, g, lower_bound, norm_gain):
    f32 = jnp.float32
    B, S, _ = q.shape
    nc = S // HG_CHUNK
    q = jax.nn.silu(q.astype(f32))
    v = jax.nn.silu(i.astype(f32))
    lb = lower_bound.astype(f32)
    f = lb + (1.0 - lb) * jax.nn.sigmoid(f_logit.astype(f32))
    log_f = jnp.log(f)
    k = 1.0 - f

    def to_chunks(t, d):
        return t.reshape(B, nc, HG_CHUNK, HG_HEADS, d).transpose(1, 0, 3, 2, 4)

    qc, kc, gc = to_chunks(q, HG_DK), to_chunks(k, HG_DK), to_chunks(log_f, HG_DK)
    vc = to_chunks(v, HG_DV)
    causal = jnp.tril(jnp.ones((HG_CHUNK, HG_CHUNK), bool))

    def step(state, inp):
        qb, kb, vb, gb = inp
        b = jnp.cumsum(gb, axis=2)
        inter = jnp.einsum('bhtk,bhkv->bhtv', qb * jnp.exp(b), state)
        diff = b[:, :, :, None, :] - b[:, :, None, :, :]
        decay = jnp.exp(jnp.where(causal[:, :, None], diff, -jnp.inf))
        scores = jnp.einsum('bhtk,bhtsk,bhsk->bhts', qb, decay, kb)
        intra = jnp.einsum('bhts,bhsv->bhtv', scores, vb)
        b_last = b[:, :, -1:, :]
        new_state = (jnp.exp(b_last[:, :, 0, :, None]) * state
                     + jnp.einsum('bhsk,bhsv->bhkv', kb * jnp.exp(b_last - b), vb))
        return new_state, inter + intra

    state0 = jnp.zeros((B, HG_HEADS, HG_DK, HG_DV), f32)
    _, o = lax.scan(step, state0, (qc, kc, vc, gc))
    o = o.transpose(1, 0, 3, 2, 4).reshape(B, S, HG_HEADS, HG_DV)
    gate = jax.nn.sigmoid(g.astype(f32)).reshape(B, S, HG_HEADS, HG_DV)
    o = rms_norm(o * gate, norm_gain.reshape(HG_HEADS, HG_DV))
    return o.reshape(B, S, BRANCH_WIDTH)


def dsa_mixer(c_q, c_kv, k_idx, w_idx, q_norm, kv_norm, w_uq, w_iq, w_uk, w_uv):
    f32 = jnp.float32
    B, S, _ = c_q.shape
    topk = min(DS_TOPK_MAX, S // 4)
    nb = S // DS_QBLOCK
    c_q = rms_norm(c_q.astype(f32), q_norm)
    c_kv = rms_norm(c_kv.astype(f32), kv_norm)
    k_idx = k_idx.astype(f32)
    q = (c_q @ w_uq.astype(f32)).reshape(B, S, DS_HEADS, DS_HEAD_DIM)
    q_idx = (c_q @ w_iq.astype(f32)).reshape(B, S, DS_IDX_HEADS, DS_IDX_DIM)
    w_idx = w_idx.astype(f32) * (DS_IDX_HEADS ** -0.5 * DS_IDX_DIM ** -0.5)
    w_uk = w_uk.astype(f32).reshape(DS_KV_LORA, DS_HEADS, DS_HEAD_DIM)
    w_uv = w_uv.astype(f32).reshape(DS_KV_LORA, DS_HEADS, DS_HEAD_DIM)
    q_lat = jnp.einsum('bshd,chd->bshc', q, w_uk) * DS_HEAD_DIM ** -0.5
    key_pos = jnp.arange(S)

    def blockify(t):
        return t.reshape((B, nb, DS_QBLOCK) + t.shape[2:]).swapaxes(0, 1)

    def attend_block(args):
        qi, wi, ql, t0 = args
        t = t0 + jnp.arange(DS_QBLOCK)
        idx_logits = jax.nn.relu(jnp.einsum('bqhd,bsd->bqhs', qi, k_idx))
        score = jnp.einsum('bqhs,bqh->bqs', idx_logits, wi)
        score = jnp.where(key_pos[None, None, :] <= t[None, :, None], score, -jnp.inf)
        _, sel = lax.top_k(score, topk)
        valid = sel <= t[None, :, None]
        kv_sel = jax.vmap(lambda c, ix: c[ix])(c_kv, sel)
        logits = jnp.einsum('bqhc,bqkc->bqhk', ql, kv_sel)
        logits = jnp.where(valid[:, :, None, :], logits, -jnp.inf)
        p = jax.nn.softmax(logits, axis=-1)
        return jnp.einsum('bqhk,bqkc->bqhc', p, kv_sel)

    o_lat = lax.map(attend_block, (blockify(q_idx), blockify(w_idx), blockify(q_lat),
                                   jnp.arange(nb) * DS_QBLOCK))
    o_lat = o_lat.swapaxes(0, 1).reshape(B, S, DS_HEADS, DS_KV_LORA)
    o = jnp.einsum('bshc,chv->bshv', o_lat, w_uv)
    return o.reshape(B, S, BRANCH_WIDTH)


def segsum(a):
    T = a.shape[-1]
    cs = jnp.cumsum(a, axis=-1)
    diff = cs[..., :, None] - cs[..., None, :]
    return jnp.where(jnp.tril(jnp.ones((T, T), bool)), diff, -jnp.inf)


def ssd_chunked(x, a, b, c):
    Bsz, S, H, P = x.shape
    nc = S // MB_CHUNK
    x = x.reshape(Bsz, nc, MB_CHUNK, H, P)
    b = b.reshape(Bsz, nc, MB_CHUNK, H, -1)
    c = c.reshape(Bsz, nc, MB_CHUNK, H, -1)
    a = a.reshape(Bsz, nc, MB_CHUNK, H).transpose(0, 3, 1, 2)
    a_cs = jnp.cumsum(a, axis=-1)
    Lmat = jnp.exp(segsum(a))
    y_diag = jnp.einsum('bclhn,bcshn,bhcls,bcshp->bclhp', c, b, Lmat, x)
    decay_states = jnp.exp(a_cs[..., -1:] - a_cs)
    states = jnp.einsum('bclhn,bhcl,bclhp->bchpn', b, decay_states, x)
    states = jnp.concatenate([jnp.zeros_like(states[:, :1]), states], axis=1)
    chunk_decay = jnp.exp(segsum(jnp.pad(a_cs[..., -1], ((0, 0), (0, 0), (1, 0)))))
    states = jnp.einsum('bhzc,bchpn->bzhpn', chunk_decay, states)[:, :-1]
    y_off = jnp.einsum('bclhn,bchpn,bhcl->bclhp', c, states, jnp.exp(a_cs))
    return (y_diag + y_off).reshape(Bsz, S, H, P)


def mamba2_mixer(z, xbc, dt, conv_w, conv_b, dt_bias, a_log, d_skip, norm_gain):
    f32 = jnp.float32
    B, S, _ = z.shape
    rep = MB_HEADS // MB_GROUPS
    xbc = jax.nn.silu(causal_dwconv(xbc.astype(f32), conv_w) + conv_b.astype(f32))
    xs, bm, cm = jnp.split(xbc, [MB_D_INNER, MB_D_INNER + MB_GROUPS * MB_STATE], axis=-1)
    xs = xs.reshape(B, S, MB_HEADS, MB_HEAD_DIM)
    bm = jnp.repeat(bm.reshape(B, S, MB_GROUPS, MB_STATE), rep, axis=2)
    cm = jnp.repeat(cm.reshape(B, S, MB_GROUPS, MB_STATE), rep, axis=2)
    dt = jax.nn.softplus(dt.astype(f32) + dt_bias.astype(f32))
    a = -jnp.exp(a_log.astype(f32))
    y = ssd_chunked(xs * dt[..., None], a * dt, bm, cm) + d_skip.astype(f32)[:, None] * xs
    y = y.reshape(B, S, MB_D_INNER) * jax.nn.silu(z.astype(f32))
    y = rms_norm(y.reshape(B, S, MB_GROUPS, MB_D_INNER // MB_GROUPS),
                 norm_gain.reshape(MB_GROUPS, MB_D_INNER // MB_GROUPS))
    return y.reshape(B, S, MB_D_INNER)


def fox_mixer(q, k, v, f_logit, f_bias):
    f32 = jnp.float32
    B, S, _ = q.shape
    nb = S // FX_QBLOCK
    q = q.astype(f32).reshape(B, S, FX_HEADS, FX_HEAD_DIM) * FX_HEAD_DIM ** -0.5
    k = k.astype(f32).reshape(B, S, FX_HEADS, FX_HEAD_DIM)
    v = v.astype(f32).reshape(B, S, FX_HEADS, FX_HEAD_DIM)
    log_f = jax.nn.log_sigmoid(f_logit.astype(f32) + f_bias.astype(f32))
    cum = jnp.cumsum(log_f, axis=1)
    cum_k = cum.transpose(0, 2, 1)
    key_pos = jnp.arange(S)

    def blockify(t):
        return t.reshape((B, nb, FX_QBLOCK) + t.shape[2:]).swapaxes(0, 1)

    def attend_block(args):
        qb, cb, t0 = args
        t = t0 + jnp.arange(FX_QBLOCK)
        logits = (jnp.einsum('bqhd,bshd->bhqs', qb, k)
                  + cb.transpose(0, 2, 1)[..., None] - cum_k[:, :, None, :])
        logits = jnp.where(key_pos[None, :] <= t[:, None], logits, -jnp.inf)
        p = jax.nn.softmax(logits, axis=-1)
        return jnp.einsum('bhqs,bshd->bqhd', p, v)

    o = lax.map(attend_block, (blockify(q), blockify(cum), jnp.arange(nb) * FX_QBLOCK))
    return o.swapaxes(0, 1).reshape(B, S, BRANCH_WIDTH)


def conv_ffn(h, w_gate, w_up, conv_w, w_down):
    gate = causal_dwconv(h @ w_gate, conv_w)
    return (jax.nn.silu(gate) * (h @ w_up)) @ w_down


def setup_inputs(seed: int = 0) -> dict:
    key = jax.random.key(seed)
    keys = iter(jax.random.split(key, 32))
    L = DEPTH

    def nrm(shape, scale):
        return jax.random.normal(next(keys), shape, jnp.float32) * scale

    def gain(shape):
        return 1.0 + nrm(shape, 0.02)

    x = nrm((BATCH, SEQ, D_MODEL), 1.0)
    attn_norm = gain((L, D_MODEL))
    ffn_norm = gain((L, D_MODEL))
    final_norm = gain((D_MODEL,))
    w_in = nrm((L, D_MODEL, D_IN), D_MODEL ** -0.5)
    hgrn_lb_logits = nrm((L, HG_HEADS * HG_DK), 0.5)
    hgrn_norm = gain((L, BRANCH_WIDTH))
    dsa_q_norm = gain((L, DS_Q_LORA))
    dsa_kv_norm = gain((L, DS_KV_LORA))
    dsa_w_uq = nrm((L, DS_Q_LORA, DS_HEADS * DS_HEAD_DIM), DS_Q_LORA ** -0.5)
    dsa_w_iq = nrm((L, DS_Q_LORA, DS_IDX_HEADS * DS_IDX_DIM), DS_Q_LORA ** -0.5)
    dsa_w_uk = nrm((L, DS_KV_LORA, DS_HEADS * DS_HEAD_DIM), DS_KV_LORA ** -0.5)
    dsa_w_uv = nrm((L, DS_KV_LORA, DS_HEADS * DS_HEAD_DIM), DS_KV_LORA ** -0.5)
    ssm_conv_w = nrm((L, MB_CONV, MB_CONV_DIM), MB_CONV ** -0.5)
    ssm_conv_b = nrm((L, MB_CONV_DIM), 0.02)
    dt0 = jnp.exp(jax.random.uniform(next(keys), (L, MB_HEADS), jnp.float32,
                                     minval=math.log(1e-3), maxval=math.log(1e-1)))
    ssm_dt_bias = dt0 + jnp.log(-jnp.expm1(-dt0))
    ssm_a_log = jnp.log(jax.random.uniform(next(keys), (L, MB_HEADS), jnp.float32,
                                           minval=1.0, maxval=16.0))
    ssm_d = gain((L, MB_HEADS))
    ssm_norm = gain((L, MB_D_INNER))
    fox_f_bias = 2.0 + nrm((L, FX_HEADS), 0.5)
    w_gate = nrm((L, N_BRANCHES, D_MODEL, D_MODEL), D_MODEL ** -0.5)
    w_branch = nrm((L, N_BRANCHES, BRANCH_WIDTH, D_MODEL), BRANCH_WIDTH ** -0.5)
    w_out = nrm((L, D_MODEL, D_MODEL), D_MODEL ** -0.5)
    ffn_w_gate = nrm((L, D_MODEL, D_FF), D_MODEL ** -0.5)
    ffn_w_up = nrm((L, D_MODEL, D_FF), D_MODEL ** -0.5)
    ffn_conv = nrm((L, FFN_CONV, D_FF), FFN_CONV ** -0.5)
    ffn_w_down = nrm((L, D_FF, D_MODEL), D_FF ** -0.5)
    return {'x': x, 'attn_norm': attn_norm, 'ffn_norm': ffn_norm, 'final_norm': final_norm,
            'w_in': w_in, 'hgrn_lb_logits': hgrn_lb_logits, 'hgrn_norm': hgrn_norm,
            'dsa_q_norm': dsa_q_norm, 'dsa_kv_norm': dsa_kv_norm, 'dsa_w_uq': dsa_w_uq,
            'dsa_w_iq': dsa_w_iq, 'dsa_w_uk': dsa_w_uk, 'dsa_w_uv': dsa_w_uv,
            'ssm_conv_w': ssm_conv_w, 'ssm_conv_b': ssm_conv_b, 'ssm_dt_bias': ssm_dt_bias,
            'ssm_a_log': ssm_a_log, 'ssm_d': ssm_d, 'ssm_norm': ssm_norm,
            'fox_f_bias': fox_f_bias, 'w_gate': w_gate, 'w_branch': w_branch, 'w_out': w_out,
            'ffn_w_gate': ffn_w_gate, 'ffn_w_up': ffn_w_up, 'ffn_conv': ffn_conv,
            'ffn_w_down': ffn_w_down}


def reference(x, attn_norm, ffn_norm, final_norm, w_in, hgrn_lb_logits, hgrn_norm,
              dsa_q_norm, dsa_kv_norm, dsa_w_uq, dsa_w_iq, dsa_w_uk, dsa_w_uv,
              ssm_conv_w, ssm_conv_b, ssm_dt_bias, ssm_a_log, ssm_d, ssm_norm,
              fox_f_bias, w_gate, w_branch, w_out, ffn_w_gate, ffn_w_up, ffn_conv, ffn_w_down):
    lower_bounds = hgrn_lower_bounds(hgrn_lb_logits)
    for l in range(DEPTH):
        h = rms_norm(x, attn_norm[l])
        (hq, hf, hi, hg, dcq, dckv, dki, dwi, mz, mxbc, mdt,
         fq, fk, fv, ff) = split_columns(h @ w_in[l])
        y_a = hgrn2_mixer(hq, hf, hi, hg, lower_bounds[l], hgrn_norm[l])
        y_b = dsa_mixer(dcq, dckv, dki, dwi, dsa_q_norm[l], dsa_kv_norm[l],
                        dsa_w_uq[l], dsa_w_iq[l], dsa_w_uk[l], dsa_w_uv[l])
        y_c = mamba2_mixer(mz, mxbc, mdt, ssm_conv_w[l], ssm_conv_b[l], ssm_dt_bias[l],
                           ssm_a_log[l], ssm_d[l], ssm_norm[l])
        y_d = fox_mixer(fq, fk, fv, ff, fox_f_bias[l])
        merged = jnp.zeros_like(x)
        for n, y_n in enumerate((y_a, y_b, y_c, y_d)):
            merged = merged + jax.nn.sigmoid(h @ w_gate[l, n]) * (y_n.astype(x.dtype) @ w_branch[l, n])
        x = x + merged @ w_out[l]
        h = rms_norm(x, ffn_norm[l])
        x = x + conv_ffn(h, ffn_w_gate[l], ffn_w_up[l], ffn_conv[l], ffn_w_down[l])
    return rms_norm(x, final_norm)
```

```python
import functools

import jax
import jax.numpy as jnp
from jax import lax
from jax.experimental import pallas as pl
from jax.experimental.pallas import tpu as pltpu

F32, BF16, I32 = jnp.float32, jnp.bfloat16, jnp.int32
HIGHEST = lax.Precision.HIGHEST

D_MODEL = 4096
DEPTH = 2
BRANCH = 1024
HG_HEADS, HG_D, HG_CHUNK, HG_SUB = 8, 128, 64, 16
DS_HEADS, DS_HEAD_DIM, DS_Q_LORA, DS_KV_LORA = 8, 128, 768, 512
DS_IDX_HEADS, DS_IDX_DIM, DS_TOPK_MAX, DS_QBLOCK = 16, 64, 256, 128
MB_HEADS, MB_P, MB_N, MB_GROUPS, MB_CONV, MB_CHUNK = 16, 64, 128, 2, 4, 128
MB_CONV_DIM = BRANCH + 2 * MB_GROUPS * MB_N
FX_HEADS, FX_D = 8, 128
D_FF = 11008
D_FF_PAD = 11264
FFN_CONV = 3
EPS = 1e-6
LANES = 128
NEG_BIG = -1e30
INT_MIN = -(2 ** 31)

MISC_CKV = 0
MISC_IDX = 512
MISC_DT = 640
MISC_FF = 768
MISC_W = 896


def _cp(sem, vmem_mb=48):
    return pltpu.CompilerParams(dimension_semantics=sem, vmem_limit_bytes=vmem_mb * 2 ** 20)


def _dot(a, b, precision=None):
    return jnp.dot(a, b, preferred_element_type=F32, precision=precision)


def _dot_nt(a, b):
    return lax.dot_general(a, b, (((1,), (1,)), ((), ())), preferred_element_type=F32)


def _dot_tn(a, b):
    return lax.dot_general(a, b, (((0,), (0,)), ((), ())), preferred_element_type=F32)


def _tril(n):
    r = lax.broadcasted_iota(I32, (n, n), 0)
    c = lax.broadcasted_iota(I32, (n, n), 1)
    return (r >= c).astype(F32)


def _silu(x):
    return x * jax.nn.sigmoid(x)


def _rmsnorm_body(x_ref, g_ref, o_ref):
    x = x_ref[...].astype(F32)
    ms = jnp.mean(x * x, axis=-1, keepdims=True)
    o_ref[...] = (x * lax.rsqrt(ms + EPS) * g_ref[...]).astype(o_ref.dtype)


def rmsnorm(x, gain, out_dtype, width=None, col_block=0, tm=256):
    m = x.shape[0]
    width = x.shape[1] if width is None else width
    return pl.pallas_call(
        _rmsnorm_body,
        out_shape=jax.ShapeDtypeStruct((m, width), out_dtype),
        grid=(m // tm,),
        in_specs=[pl.BlockSpec((tm, width), lambda i: (i, col_block)),
                  pl.BlockSpec((1, width), lambda i: (0, 0))],
        out_specs=pl.BlockSpec((tm, width), lambda i: (i, 0)),
        compiler_params=_cp(("parallel",)),
        name="rmsnorm",
    )(x, gain.reshape(1, width).astype(F32))


def _mm_body(a_ref, b_ref, o_ref):
    o_ref[...] = _dot(a_ref[...], b_ref[...]).astype(o_ref.dtype)


def _mm_add_body(a_ref, b_ref, r_ref, o_ref):
    o_ref[...] = (r_ref[...] + _dot(a_ref[...], b_ref[...])).astype(o_ref.dtype)


def matmul(a, b, out_dtype, tm, tn, residual=None):
    m, k = a.shape
    n = b.shape[1]
    tm, tn = min(tm, m), min(tn, n)
    in_specs = [pl.BlockSpec((tm, k), lambda i, j: (i, 0)),
                pl.BlockSpec((k, tn), lambda i, j: (0, j))]
    args = (a, b)
    if residual is not None:
        in_specs.append(pl.BlockSpec((tm, tn), lambda i, j: (i, j)))
        args = (a, b, residual)
    return pl.pallas_call(
        _mm_body if residual is None else _mm_add_body,
        out_shape=jax.ShapeDtypeStruct((m, n), out_dtype),
        grid=(m // tm, n // tn),
        in_specs=in_specs,
        out_specs=pl.BlockSpec((tm, tn), lambda i, j: (i, j)),
        compiler_params=_cp(("parallel", "arbitrary")),
        name="matmul",
    )(*args)


def _mm_res_body(a_ref, b_ref, r_ref, o_ref, acc_ref):
    k = pl.program_id(2)

    @pl.when(k == 0)
    def _():
        acc_ref[...] = r_ref[...]

    acc_ref[...] += _dot(a_ref[...], b_ref[...])

    @pl.when(k == pl.num_programs(2) - 1)
    def _():
        o_ref[...] = acc_ref[...]


def matmul_residual(a, b, res, tm, tn, tk):
    m, kdim = a.shape
    n = b.shape[1]
    tm, tn, tk = min(tm, m), min(tn, n), min(tk, kdim)
    return pl.pallas_call(
        _mm_res_body,
        out_shape=jax.ShapeDtypeStruct((m, n), F32),
        grid=(m // tm, n // tn, kdim // tk),
        in_specs=[pl.BlockSpec((tm, tk), lambda i, j, k: (i, k)),
                  pl.BlockSpec((tk, tn), lambda i, j, k: (k, j)),
                  pl.BlockSpec((tm, tn), lambda i, j, k: (i, j))],
        out_specs=pl.BlockSpec((tm, tn), lambda i, j, k: (i, j)),
        scratch_shapes=[pltpu.VMEM((tm, tn), F32)],
        compiler_params=_cp(("parallel", "parallel", "arbitrary")),
        name="matmul_residual",
    )(a, b, res)


def _hgrn_body(lbl_ref, gain_ref, q_ref, f_ref, i_ref, g_ref, o_ref, st_ref, *, layer, nchunks):
    c = pl.program_id(2)

    @pl.when(c == 0)
    def _():
        st_ref[...] = jnp.zeros_like(st_ref)

    logits = lbl_ref[...]
    e = jnp.exp(logits - jnp.max(logits, axis=0, keepdims=True))
    p = e / jnp.sum(e, axis=0, keepdims=True)
    lb = jnp.sum(p[0:layer + 1], axis=0, keepdims=True) - p[0:1]
    gain = gain_ref[...]
    tril = _tril(HG_CHUNK)
    row = lax.broadcasted_iota(I32, (HG_CHUNK, 1), 0)
    row_in_sub = row % HG_SUB
    nsub = HG_CHUNK // HG_SUB

    def chunk(ci, carry):
        sl = pl.ds(pl.multiple_of(ci * HG_CHUNK, HG_CHUNK), HG_CHUNK)
        q = _silu(q_ref[sl, :])
        v = _silu(i_ref[sl, :])
        f = lb + (1.0 - lb) * jax.nn.sigmoid(f_ref[sl, :])
        logf = jnp.log(f)
        k = 1.0 - f
        b = _dot(tril, logf, HIGHEST)
        st = st_ref[...]

        o = _dot_nt((q * jnp.exp(b)).astype(BF16), st.astype(BF16))

        intra = jnp.zeros((HG_CHUNK, HG_D), F32)
        for d in range(HG_SUB):
            ks = k if d == 0 else pltpu.roll(k, d, 0)
            bs = b if d == 0 else pltpu.roll(b, d, 0)
            vs = v if d == 0 else pltpu.roll(v, d, 0)
            expo = jnp.where(row_in_sub >= d, b - bs, -jnp.inf)
            w = jnp.sum(q * ks * jnp.exp(expo), axis=-1, keepdims=True)
            intra = intra + w * vs
        o = o + intra

        parts = [jnp.zeros((HG_SUB, HG_D), F32)]
        for si in range(1, nsub):
            lo = si * HG_SUB
            r = b[lo - 1:lo, :]
            qi = (q[lo:lo + HG_SUB] * jnp.exp(b[lo:lo + HG_SUB] - r)).astype(BF16)
            kj = (k[0:lo] * jnp.exp(r - b[0:lo])).astype(BF16)
            sc = _dot_nt(qi, kj)
            parts.append(_dot(sc.astype(BF16), v[0:lo].astype(BF16)))
        o = o + jnp.concatenate(parts, axis=0)

        b_last = b[HG_CHUNK - 1:HG_CHUNK, :]
        kd = (k * jnp.exp(b_last - b)).astype(BF16)
        st_ref[...] = st * jnp.exp(b_last) + _dot_tn(v.astype(BF16), kd)

        og = o * jax.nn.sigmoid(g_ref[sl, :])
        ms = jnp.mean(og * og, axis=-1, keepdims=True)
        o_ref[sl, :] = (og * lax.rsqrt(ms + EPS) * gain).astype(o_ref.dtype)
        return carry

    lax.fori_loop(0, nchunks, chunk, 0)


def hgrn2(p_hg, lb_logits, norm_gain, layer, batch, seq, tt=256):
    tt = min(tt, seq)
    nt = seq // tt
    hb = HG_HEADS

    def col(sec):
        return lambda b, h, c: (b * nt + c, sec * hb + h)

    return pl.pallas_call(
        functools.partial(_hgrn_body, layer=layer, nchunks=tt // HG_CHUNK),
        out_shape=jax.ShapeDtypeStruct((batch * seq, BRANCH), BF16),
        grid=(batch, hb, nt),
        in_specs=[pl.BlockSpec((DEPTH, HG_D), lambda b, h, c: (0, h)),
                  pl.BlockSpec((1, HG_D), lambda b, h, c: (0, h)),
                  pl.BlockSpec((tt, HG_D), col(0)),
                  pl.BlockSpec((tt, HG_D), col(1)),
                  pl.BlockSpec((tt, HG_D), col(2)),
                  pl.BlockSpec((tt, HG_D), col(3))],
        out_specs=pl.BlockSpec((tt, HG_D), lambda b, h, c: (b * nt + c, h)),
        scratch_shapes=[pltpu.VMEM((HG_D, HG_D), F32)],
        compiler_params=_cp(("parallel", "parallel", "arbitrary")),
        name="hgrn2",
    )(lb_logits.astype(F32), norm_gain.reshape(1, BRANCH).astype(F32), p_hg, p_hg, p_hg, p_hg)


def _mamba_body(z_ref, xbc_ref, dt_ref, cw_ref, cb_ref, dtb_ref, alog_ref, dsk_ref, gain_ref,
                o_ref, prev_ref, st_ref):
    c = pl.program_id(1)
    L = MB_CHUNK

    @pl.when(c == 0)
    def _():
        prev_ref[...] = jnp.zeros_like(prev_ref)
        st_ref[...] = jnp.zeros_like(st_ref)

    x = xbc_ref[...]
    prev = prev_ref[...]
    row = lax.broadcasted_iota(I32, (L, 1), 0)
    cw = cw_ref[...]
    acc = x * cw[MB_CONV - 1:MB_CONV, :] + cb_ref[...]
    for j in range(1, MB_CONV):
        sh = jnp.where(row >= j, pltpu.roll(x, j, 0), pltpu.roll(prev, j, 0))
        acc = acc + sh * cw[MB_CONV - 1 - j:MB_CONV - j, :]
    prev_ref[...] = x
    xbc = _silu(acc)
    xs = xbc[:, 0:BRANCH]
    bm = xbc[:, BRANCH:BRANCH + MB_GROUPS * MB_N]
    cm = xbc[:, BRANCH + MB_GROUPS * MB_N:MB_CONV_DIM]

    raw = dt_ref[...] + dtb_ref[...]
    dt = jnp.maximum(raw, 0.0) + jnp.log1p(jnp.exp(-jnp.abs(raw)))
    a = -jnp.exp(alog_ref[...]) * dt
    tril = _tril(L)
    a_cs = _dot(tril, a, HIGHEST)
    a_cs_t = a_cs.T
    causal = tril > 0.5
    dsk = dsk_ref[...]

    hpg = MB_HEADS // MB_GROUPS
    ys = []
    for g in range(MB_GROUPS):
        bg = bm[:, g * MB_N:(g + 1) * MB_N]
        cg = cm[:, g * MB_N:(g + 1) * MB_N]
        cb = _dot_nt(cg.astype(BF16), bg.astype(BF16))
        for hh in range(hpg):
            h = g * hpg + hh
            acol = a_cs[:, h:h + 1]
            arow = a_cs_t[h:h + 1, :]
            lmat = jnp.exp(jnp.where(causal, acol - arow, -jnp.inf))
            xh = xs[:, h * MB_P:(h + 1) * MB_P]
            xdt = (xh * dt[:, h:h + 1]).astype(BF16)
            y = _dot((cb * lmat).astype(BF16), xdt)
            st = st_ref[h]
            y = y + _dot(cg.astype(BF16), st.astype(BF16)) * jnp.exp(acol)
            a_last = a_cs[L - 1:L, h:h + 1]
            bdec = (bg * jnp.exp(a_last - acol)).astype(BF16)
            st_ref[h] = st * jnp.exp(a_last) + _dot_tn(bdec, xdt)
            ys.append(y + dsk[:, h:h + 1] * xh)
    y = jnp.concatenate(ys, axis=1)
    y = y * _silu(z_ref[...])
    gw = BRANCH // MB_GROUPS
    outs = []
    for g in range(MB_GROUPS):
        yg = y[:, g * gw:(g + 1) * gw]
        ms = jnp.mean(yg * yg, axis=-1, keepdims=True)
        outs.append(yg * lax.rsqrt(ms + EPS))
    o_ref[...] = (jnp.concatenate(outs, axis=1) * gain_ref[...]).astype(o_ref.dtype)


def _pad_lanes(v, n=LANES):
    v = v.reshape(1, -1).astype(F32)
    return jnp.pad(v, ((0, 0), (0, n - v.shape[1])))


def mamba2(p_z, p_xbc, p_misc, conv_w, conv_b, dt_bias, a_log, d_skip, norm_gain, batch, seq):
    nc = seq // MB_CHUNK
    L = MB_CHUNK
    full = lambda shape: pl.BlockSpec(shape, lambda b, c: (0, 0))
    return pl.pallas_call(
        _mamba_body,
        out_shape=jax.ShapeDtypeStruct((batch * seq, BRANCH), BF16),
        grid=(batch, nc),
        in_specs=[pl.BlockSpec((L, BRANCH), lambda b, c: (b * nc + c, 0)),
                  pl.BlockSpec((L, MB_CONV_DIM), lambda b, c: (b * nc + c, 0)),
                  pl.BlockSpec((L, LANES), lambda b, c: (b * nc + c, MISC_DT // LANES)),
                  full((MB_CONV, MB_CONV_DIM)), full((1, MB_CONV_DIM)),
                  full((1, LANES)), full((1, LANES)), full((1, LANES)), full((1, BRANCH))],
        out_specs=pl.BlockSpec((L, BRANCH), lambda b, c: (b * nc + c, 0)),
        scratch_shapes=[pltpu.VMEM((L, MB_CONV_DIM), F32),
                        pltpu.VMEM((MB_HEADS, MB_N, MB_P), F32)],
        compiler_params=_cp(("parallel", "arbitrary")),
        name="mamba2",
    )(p_z, p_xbc, p_misc, conv_w.astype(F32), conv_b.reshape(1, -1).astype(F32),
      _pad_lanes(dt_bias), _pad_lanes(a_log), _pad_lanes(d_skip), norm_gain.reshape(1, BRANCH).astype(F32))


def _fox_cum_body(f_ref, bias_ref, o_ref, carry_ref, *, tt):
    c = pl.program_id(1)

    @pl.when(c == 0)
    def _():
        carry_ref[...] = jnp.zeros_like(carry_ref)

    logf = jax.nn.log_sigmoid(f_ref[...] + bias_ref[...])
    cum = _dot(_tril(tt), logf, HIGHEST) + carry_ref[...]
    o_ref[...] = cum
    carry_ref[...] = cum[tt - 1:tt, :]


def fox_cumsum(p_misc, f_bias, batch, seq, tt=512):
    tt = min(tt, seq)
    nt = seq // tt
    return pl.pallas_call(
        functools.partial(_fox_cum_body, tt=tt),
        out_shape=jax.ShapeDtypeStruct((batch * seq, LANES), F32),
        grid=(batch, nt),
        in_specs=[pl.BlockSpec((tt, LANES), lambda b, c: (b * nt + c, MISC_FF // LANES)),
                  pl.BlockSpec((1, LANES), lambda b, c: (0, 0))],
        out_specs=pl.BlockSpec((tt, LANES), lambda b, c: (b * nt + c, 0)),
        scratch_shapes=[pltpu.VMEM((1, LANES), F32)],
        compiler_params=_cp(("parallel", "arbitrary")),
        name="fox_cumsum",
    )(p_misc, _pad_lanes(f_bias))


def _fox_body(q_ref, k_ref, v_ref, cq_ref, ck_ref, o_ref, m_ref, l_ref, acc_ref, *, tq, tk):
    i = pl.program_id(1)
    j = pl.program_id(2)
    scale = FX_D ** -0.5

    @pl.when(j == 0)
    def _():
        m_ref[...] = jnp.full_like(m_ref, NEG_BIG)
        l_ref[...] = jnp.zeros_like(l_ref)
        acc_ref[...] = jnp.zeros_like(acc_ref)

    @pl.when(j * tk <= i * tq + tq - 1)
    def _():
        t_glob = i * tq + lax.broadcasted_iota(I32, (tq, 1), 0)
        s_glob = j * tk + lax.broadcasted_iota(I32, (1, tk), 1)
        causal = s_glob <= t_glob
        cq = cq_ref[...]
        ck = ck_ref[0]
        for h in range(FX_HEADS):
            hs = slice(h * FX_D, (h + 1) * FX_D)
            s = _dot_nt(q_ref[:, hs], k_ref[:, hs]) * scale + (cq[:, h:h + 1] - ck[h:h + 1, :])
            s = jnp.where(causal, s, -jnp.inf)
            m_prev = m_ref[h][:, 0:1]
            m_new = jnp.maximum(m_prev, jnp.max(s, axis=1, keepdims=True))
            alpha = jnp.exp(m_prev - m_new)
            p = jnp.exp(s - m_new)
            l_new = alpha * l_ref[h][:, 0:1] + jnp.sum(p, axis=1, keepdims=True)
            acc_ref[h] = alpha * acc_ref[h] + _dot(p.astype(BF16), v_ref[:, hs])
            m_ref[h] = jnp.broadcast_to(m_new, (tq, LANES))
            l_ref[h] = jnp.broadcast_to(l_new, (tq, LANES))

    @pl.when(j == pl.num_programs(2) - 1)
    def _():
        for h in range(FX_HEADS):
            o_ref[:, h * FX_D:(h + 1) * FX_D] = (acc_ref[h] / l_ref[h][:, 0:1]).astype(o_ref.dtype)


def fox_attention(p_fx, cum, cum_t, batch, seq, tq=256, tk=512):
    tq, tk = min(tq, seq), min(tk, seq)
    nq, nk = seq // tq, seq // tk

    def kv_blk(i, j):
        return jnp.minimum(j, (i * tq + tq - 1) // tk)

    return pl.pallas_call(
        functools.partial(_fox_body, tq=tq, tk=tk),
        out_shape=jax.ShapeDtypeStruct((batch * seq, BRANCH), BF16),
        grid=(batch, nq, nk),
        in_specs=[pl.BlockSpec((tq, BRANCH), lambda b, i, j: (b * nq + i, 0)),
                  pl.BlockSpec((tk, BRANCH), lambda b, i, j: (b * nk + kv_blk(i, j), 1)),
                  pl.BlockSpec((tk, BRANCH), lambda b, i, j: (b * nk + kv_blk(i, j), 2)),
                  pl.BlockSpec((tq, LANES), lambda b, i, j: (b * nq + i, 0)),
                  pl.BlockSpec((1, FX_HEADS, tk), lambda b, i, j: (b, 0, kv_blk(i, j)))],
        out_specs=pl.BlockSpec((tq, BRANCH), lambda b, i, j: (b * nq + i, 0)),
        scratch_shapes=[pltpu.VMEM((FX_HEADS, tq, LANES), F32),
                        pltpu.VMEM((FX_HEADS, tq, LANES), F32),
                        pltpu.VMEM((FX_HEADS, tq, FX_D), F32)],
        compiler_params=_cp(("parallel", "parallel", "arbitrary")),
        name="fox_attention",
    )(p_fx, p_fx, p_fx, cum, cum_t)


def _qlat_body(q_ref, wuk_ref, o_ref):
    o_ref[...] = (_dot_nt(q_ref[...], wuk_ref[...]) * DS_HEAD_DIM ** -0.5).astype(o_ref.dtype)


def dsa_qlat(qcat, w_uk, tm=1024):
    m = qcat.shape[0]
    tm = min(tm, m)
    return pl.pallas_call(
        _qlat_body,
        out_shape=jax.ShapeDtypeStruct((m, DS_HEADS * DS_KV_LORA), BF16),
        grid=(m // tm, DS_HEADS),
        in_specs=[pl.BlockSpec((tm, DS_HEAD_DIM), lambda i, h: (i, h)),
                  pl.BlockSpec((DS_KV_LORA, DS_HEAD_DIM), lambda i, h: (0, h))],
        out_specs=pl.BlockSpec((tm, DS_KV_LORA), lambda i, h: (i, h)),
        compiler_params=_cp(("parallel", "arbitrary")),
        name="dsa_qlat",
    )(qcat, w_uk)


def _dsa_body(qi_ref, misc_ref, ql_ref, kidx_ref, ckv_ref, wuv_ref, o_ref,
              keys_ref, m_ref, l_ref, acc_ref, *, tk, topk):
    i = pl.program_id(1)
    tq = DS_QBLOCK
    nkb = (i * tq + tq - 1) // tk + 1
    t_glob = i * tq + lax.broadcasted_iota(I32, (tq, 1), 0)
    col = lax.broadcasted_iota(I32, (1, tk), 1)
    w = misc_ref[:, DS_IDX_DIM:DS_IDX_DIM + DS_IDX_HEADS] * (DS_IDX_HEADS ** -0.5 * DS_IDX_DIM ** -0.5)
    qh = [qi_ref[:, h * DS_IDX_DIM:(h + 1) * DS_IDX_DIM] for h in range(DS_IDX_HEADS)]

    def score_body(j, carry):
        kb = kidx_ref[pl.ds(pl.multiple_of(j * tk, tk), tk), :]
        sc = jnp.zeros((tq, tk), F32)
        for h in range(DS_IDX_HEADS):
            sc = sc + jnp.maximum(_dot_nt(qh[h], kb), 0.0) * w[:, h:h + 1]
        sc = jnp.where(j * tk + col <= t_glob, sc, -jnp.inf)
        bits = pltpu.bitcast(sc, I32)
        keys_ref[j] = jnp.where(bits < 0, bits ^ 0x7FFFFFFF, bits)
        return carry

    lax.fori_loop(0, nkb, score_body, 0)

    def count_ge(cand):
        def body(j, acc):
            ge = jnp.where(keys_ref[j] >= cand, 1.0, 0.0)
            part = ge[:, 0:LANES]
            for cc in range(1, tk // LANES):
                part = part + ge[:, cc * LANES:(cc + 1) * LANES]
            return acc + part
        acc = lax.fori_loop(0, nkb, body, jnp.zeros((tq, LANES), F32))
        return jnp.sum(acc, axis=1, keepdims=True)

    kf = float(topk)
    thr0 = jnp.where(count_ge(jnp.zeros((tq, 1), I32)) >= kf, 0, INT_MIN).astype(I32)

    def bit_body(bi, thr):
        cand = thr | lax.shift_left(jnp.int32(1), 30 - bi)
        return jnp.where(count_ge(cand) >= kf, cand, thr)

    thr = lax.fori_loop(0, 31, bit_body, thr0)

    m_ref[...] = jnp.full_like(m_ref, NEG_BIG)
    l_ref[...] = jnp.zeros_like(l_ref)
    acc_ref[...] = jnp.zeros_like(acc_ref)

    def attn_body(j, carry):
        kv = ckv_ref[pl.ds(pl.multiple_of(j * tk, tk), tk), :]
        sel = (keys_ref[j] >= thr) & (j * tk + col <= t_glob)
        for h in range(DS_HEADS):
            s = _dot_nt(ql_ref[:, h * DS_KV_LORA:(h + 1) * DS_KV_LORA], kv)
            s = jnp.where(sel, s, -jnp.inf)
            m_prev = m_ref[h][:, 0:1]
            m_new = jnp.maximum(m_prev, jnp.max(s, axis=1, keepdims=True))
            alpha = jnp.exp(m_prev - m_new)
            p = jnp.exp(s - m_new)
            l_new = alpha * l_ref[h][:, 0:1] + jnp.sum(p, axis=1, keepdims=True)
            acc_ref[h] = alpha * acc_ref[h] + _dot(p.astype(BF16), kv)
            m_ref[h] = jnp.broadcast_to(m_new, (tq, LANES))
            l_ref[h] = jnp.broadcast_to(l_new, (tq, LANES))
        return carry

    lax.fori_loop(0, nkb, attn_body, 0)

    for h in range(DS_HEADS):
        o_lat = (acc_ref[h] / l_ref[h][:, 0:1]).astype(BF16)
        o_ref[:, h * DS_HEAD_DIM:(h + 1) * DS_HEAD_DIM] = _dot(
            o_lat, wuv_ref[:, h * DS_HEAD_DIM:(h + 1) * DS_HEAD_DIM]).astype(o_ref.dtype)


def dsa_attention(qcat, p_misc, q_lat, k_idx, c_kv, w_uv, batch, seq, tk=512):
    tk = min(tk, seq)
    tq = DS_QBLOCK
    nq = seq // tq
    topk = min(DS_TOPK_MAX, seq // 4)
    assert tk >= topk
    return pl.pallas_call(
        functools.partial(_dsa_body, tk=tk, topk=topk),
        out_shape=jax.ShapeDtypeStruct((batch * seq, BRANCH), BF16),
        grid=(batch, nq),
        in_specs=[pl.BlockSpec((tq, DS_IDX_HEADS * DS_IDX_DIM), lambda b, i: (b * nq + i, 1)),
                  pl.BlockSpec((tq, LANES), lambda b, i: (b * nq + i, MISC_IDX // LANES)),
                  pl.BlockSpec((tq, DS_HEADS * DS_KV_LORA), lambda b, i: (b * nq + i, 0)),
                  pl.BlockSpec((seq, DS_IDX_DIM), lambda b, i: (b, 0)),
                  pl.BlockSpec((seq, DS_KV_LORA), lambda b, i: (b, 0)),
                  pl.BlockSpec((DS_KV_LORA, BRANCH), lambda b, i: (0, 0))],
        out_specs=pl.BlockSpec((tq, BRANCH), lambda b, i: (b * nq + i, 0)),
        scratch_shapes=[pltpu.VMEM((seq // tk, tq, tk), I32),
                        pltpu.VMEM((DS_HEADS, tq, LANES), F32),
                        pltpu.VMEM((DS_HEADS, tq, LANES), F32),
                        pltpu.VMEM((DS_HEADS, tq, DS_KV_LORA), F32)],
        compiler_params=_cp(("parallel", "arbitrary")),
        name="dsa_attention",
    )(qcat, p_misc, q_lat, k_idx, c_kv, w_uv)


def _merge_body(h_ref, wg_ref, wb_ref, ya_ref, yb_ref, yc_ref, yd_ref, o_ref, acc_ref):
    n = pl.program_id(2)
    gate = jax.nn.sigmoid(_dot(h_ref[...], wg_ref[0]))
    for idx, y_ref in enumerate((ya_ref, yb_ref, yc_ref, yd_ref)):
        @pl.when(n == idx)
        def _(y_ref=y_ref, idx=idx):
            contrib = gate * _dot(y_ref[...], wb_ref[0])
            if idx == 0:
                acc_ref[...] = contrib
            else:
                acc_ref[...] += contrib

    @pl.when(n == pl.num_programs(2) - 1)
    def _():
        o_ref[...] = acc_ref[...].astype(o_ref.dtype)


def gated_merge(h, w_gate, w_branch, ys, tm=1024, tn=256):
    m = h.shape[0]
    tm = min(tm, m)
    ymap = lambda i, j, n: (i, 0)
    return pl.pallas_call(
        _merge_body,
        out_shape=jax.ShapeDtypeStruct((m, D_MODEL), BF16),
        grid=(m // tm, D_MODEL // tn, 4),
        in_specs=[pl.BlockSpec((tm, D_MODEL), lambda i, j, n: (i, 0)),
                  pl.BlockSpec((1, D_MODEL, tn), lambda i, j, n: (n, 0, j)),
                  pl.BlockSpec((1, BRANCH, tn), lambda i, j, n: (n, 0, j)),
                  pl.BlockSpec((tm, BRANCH), ymap), pl.BlockSpec((tm, BRANCH), ymap),
                  pl.BlockSpec((tm, BRANCH), ymap), pl.BlockSpec((tm, BRANCH), ymap)],
        out_specs=pl.BlockSpec((tm, tn), lambda i, j, n: (i, j)),
        scratch_shapes=[pltpu.VMEM((tm, tn), F32)],
        compiler_params=_cp(("parallel", "parallel", "arbitrary")),
        name="gated_merge",
    )(h, w_gate, w_branch, *ys)


def _ffn_up_body(h_ref, halo_ref, wg_ref, wu_ref, cw_ref, o_ref, *, tiles_per_seq):
    i = pl.program_id(0)
    g = _dot(h_ref[...], wg_ref[...])
    u = _dot(h_ref[...], wu_ref[...])
    gh = _dot(halo_ref[...], wg_ref[...])
    gh = gh * jnp.where(i % tiles_per_seq == 0, 0.0, 1.0)
    cw = cw_ref[...]
    row8 = lax.broadcasted_iota(I32, (8, 1), 0)
    y = g * cw[FFN_CONV - 1:FFN_CONV, :]
    for j in range(1, FFN_CONV):
        rolled = pltpu.roll(g, j, 0)
        head = jnp.where(row8 < j, pltpu.roll(gh, j, 0), rolled[0:8])
        shifted = jnp.concatenate([head, rolled[8:]], axis=0)
        y = y + shifted * cw[FFN_CONV - 1 - j:FFN_CONV - j, :]
    o_ref[...] = (_silu(y) * u).astype(o_ref.dtype)


def ffn_up(h, w_gate, w_up, conv_w, seq, tm=1024, tn=256):
    m = h.shape[0]
    tm = min(tm, seq)
    dff = w_gate.shape[1]
    return pl.pallas_call(
        functools.partial(_ffn_up_body, tiles_per_seq=seq // tm),
        out_shape=jax.ShapeDtypeStruct((m, dff), BF16),
        grid=(m // tm, dff // tn),
        in_specs=[pl.BlockSpec((tm, D_MODEL), lambda i, j: (i, 0)),
                  pl.BlockSpec((8, D_MODEL), lambda i, j: (jnp.maximum(i * (tm // 8) - 1, 0), 0)),
                  pl.BlockSpec((D_MODEL, tn), lambda i, j: (0, j)),
                  pl.BlockSpec((D_MODEL, tn), lambda i, j: (0, j)),
                  pl.BlockSpec((FFN_CONV, tn), lambda i, j: (0, j))],
        out_specs=pl.BlockSpec((tm, tn), lambda i, j: (i, j)),
        compiler_params=_cp(("parallel", "arbitrary")),
        name="ffn_up",
    )(h, h, w_gate, w_up, conv_w.astype(F32))


def _in_proj_weights(w):
    o = 0
    secs = {}
    for name, width in (("hq", 1024), ("hf", 1024), ("hi", 1024), ("hg", 1024), ("cq", 768), ("ckv", 512),
                        ("kidx", 64), ("widx", 16), ("z", 1024), ("xbc", MB_CONV_DIM), ("dt", 16),
                        ("fq", 1024), ("fk", 1024), ("fv", 1024), ("ff", 8)):
        secs[name] = w[:, o:o + width]
        o += width
    zeros = lambda n: jnp.zeros((w.shape[0], n), w.dtype)
    misc = jnp.concatenate([secs["ckv"], secs["kidx"], secs["widx"], zeros(LANES - 80),
                            secs["dt"], zeros(LANES - 16), secs["ff"], zeros(LANES - 8)], axis=1)
    return dict(hg=w[:, 0:4096].astype(BF16), cq=secs["cq"].astype(BF16), misc=misc.astype(BF16),
                z=secs["z"].astype(BF16), xbc=secs["xbc"].astype(BF16),
                fx=jnp.concatenate([secs["fq"], secs["fk"], secs["fv"]], axis=1).astype(BF16))


def _layer(x, l, batch, seq, prm):
    m = batch * seq
    h = rmsnorm(x, prm["attn_norm"][l], BF16)
    w = _in_proj_weights(prm["w_in"][l])
    p_hg = matmul(h, w["hg"], F32, 1024, 512)
    p_cq = matmul(h, w["cq"], F32, 1024, 768)
    p_misc = matmul(h, w["misc"], F32, 1024, MISC_W)
    p_z = matmul(h, w["z"], F32, 1024, 512)
    p_xbc = matmul(h, w["xbc"], F32, 1024, 512)
    p_fx = matmul(h, w["fx"], BF16, 1024, 512)

    y_a = hgrn2(p_hg, prm["hgrn_lb_logits"], prm["hgrn_norm"][l], l, batch, seq)

    c_q = rmsnorm(p_cq, prm["dsa_q_norm"][l], BF16)
    c_kv = rmsnorm(p_misc, prm["dsa_kv_norm"][l], BF16, width=DS_KV_LORA, col_block=0)
    k_idx = p_misc[:, MISC_IDX:MISC_IDX + DS_IDX_DIM].astype(BF16)
    w_q = jnp.concatenate([prm["dsa_w_uq"][l], prm["dsa_w_iq"][l]], axis=1).astype(BF16)
    qcat = matmul(c_q, w_q, BF16, 1024, 1024)
    q_lat = dsa_qlat(qcat, prm["dsa_w_uk"][l].astype(BF16))
    y_b = dsa_attention(qcat, p_misc, q_lat, k_idx, c_kv, prm["dsa_w_uv"][l].astype(BF16), batch, seq)

    y_c = mamba2(p_z, p_xbc, p_misc, prm["ssm_conv_w"][l], prm["ssm_conv_b"][l], prm["ssm_dt_bias"][l],
                 prm["ssm_a_log"][l], prm["ssm_d"][l], prm["ssm_norm"][l], batch, seq)

    cum = fox_cumsum(p_misc, prm["fox_f_bias"][l], batch, seq)
    cum_t = cum.reshape(batch, seq, LANES)[:, :, :FX_HEADS].transpose(0, 2, 1)
    y_d = fox_attention(p_fx, cum, cum_t, batch, seq)

    merged = gated_merge(h, prm["w_gate"][l].astype(BF16), prm["w_branch"][l].astype(BF16),
                         (y_a, y_b, y_c, y_d))
    x = matmul(merged, prm["w_out"][l].astype(BF16), F32, 1024, 512, residual=x)

    h2 = rmsnorm(x, prm["ffn_norm"][l], BF16)
    padc = ((0, 0), (0, D_FF_PAD - D_FF))
    act = ffn_up(h2, jnp.pad(prm["ffn_w_gate"][l].astype(BF16), padc),
                 jnp.pad(prm["ffn_w_up"][l].astype(BF16), padc), jnp.pad(prm["ffn_conv"][l], padc), seq)
    w_down = jnp.pad(prm["ffn_w_down"][l].astype(BF16), ((0, D_FF_PAD - D_FF), (0, 0)))
    return matmul_residual(act, w_down, x, 1024, 1024, 1024)


def kernel(x, attn_norm, ffn_norm, final_norm, w_in, hgrn_lb_logits, hgrn_norm, dsa_q_norm, dsa_kv_norm,
           dsa_w_uq, dsa_w_iq, dsa_w_uk, dsa_w_uv, ssm_conv_w, ssm_conv_b, ssm_dt_bias, ssm_a_log, ssm_d,
           ssm_norm, fox_f_bias, w_gate, w_branch, w_out, ffn_w_gate, ffn_w_up, ffn_conv, ffn_w_down):
    batch, seq, d = x.shape
    prm = dict(attn_norm=attn_norm, ffn_norm=ffn_norm, w_in=w_in, hgrn_lb_logits=hgrn_lb_logits,
               hgrn_norm=hgrn_norm, dsa_q_norm=dsa_q_norm, dsa_kv_norm=dsa_kv_norm, dsa_w_uq=dsa_w_uq,
               dsa_w_iq=dsa_w_iq, dsa_w_uk=dsa_w_uk, dsa_w_uv=dsa_w_uv, ssm_conv_w=ssm_conv_w,
               ssm_conv_b=ssm_conv_b, ssm_dt_bias=ssm_dt_bias, ssm_a_log=ssm_a_log, ssm_d=ssm_d,
               ssm_norm=ssm_norm, fox_f_bias=fox_f_bias, w_gate=w_gate, w_branch=w_branch, w_out=w_out,
               ffn_w_gate=ffn_w_gate, ffn_w_up=ffn_w_up, ffn_conv=ffn_conv, ffn_w_down=ffn_w_down)
    xf = x.reshape(batch * seq, d)
    for l in range(DEPTH):
        xf = _layer(xf, l, batch, seq, prm)
    return rmsnorm(xf, final_norm, x.dtype).reshape(batch, seq, d)
```

```python
import functools

import jax
import jax.numpy as jnp
from jax import lax
from jax.experimental import pallas as pl
from jax.experimental.pallas import tpu as pltpu

F32, BF16, I32 = jnp.float32, jnp.bfloat16, jnp.int32
HIGHEST = lax.Precision.HIGHEST

D_MODEL = 4096
DEPTH = 2
BRANCH = 1024
HG_HEADS, HG_D, HG_CHUNK, HG_SUB = 8, 128, 64, 16
DS_HEADS, DS_HEAD_DIM, DS_Q_LORA, DS_KV_LORA = 8, 128, 768, 512
DS_IDX_HEADS, DS_IDX_DIM, DS_TOPK_MAX = 16, 64, 256
DSA_SOFTMAX_ROWS = 64
MB_HEADS, MB_P, MB_N, MB_GROUPS, MB_CONV, MB_CHUNK = 16, 64, 128, 2, 4, 128
MB_CONV_DIM = BRANCH + 2 * MB_GROUPS * MB_N
FX_HEADS, FX_D = 8, 128
D_FF = 11008
D_FF_PAD = 11264
FFN_CONV = 3
EPS = 1e-6
LANES = 128
NEG_BIG = -1e30
LOG2E = 1.4426950408889634
INT_MIN = -(2 ** 31)

MISC_CKV = 0
MISC_IDX = 512
MISC_DT = 640
MISC_FF = 768
MISC_W = 896


def _cp(sem, vmem_mb=48):
    return pltpu.CompilerParams(dimension_semantics=sem, vmem_limit_bytes=vmem_mb * 2 ** 20)


def _dot(a, b, precision=None):
    return jnp.dot(a, b, preferred_element_type=F32, precision=precision)


def _dot_nt(a, b):
    return lax.dot_general(a, b, (((1,), (1,)), ((), ())), preferred_element_type=F32)


def _dot_tn(a, b):
    return lax.dot_general(a, b, (((0,), (0,)), ((), ())), preferred_element_type=F32)


def _tril(n):
    r = lax.broadcasted_iota(I32, (n, n), 0)
    c = lax.broadcasted_iota(I32, (n, n), 1)
    return (r >= c).astype(F32)


def _silu(x):
    return x * jax.nn.sigmoid(x)


def _rmsnorm_body(x_ref, g_ref, o_ref):
    x = x_ref[...].astype(F32)
    ms = jnp.mean(x * x, axis=-1, keepdims=True)
    o_ref[...] = (x * lax.rsqrt(ms + EPS) * g_ref[...]).astype(o_ref.dtype)


def rmsnorm(x, gain, out_dtype, width=None, col_block=0, tm=256):
    m = x.shape[0]
    width = x.shape[1] if width is None else width
    return pl.pallas_call(
        _rmsnorm_body,
        out_shape=jax.ShapeDtypeStruct((m, width), out_dtype),
        grid=(m // tm,),
        in_specs=[pl.BlockSpec((tm, width), lambda i: (i, col_block)),
                  pl.BlockSpec((1, width), lambda i: (0, 0))],
        out_specs=pl.BlockSpec((tm, width), lambda i: (i, 0)),
        compiler_params=_cp(("parallel",)),
        name="rmsnorm",
    )(x, gain.reshape(1, width).astype(F32))


def _mm_body(a_ref, b_ref, o_ref):
    o_ref[...] = _dot(a_ref[...], b_ref[...]).astype(o_ref.dtype)


def _mm_add_body(a_ref, b_ref, r_ref, o_ref):
    o_ref[...] = (r_ref[...] + _dot(a_ref[...], b_ref[...])).astype(o_ref.dtype)


def _mm_scale_body(a_ref, b_ref, s_ref, o_ref):
    o_ref[...] = (_dot(a_ref[...], b_ref[...]) * s_ref[...]).astype(o_ref.dtype)


def matmul(a, b, out_dtype, tm, tn, residual=None, col_scale=None, n=None):
    m, k = a.shape
    n = b.shape[1] if n is None else n
    tm, tn = min(tm, m), min(tn, n)
    in_specs = [pl.BlockSpec((tm, k), lambda i, j: (i, 0)),
                pl.BlockSpec((k, tn), lambda i, j: (0, j))]
    args, body = (a, b), _mm_body
    if residual is not None:
        in_specs.append(pl.BlockSpec((tm, tn), lambda i, j: (i, j)))
        args, body = (a, b, residual), _mm_add_body
    elif col_scale is not None:
        in_specs.append(pl.BlockSpec((1, tn), lambda i, j: (0, j)))
        args, body = (a, b, col_scale), _mm_scale_body
    return pl.pallas_call(
        body,
        out_shape=jax.ShapeDtypeStruct((m, n), out_dtype),
        grid=(m // tm, n // tn),
        in_specs=in_specs,
        out_specs=pl.BlockSpec((tm, tn), lambda i, j: (i, j)),
        compiler_params=_cp(("parallel", "arbitrary")),
        name="matmul",
    )(*args)


def _mm_res_body(a_ref, b_ref, r_ref, o_ref, acc_ref):
    k = pl.program_id(2)

    @pl.when(k == 0)
    def _():
        acc_ref[...] = r_ref[...]

    acc_ref[...] += _dot(a_ref[...], b_ref[...])

    @pl.when(k == pl.num_programs(2) - 1)
    def _():
        o_ref[...] = acc_ref[...]


def matmul_residual(a, b, res, tm, tn, tk):
    m, kdim = a.shape
    n = b.shape[1]
    tm, tn, tk = min(tm, m), min(tn, n), min(tk, kdim)
    return pl.pallas_call(
        _mm_res_body,
        out_shape=jax.ShapeDtypeStruct((m, n), F32),
        grid=(m // tm, n // tn, kdim // tk),
        in_specs=[pl.BlockSpec((tm, tk), lambda i, j, k: (i, k)),
                  pl.BlockSpec((tk, tn), lambda i, j, k: (k, j)),
                  pl.BlockSpec((tm, tn), lambda i, j, k: (i, j))],
        out_specs=pl.BlockSpec((tm, tn), lambda i, j, k: (i, j)),
        scratch_shapes=[pltpu.VMEM((tm, tn), F32)],
        compiler_params=_cp(("parallel", "parallel", "arbitrary")),
        name="matmul_residual",
    )(a, b, res)


def _hgrn_body(lbl_ref, gain_ref, q_ref, f_ref, i_ref, g_ref, o_ref, st_ref, *, layer, nchunks):
    c = pl.program_id(2)

    @pl.when(c == 0)
    def _():
        st_ref[...] = jnp.zeros_like(st_ref)

    logits = lbl_ref[...]
    e = jnp.exp(logits - jnp.max(logits, axis=0, keepdims=True))
    p = e / jnp.sum(e, axis=0, keepdims=True)
    lb = jnp.sum(p[0:layer + 1], axis=0, keepdims=True) - p[0:1]
    gain = gain_ref[...]
    tril = _tril(HG_CHUNK)
    row = lax.broadcasted_iota(I32, (HG_CHUNK, 1), 0)
    row_in_sub = row % HG_SUB
    nsub = HG_CHUNK // HG_SUB

    def chunk(ci, carry):
        sl = pl.ds(pl.multiple_of(ci * HG_CHUNK, HG_CHUNK), HG_CHUNK)
        q = _silu(q_ref[sl, :])
        v = _silu(i_ref[sl, :])
        f = lb + (1.0 - lb) * jax.nn.sigmoid(f_ref[sl, :])
        logf = jnp.log(f)
        k = 1.0 - f
        b = _dot(tril, logf, HIGHEST)
        st = st_ref[...]

        o = _dot_nt((q * jnp.exp(b)).astype(BF16), st.astype(BF16))

        intra = jnp.zeros((HG_CHUNK, HG_D), F32)
        for d in range(HG_SUB):
            ks = k if d == 0 else pltpu.roll(k, d, 0)
            bs = b if d == 0 else pltpu.roll(b, d, 0)
            vs = v if d == 0 else pltpu.roll(v, d, 0)
            expo = jnp.where(row_in_sub >= d, b - bs, -jnp.inf)
            w = jnp.sum(q * ks * jnp.exp(expo), axis=-1, keepdims=True)
            intra = intra + w * vs
        o = o + intra

        parts = [jnp.zeros((HG_SUB, HG_D), F32)]
        for si in range(1, nsub):
            lo = si * HG_SUB
            r = b[lo - 1:lo, :]
            qi = (q[lo:lo + HG_SUB] * jnp.exp(b[lo:lo + HG_SUB] - r)).astype(BF16)
            kj = (k[0:lo] * jnp.exp(r - b[0:lo])).astype(BF16)
            sc = _dot_nt(qi, kj)
            parts.append(_dot(sc.astype(BF16), v[0:lo].astype(BF16)))
        o = o + jnp.concatenate(parts, axis=0)

        b_last = b[HG_CHUNK - 1:HG_CHUNK, :]
        kd = (k * jnp.exp(b_last - b)).astype(BF16)
        st_ref[...] = st * jnp.exp(b_last) + _dot_tn(v.astype(BF16), kd)

        og = o * jax.nn.sigmoid(g_ref[sl, :])
        ms = jnp.mean(og * og, axis=-1, keepdims=True)
        o_ref[sl, :] = (og * lax.rsqrt(ms + EPS) * gain).astype(o_ref.dtype)
        return carry

    lax.fori_loop(0, nchunks, chunk, 0, unroll=True)


def hgrn2(p_hg, lb_logits, norm_gain, layer, batch, seq, tt=256):
    tt = min(tt, seq)
    nt = seq // tt
    hb = HG_HEADS

    def col(sec):
        return lambda b, h, c: (b * nt + c, sec * hb + h)

    return pl.pallas_call(
        functools.partial(_hgrn_body, layer=layer, nchunks=tt // HG_CHUNK),
        out_shape=jax.ShapeDtypeStruct((batch * seq, BRANCH), BF16),
        grid=(batch, hb, nt),
        in_specs=[pl.BlockSpec((DEPTH, HG_D), lambda b, h, c: (0, h)),
                  pl.BlockSpec((1, HG_D), lambda b, h, c: (0, h)),
                  pl.BlockSpec((tt, HG_D), col(0)),
                  pl.BlockSpec((tt, HG_D), col(1)),
                  pl.BlockSpec((tt, HG_D), col(2)),
                  pl.BlockSpec((tt, HG_D), col(3))],
        out_specs=pl.BlockSpec((tt, HG_D), lambda b, h, c: (b * nt + c, h)),
        scratch_shapes=[pltpu.VMEM((HG_D, HG_D), F32)],
        compiler_params=_cp(("parallel", "parallel", "arbitrary")),
        name="hgrn2",
    )(lb_logits.astype(F32), norm_gain.reshape(1, BRANCH).astype(F32), p_hg, p_hg, p_hg, p_hg)


def _mamba_body(z_ref, xbc_ref, dt_ref, cw_ref, cb_ref, dtb_ref, alog_ref, dsk_ref, gain_ref,
                o_ref, prev_ref, st_ref):
    c = pl.program_id(1)
    L = MB_CHUNK

    @pl.when(c == 0)
    def _():
        prev_ref[...] = jnp.zeros_like(prev_ref)
        st_ref[...] = jnp.zeros_like(st_ref)

    x = xbc_ref[...]
    prev = prev_ref[...]
    row = lax.broadcasted_iota(I32, (L, 1), 0)
    cw = cw_ref[...]
    acc = x * cw[MB_CONV - 1:MB_CONV, :] + cb_ref[...]
    for j in range(1, MB_CONV):
        sh = jnp.where(row >= j, pltpu.roll(x, j, 0), pltpu.roll(prev, j, 0))
        acc = acc + sh * cw[MB_CONV - 1 - j:MB_CONV - j, :]
    prev_ref[...] = x
    xbc = _silu(acc)
    xs = xbc[:, 0:BRANCH]
    bm = xbc[:, BRANCH:BRANCH + MB_GROUPS * MB_N]
    cm = xbc[:, BRANCH + MB_GROUPS * MB_N:MB_CONV_DIM]

    raw = dt_ref[...] + dtb_ref[...]
    dt = jnp.maximum(raw, 0.0) + jnp.log1p(jnp.exp(-jnp.abs(raw)))
    a = -jnp.exp(alog_ref[...]) * dt
    tril = _tril(L)
    a_cs = _dot(tril, a, HIGHEST)
    a_cs_t = a_cs.T
    causal = tril > 0.5
    dsk = dsk_ref[...]

    hpg = MB_HEADS // MB_GROUPS
    ys = []
    for g in range(MB_GROUPS):
        bg = bm[:, g * MB_N:(g + 1) * MB_N]
        cg = cm[:, g * MB_N:(g + 1) * MB_N]
        cb = _dot_nt(cg.astype(BF16), bg.astype(BF16))
        for hh in range(hpg):
            h = g * hpg + hh
            acol = a_cs[:, h:h + 1]
            arow = a_cs_t[h:h + 1, :]
            lmat = jnp.exp(jnp.where(causal, acol - arow, -jnp.inf))
            xh = xs[:, h * MB_P:(h + 1) * MB_P]
            xdt = (xh * dt[:, h:h + 1]).astype(BF16)
            y = _dot((cb * lmat).astype(BF16), xdt)
            st = st_ref[h]
            y = y + _dot(cg.astype(BF16), st.astype(BF16)) * jnp.exp(acol)
            a_last = a_cs[L - 1:L, h:h + 1]
            bdec = (bg * jnp.exp(a_last - acol)).astype(BF16)
            st_ref[h] = st * jnp.exp(a_last) + _dot_tn(bdec, xdt)
            ys.append(y + dsk[:, h:h + 1] * xh)
    y = jnp.concatenate(ys, axis=1)
    y = y * _silu(z_ref[...])
    gw = BRANCH // MB_GROUPS
    outs = []
    for g in range(MB_GROUPS):
        yg = y[:, g * gw:(g + 1) * gw]
        ms = jnp.mean(yg * yg, axis=-1, keepdims=True)
        outs.append(yg * lax.rsqrt(ms + EPS))
    o_ref[...] = (jnp.concatenate(outs, axis=1) * gain_ref[...]).astype(o_ref.dtype)


def _pad_lanes(v, n=LANES):
    v = v.reshape(1, -1).astype(F32)
    return jnp.pad(v, ((0, 0), (0, n - v.shape[1])))


def mamba2(p_z, p_xbc, p_misc, conv_w, conv_b, dt_bias, a_log, d_skip, norm_gain, batch, seq):
    nc = seq // MB_CHUNK
    L = MB_CHUNK
    full = lambda shape: pl.BlockSpec(shape, lambda b, c: (0, 0))
    return pl.pallas_call(
        _mamba_body,
        out_shape=jax.ShapeDtypeStruct((batch * seq, BRANCH), BF16),
        grid=(batch, nc),
        in_specs=[pl.BlockSpec((L, BRANCH), lambda b, c: (b * nc + c, 0)),
                  pl.BlockSpec((L, MB_CONV_DIM), lambda b, c: (b * nc + c, 0)),
                  pl.BlockSpec((L, LANES), lambda b, c: (b * nc + c, MISC_DT // LANES)),
                  full((MB_CONV, MB_CONV_DIM)), full((1, MB_CONV_DIM)),
                  full((1, LANES)), full((1, LANES)), full((1, LANES)), full((1, BRANCH))],
        out_specs=pl.BlockSpec((L, BRANCH), lambda b, c: (b * nc + c, 0)),
        scratch_shapes=[pltpu.VMEM((L, MB_CONV_DIM), F32),
                        pltpu.VMEM((MB_HEADS, MB_N, MB_P), F32)],
        compiler_params=_cp(("parallel", "arbitrary")),
        name="mamba2",
    )(p_z, p_xbc, p_misc, conv_w.astype(F32), conv_b.reshape(1, -1).astype(F32),
      _pad_lanes(dt_bias), _pad_lanes(a_log), _pad_lanes(d_skip), norm_gain.reshape(1, BRANCH).astype(F32))


def _fox_cum_body(f_ref, bias_ref, qb_ref, kb_ref, carry_ref, *, tt):
    c = pl.program_id(1)

    @pl.when(c == 0)
    def _():
        carry_ref[...] = jnp.zeros_like(carry_ref)

    logf = jax.nn.log_sigmoid(f_ref[...] + bias_ref[...])
    cum = _dot(_tril(tt), logf, HIGHEST) + carry_ref[...]
    carry_ref[...] = cum[tt - 1:tt, :]

    c2 = cum * LOG2E
    lane = lax.broadcasted_iota(I32, (1, LANES), 1)
    ones = jnp.where(lane < 6, 1.0, 0.0)
    for h in range(FX_HEADS):
        col = c2[:, h:h + 1]
        hi = col.astype(BF16).astype(F32)
        r1 = col - hi
        mid = r1.astype(BF16).astype(F32)
        lo = r1 - mid
        qb = jnp.where(lane == 0, hi, jnp.where(lane == 1, mid, jnp.where(lane == 2, lo, ones)))
        kb = jnp.where(lane == 3, -hi, jnp.where(lane == 4, -mid, jnp.where(lane == 5, -lo, ones)))
        qb_ref[:, h * LANES:(h + 1) * LANES] = qb.astype(BF16)
        kb_ref[:, h * LANES:(h + 1) * LANES] = kb.astype(BF16)


def fox_bias_columns(p_misc, f_bias, batch, seq, tt=512):
    tt = min(tt, seq)
    nt = seq // tt
    out = jax.ShapeDtypeStruct((batch * seq, FX_HEADS * LANES), BF16)
    ospec = pl.BlockSpec((tt, FX_HEADS * LANES), lambda b, c: (b * nt + c, 0))
    return pl.pallas_call(
        functools.partial(_fox_cum_body, tt=tt),
        out_shape=(out, out),
        grid=(batch, nt),
        in_specs=[pl.BlockSpec((tt, LANES), lambda b, c: (b * nt + c, MISC_FF // LANES)),
                  pl.BlockSpec((1, LANES), lambda b, c: (0, 0))],
        out_specs=(ospec, ospec),
        scratch_shapes=[pltpu.VMEM((1, LANES), F32)],
        compiler_params=_cp(("parallel", "arbitrary")),
        name="fox_bias_columns",
    )(p_misc, _pad_lanes(f_bias))


def _fox_body(q_ref, qb_ref, k_ref, kb_ref, v_ref, o_ref, m_ref, l_ref, acc_ref, *, tq, tk):
    i = pl.program_id(1)
    j = pl.program_id(2)

    @pl.when(j == 0)
    def _():
        m_ref[...] = jnp.full_like(m_ref, NEG_BIG)
        l_ref[...] = jnp.zeros_like(l_ref)
        acc_ref[...] = jnp.zeros_like(acc_ref)

    def block(masked):
        if masked:
            t_glob = i * tq + lax.broadcasted_iota(I32, (tq, 1), 0)
            s_glob = j * tk + lax.broadcasted_iota(I32, (1, tk), 1)
            causal = s_glob <= t_glob
        for h in range(FX_HEADS):
            hs = slice(h * FX_D, (h + 1) * FX_D)
            qa = jnp.concatenate([q_ref[:, hs], qb_ref[:, hs]], axis=1)
            ka = jnp.concatenate([k_ref[:, hs], kb_ref[:, hs]], axis=1)
            s = _dot_nt(qa, ka)
            if masked:
                s = jnp.where(causal, s, -jnp.inf)
            m_prev = m_ref[h][:, 0:1]
            m_new = jnp.maximum(m_prev, jnp.max(s, axis=1, keepdims=True))
            alpha = jnp.exp2(m_prev - m_new)
            p = jnp.exp2(s - m_new)
            l_new = alpha * l_ref[h][:, 0:1] + jnp.sum(p, axis=1, keepdims=True)
            acc_ref[h] = alpha * acc_ref[h] + _dot(p.astype(BF16), v_ref[:, hs])
            m_ref[h] = jnp.broadcast_to(m_new, (tq, LANES))
            l_ref[h] = jnp.broadcast_to(l_new, (tq, LANES))

    first_q, last_q = i * tq, i * tq + tq - 1
    first_k, last_k = j * tk, j * tk + tk - 1

    @pl.when(last_k <= first_q)
    def _():
        block(False)

    @pl.when((last_k > first_q) & (first_k <= last_q))
    def _():
        block(True)

    @pl.when(j == pl.num_programs(2) - 1)
    def _():
        for h in range(FX_HEADS):
            o_ref[:, h * FX_D:(h + 1) * FX_D] = (acc_ref[h] / l_ref[h][:, 0:1]).astype(o_ref.dtype)


def fox_attention(p_fx, qb, kb, batch, seq, tq=256, tk=512):
    tq, tk = min(tq, seq), min(tk, seq)
    nq, nk = seq // tq, seq // tk

    def kv_blk(i, j):
        return jnp.minimum(j, (i * tq + tq - 1) // tk)

    qspec = lambda colblk: pl.BlockSpec((tq, BRANCH), lambda b, i, j: (b * nq + i, colblk))
    kspec = lambda colblk: pl.BlockSpec((tk, BRANCH), lambda b, i, j: (b * nk + kv_blk(i, j), colblk))
    return pl.pallas_call(
        functools.partial(_fox_body, tq=tq, tk=tk),
        out_shape=jax.ShapeDtypeStruct((batch * seq, BRANCH), BF16),
        grid=(batch, nq, nk),
        in_specs=[qspec(0), qspec(0), kspec(1), kspec(0), kspec(2)],
        out_specs=pl.BlockSpec((tq, BRANCH), lambda b, i, j: (b * nq + i, 0)),
        scratch_shapes=[pltpu.VMEM((FX_HEADS, tq, LANES), F32),
                        pltpu.VMEM((FX_HEADS, tq, LANES), F32),
                        pltpu.VMEM((FX_HEADS, tq, FX_D), F32)],
        compiler_params=_cp(("parallel", "parallel", "arbitrary")),
        name="fox_attention",
    )(p_fx, qb, p_fx, kb, p_fx)


def _qlat_body(q_ref, wuk_ref, o_ref):
    r = _dot_nt(q_ref[...], wuk_ref[...]) * (DS_HEAD_DIM ** -0.5 * LOG2E)
    o_ref[...] = r.astype(o_ref.dtype).reshape(o_ref.shape)


def dsa_qlat(qcat, w_uk, tq, tm=1024):
    m = qcat.shape[0]
    tm = min(tm, m)
    return pl.pallas_call(
        _qlat_body,
        out_shape=jax.ShapeDtypeStruct((m // tq, DS_HEADS, tq, DS_KV_LORA), BF16),
        grid=(m // tm, DS_HEADS),
        in_specs=[pl.BlockSpec((tm, DS_HEAD_DIM), lambda i, h: (i, h)),
                  pl.BlockSpec((DS_KV_LORA, DS_HEAD_DIM), lambda i, h: (0, h))],
        out_specs=pl.BlockSpec((tm // tq, 1, tq, DS_KV_LORA), lambda i, h: (i, h, 0, 0)),
        compiler_params=_cp(("parallel", "arbitrary")),
        name="dsa_qlat",
    )(qcat, w_uk)


def _dsa_body(qi_ref, misc_ref, ql_ref, kidx_ref, ckv_ref, wuv_ref, o_ref,
              keys_ref, wb_ref, thr_ref, s_ref, p_ref, m_ref, l_ref, alpha_ref, acc_ref, *, tq, tk, topk):
    i = pl.program_id(1)
    nh = DS_HEADS
    rows = nh * tq
    nkb = (i * tq + tq - 1) // tk + 1
    t_glob = i * tq + lax.broadcasted_iota(I32, (tq, 1), 0)
    col = lax.broadcasted_iota(I32, (1, tk), 1)
    w = misc_ref[:, DS_IDX_DIM:DS_IDX_DIM + DS_IDX_HEADS] * (DS_IDX_HEADS ** -0.5 * DS_IDX_DIM ** -0.5)
    for h in range(DS_IDX_HEADS):
        wb_ref[h] = jnp.broadcast_to(w[:, h:h + 1], (tq, LANES))
    qh = [qi_ref[:, h * DS_IDX_DIM:(h + 1) * DS_IDX_DIM] for h in range(DS_IDX_HEADS)]
    nlt = tk // LANES

    def score_body(j, carry):
        kb = kidx_ref[pl.ds(pl.multiple_of(j * tk, tk), tk), :]
        tiles = [jnp.zeros((tq, LANES), F32) for _ in range(nlt)]
        for h in range(DS_IDX_HEADS):
            lg = jnp.maximum(_dot_nt(qh[h], kb), 0.0)
            wbh = wb_ref[h]
            tiles = [t + lg[:, cc * LANES:(cc + 1) * LANES] * wbh for cc, t in enumerate(tiles)]
        sc = jnp.concatenate(tiles, axis=1)
        sc = jnp.where(j * tk + col <= t_glob, sc, -jnp.inf)
        bits = pltpu.bitcast(sc, I32)
        keys_ref[j] = jnp.where(bits < 0, bits ^ 0x7FFFFFFF, bits)
        return carry

    lax.fori_loop(0, nkb, score_body, 0)

    def count_ge(cand):
        def body(j, acc):
            ge = jnp.where(keys_ref[j] >= cand, 1.0, 0.0)
            part = ge[:, 0:LANES]
            for cc in range(1, tk // LANES):
                part = part + ge[:, cc * LANES:(cc + 1) * LANES]
            return acc + part
        acc = lax.fori_loop(0, nkb, body, jnp.zeros((tq, LANES), F32))
        return jnp.sum(acc, axis=1, keepdims=True)

    kf = float(topk)
    c0 = count_ge(jnp.zeros((tq, 1), I32))
    nonneg = c0 >= kf
    thr = jnp.where(nonneg, 0, INT_MIN).astype(I32)
    cnt = jnp.where(nonneg, c0, (nkb * tk).astype(F32))

    def search_bit(bi, thr, cnt):
        cand = thr | lax.shift_left(jnp.int32(1), bi)
        c = count_ge(cand)
        take = c >= kf
        return jnp.where(take, cand, thr), jnp.where(take, c, cnt)

    thr, cnt = search_bit(jnp.int32(30), thr, cnt)
    bits_per_round = 3

    def search_cond(carry):
        bi, _, cnt = carry
        return (bi >= 0) & (jnp.max(cnt) > kf)

    def search_round(carry):
        bi, thr, cnt = carry
        for r in range(bits_per_round):
            thr, cnt = search_bit(bi - r, thr, cnt)
        return bi - bits_per_round, thr, cnt

    _, thr, _ = lax.while_loop(search_cond, search_round, (jnp.int32(29), thr, cnt))

    thr_ref[...] = thr
    m_ref[...] = jnp.full_like(m_ref, NEG_BIG)
    l_ref[...] = jnp.zeros_like(l_ref)
    acc_ref[...] = jnp.zeros_like(acc_ref)
    rc = DSA_SOFTMAX_ROWS
    chunks_per_head = tq // rc

    def attn_body(j, carry):
        kv = ckv_ref[pl.ds(pl.multiple_of(j * tk, tk), tk), :]
        s_ref[...] = _dot_nt(ql_ref[0].reshape(rows, DS_KV_LORA), kv)

        for c in range(rows // rc):
            tok0 = (c // nh) * rc
            r0 = (c % nh) * tq + tok0
            rsl, tsl = slice(r0, r0 + rc), slice(tok0, tok0 + rc)
            if c % nh == 0:
                tg = i * tq + tok0 + lax.broadcasted_iota(I32, (rc, 1), 0)
                sel = (keys_ref[j, tsl, :] >= thr_ref[tsl, :]) & (j * tk + col <= tg)
            s = jnp.where(sel, s_ref[rsl, :], -jnp.inf)
            m_prev = m_ref[rsl, :]
            m_new = jnp.maximum(m_prev, jnp.max(s, axis=1, keepdims=True))
            alpha = jnp.exp2(m_prev - m_new)
            p = jnp.exp2(s - m_new)
            l_ref[rsl, :] = alpha * l_ref[rsl, :] + jnp.sum(p, axis=1, keepdims=True)
            m_ref[rsl, :] = m_new
            alpha_ref[rsl, :] = jnp.broadcast_to(alpha, (rc, LANES))
            p_ref[rsl, :] = p.astype(BF16)

        pv = _dot(p_ref[...], kv)
        alpha_b = alpha_ref[...]
        for cc in range(DS_KV_LORA // LANES):
            lsl = slice(cc * LANES, (cc + 1) * LANES)
            acc_ref[:, lsl] = alpha_b * acc_ref[:, lsl] + pv[:, lsl]
        return carry

    lax.fori_loop(0, nkb, attn_body, 0)

    for h in range(nh):
        hsl = slice(h * tq, (h + 1) * tq)
        o_lat = (acc_ref[hsl, :] / l_ref[hsl, :]).astype(BF16)
        o_ref[:, h * DS_HEAD_DIM:(h + 1) * DS_HEAD_DIM] = _dot(
            o_lat, wuv_ref[:, h * DS_HEAD_DIM:(h + 1) * DS_HEAD_DIM]).astype(o_ref.dtype)


def dsa_attention(qcat, p_misc, q_lat, k_idx, c_kv, w_uv, batch, seq, tq=256, tk=512):
    tq, tk = min(tq, seq), min(tk, seq)
    nq = seq // tq
    rows = DS_HEADS * tq
    topk = min(DS_TOPK_MAX, seq // 4)
    assert tk >= topk and tq % DSA_SOFTMAX_ROWS == 0
    return pl.pallas_call(
        functools.partial(_dsa_body, tq=tq, tk=tk, topk=topk),
        out_shape=jax.ShapeDtypeStruct((batch * seq, BRANCH), BF16),
        grid=(batch, nq),
        in_specs=[pl.BlockSpec((tq, DS_IDX_HEADS * DS_IDX_DIM), lambda b, i: (b * nq + i, 1)),
                  pl.BlockSpec((tq, LANES), lambda b, i: (b * nq + i, MISC_IDX // LANES)),
                  pl.BlockSpec((1, DS_HEADS, tq, DS_KV_LORA), lambda b, i: (b * nq + i, 0, 0, 0)),
                  pl.BlockSpec((seq, DS_IDX_DIM), lambda b, i: (b, 0)),
                  pl.BlockSpec((seq, DS_KV_LORA), lambda b, i: (b, 0)),
                  pl.BlockSpec((DS_KV_LORA, BRANCH), lambda b, i: (0, 0))],
        out_specs=pl.BlockSpec((tq, BRANCH), lambda b, i: (b * nq + i, 0)),
        scratch_shapes=[pltpu.VMEM((seq // tk, tq, tk), I32),
                        pltpu.VMEM((DS_IDX_HEADS, tq, LANES), F32),
                        pltpu.VMEM((tq, 1), I32),
                        pltpu.VMEM((rows, tk), F32),
                        pltpu.VMEM((rows, tk), BF16),
                        pltpu.VMEM((rows, 1), F32),
                        pltpu.VMEM((rows, 1), F32),
                        pltpu.VMEM((rows, LANES), F32),
                        pltpu.VMEM((rows, DS_KV_LORA), F32)],
        compiler_params=_cp(("parallel", "arbitrary")),
        name="dsa_attention",
    )(qcat, p_misc, q_lat, k_idx, c_kv, w_uv)


def _merge_body(h_ref, wg_ref, wb_ref, ya_ref, yb_ref, yc_ref, yd_ref, o_ref, acc_ref):
    n = pl.program_id(2)
    gate = jax.nn.sigmoid(_dot(h_ref[...], wg_ref[0]))
    for idx, y_ref in enumerate((ya_ref, yb_ref, yc_ref, yd_ref)):
        @pl.when(n == idx)
        def _(y_ref=y_ref, idx=idx):
            contrib = gate * _dot(y_ref[...], wb_ref[0])
            if idx == 0:
                acc_ref[...] = contrib
            else:
                acc_ref[...] += contrib

    @pl.when(n == pl.num_programs(2) - 1)
    def _():
        o_ref[...] = acc_ref[...].astype(o_ref.dtype)


def gated_merge(h, w_gate, w_branch, ys, tm=1024, tn=256):
    m = h.shape[0]
    tm = min(tm, m)
    ymap = lambda i, j, n: (i, 0)
    return pl.pallas_call(
        _merge_body,
        out_shape=jax.ShapeDtypeStruct((m, D_MODEL), BF16),
        grid=(m // tm, D_MODEL // tn, 4),
        in_specs=[pl.BlockSpec((tm, D_MODEL), lambda i, j, n: (i, 0)),
                  pl.BlockSpec((1, D_MODEL, tn), lambda i, j, n: (n, 0, j)),
                  pl.BlockSpec((1, BRANCH, tn), lambda i, j, n: (n, 0, j)),
                  pl.BlockSpec((tm, BRANCH), ymap), pl.BlockSpec((tm, BRANCH), ymap),
                  pl.BlockSpec((tm, BRANCH), ymap), pl.BlockSpec((tm, BRANCH), ymap)],
        out_specs=pl.BlockSpec((tm, tn), lambda i, j, n: (i, j)),
        scratch_shapes=[pltpu.VMEM((tm, tn), F32)],
        compiler_params=_cp(("parallel", "parallel", "arbitrary")),
        name="gated_merge",
    )(h, w_gate, w_branch, *ys)


def _ffn_up_body(h_ref, halo_ref, wg_ref, wu_ref, cw_ref, o_ref, *, tiles_per_seq):
    i = pl.program_id(0)
    g = _dot(h_ref[...], wg_ref[...])
    u = _dot(h_ref[...], wu_ref[...])
    gh = _dot(halo_ref[...], wg_ref[...])
    gh = gh * jnp.where(i % tiles_per_seq == 0, 0.0, 1.0)
    cw = cw_ref[...]
    row8 = lax.broadcasted_iota(I32, (8, 1), 0)
    y = g * cw[FFN_CONV - 1:FFN_CONV, :]
    for j in range(1, FFN_CONV):
        rolled = pltpu.roll(g, j, 0)
        head = jnp.where(row8 < j, pltpu.roll(gh, j, 0), rolled[0:8])
        shifted = jnp.concatenate([head, rolled[8:]], axis=0)
        y = y + shifted * cw[FFN_CONV - 1 - j:FFN_CONV - j, :]
    o_ref[...] = (_silu(y) * u).astype(o_ref.dtype)


def ffn_up(h, w_gate, w_up, conv_w, seq, tm=1024, tn=256):
    m = h.shape[0]
    tm = min(tm, seq)
    dff = w_gate.shape[1]
    return pl.pallas_call(
        functools.partial(_ffn_up_body, tiles_per_seq=seq // tm),
        out_shape=jax.ShapeDtypeStruct((m, dff), BF16),
        grid=(m // tm, dff // tn),
        in_specs=[pl.BlockSpec((tm, D_MODEL), lambda i, j: (i, 0)),
                  pl.BlockSpec((8, D_MODEL), lambda i, j: (jnp.maximum(i * (tm // 8) - 1, 0), 0)),
                  pl.BlockSpec((D_MODEL, tn), lambda i, j: (0, j)),
                  pl.BlockSpec((D_MODEL, tn), lambda i, j: (0, j)),
                  pl.BlockSpec((FFN_CONV, tn), lambda i, j: (0, j))],
        out_specs=pl.BlockSpec((tm, tn), lambda i, j: (i, j)),
        compiler_params=_cp(("parallel", "arbitrary")),
        name="ffn_up",
    )(h, h, w_gate, w_up, conv_w.astype(F32))


def _in_proj_weights(w):
    w = w.astype(BF16)
    o = 0
    secs = {}
    for name, width in (("hg", 4096), ("cq", 768), ("ckv", 512), ("kidx", 64), ("widx", 16), ("z", 1024),
                        ("xbc", MB_CONV_DIM), ("dt", 16), ("fx", 3072), ("ff", 8)):
        secs[name] = w[:, o:o + width]
        o += width
    zeros = lambda n: jnp.zeros((w.shape[0], n), w.dtype)
    misc = jnp.concatenate([secs["ckv"], secs["kidx"], secs["widx"], zeros(LANES - 80),
                            secs["dt"], zeros(LANES - 16), secs["ff"], zeros(LANES - 8)], axis=1)
    return dict(all=w, cq=secs["cq"], misc=misc, z=secs["z"], xbc=secs["xbc"], fx=secs["fx"])


def _pad_to(w, axis, size):
    shape = list(w.shape)
    shape[axis] = size - w.shape[axis]
    return jnp.concatenate([w.astype(BF16), jnp.zeros(shape, BF16)], axis=axis)


def _layer(x, l, batch, seq, prm):
    h = rmsnorm(x, prm["attn_norm"][l], BF16)
    w = _in_proj_weights(prm["w_in"][l])
    p_hg = matmul(h, w["all"], F32, 1024, 512, n=4 * BRANCH)
    p_cq = matmul(h, w["cq"], F32, 1024, 768)
    p_misc = matmul(h, w["misc"], F32, 1024, MISC_W)
    p_z = matmul(h, w["z"], F32, 1024, 512)
    p_xbc = matmul(h, w["xbc"], F32, 1024, 512)
    fx_scale = jnp.concatenate([jnp.full((1, BRANCH), FX_D ** -0.5 * LOG2E, F32), jnp.ones((1, 2 * BRANCH), F32)], 1)
    p_fx = matmul(h, w["fx"], BF16, 1024, 512, col_scale=fx_scale)

    y_a = hgrn2(p_hg, prm["hgrn_lb_logits"], prm["hgrn_norm"][l], l, batch, seq)

    c_q = rmsnorm(p_cq, prm["dsa_q_norm"][l], BF16)
    c_kv = rmsnorm(p_misc, prm["dsa_kv_norm"][l], BF16, width=DS_KV_LORA, col_block=0)
    k_idx = p_misc[:, MISC_IDX:MISC_IDX + DS_IDX_DIM].astype(BF16)
    w_q = jnp.concatenate([prm["dsa_w_uq"][l].astype(BF16), prm["dsa_w_iq"][l].astype(BF16)], axis=1)
    qcat = matmul(c_q, w_q, BF16, 1024, 1024)
    dsa_tq = min(256, seq)
    q_lat = dsa_qlat(qcat, prm["dsa_w_uk"][l].astype(BF16), dsa_tq)
    y_b = dsa_attention(qcat, p_misc, q_lat, k_idx, c_kv, prm["dsa_w_uv"][l].astype(BF16), batch, seq,
                        tq=dsa_tq)

    y_c = mamba2(p_z, p_xbc, p_misc, prm["ssm_conv_w"][l], prm["ssm_conv_b"][l], prm["ssm_dt_bias"][l],
                 prm["ssm_a_log"][l], prm["ssm_d"][l], prm["ssm_norm"][l], batch, seq)

    qb, kb = fox_bias_columns(p_misc, prm["fox_f_bias"][l], batch, seq)
    y_d = fox_attention(p_fx, qb, kb, batch, seq)

    merged = gated_merge(h, prm["w_gate"][l].astype(BF16), prm["w_branch"][l].astype(BF16),
                         (y_a, y_b, y_c, y_d))
    x = matmul(merged, prm["w_out"][l].astype(BF16), F32, 1024, 512, residual=x)

    h2 = rmsnorm(x, prm["ffn_norm"][l], BF16)
    conv_w = jnp.pad(prm["ffn_conv"][l], ((0, 0), (0, D_FF_PAD - D_FF)))
    act = ffn_up(h2, _pad_to(prm["ffn_w_gate"][l], 1, D_FF_PAD), _pad_to(prm["ffn_w_up"][l], 1, D_FF_PAD),
                 conv_w, seq)
    return matmul_residual(act, _pad_to(prm["ffn_w_down"][l], 0, D_FF_PAD), x, 1024, 1024, 1024)


def kernel(x, attn_norm, ffn_norm, final_norm, w_in, hgrn_lb_logits, hgrn_norm, dsa_q_norm, dsa_kv_norm,
           dsa_w_uq, dsa_w_iq, dsa_w_uk, dsa_w_uv, ssm_conv_w, ssm_conv_b, ssm_dt_bias, ssm_a_log, ssm_d,
           ssm_norm, fox_f_bias, w_gate, w_branch, w_out, ffn_w_gate, ffn_w_up, ffn_conv, ffn_w_down):
    batch, seq, d = x.shape
    prm = dict(attn_norm=attn_norm, ffn_norm=ffn_norm, w_in=w_in, hgrn_lb_logits=hgrn_lb_logits,
               hgrn_norm=hgrn_norm, dsa_q_norm=dsa_q_norm, dsa_kv_norm=dsa_kv_norm, dsa_w_uq=dsa_w_uq,
               dsa_w_iq=dsa_w_iq, dsa_w_uk=dsa_w_uk, dsa_w_uv=dsa_w_uv, ssm_conv_w=ssm_conv_w,
               ssm_conv_b=ssm_conv_b, ssm_dt_bias=ssm_dt_bias, ssm_a_log=ssm_a_log, ssm_d=ssm_d,
               ssm_norm=ssm_norm, fox_f_bias=fox_f_bias, w_gate=w_gate, w_branch=w_branch, w_out=w_out,
               ffn_w_gate=ffn_w_gate, ffn_w_up=ffn_w_up, ffn_conv=ffn_conv, ffn_w_down=ffn_w_down)
    xf = x.reshape(batch * seq, d)
    for l in range(DEPTH):
        xf = _layer(xf, l, batch, seq, prm)
    return rmsnorm(xf, final_norm, x.dtype).reshape(batch, seq, d)
```

```python
import functools

import jax
import jax.numpy as jnp
from jax import lax
from jax.experimental import pallas as pl
from jax.experimental.pallas import tpu as pltpu

F32, BF16, I32 = jnp.float32, jnp.bfloat16, jnp.int32
HIGHEST = lax.Precision.HIGHEST

D_MODEL = 4096
DEPTH = 2
BRANCH = 1024
HG_HEADS, HG_D, HG_CHUNK, HG_SUB = 8, 128, 64, 16
DS_HEADS, DS_HEAD_DIM, DS_Q_LORA, DS_KV_LORA = 8, 128, 768, 512
DS_IDX_HEADS, DS_IDX_DIM, DS_TOPK_MAX = 16, 64, 256
SOFTMAX_ROWS = 64
MB_HEADS, MB_P, MB_N, MB_GROUPS, MB_CONV, MB_CHUNK = 16, 64, 128, 2, 4, 128
MB_CONV_DIM = BRANCH + 2 * MB_GROUPS * MB_N
FX_HEADS, FX_D = 8, 128
D_FF = 11008
FFN_CONV = 3
EPS = 1e-6
LANES = 128
NEG_BIG = -1e30
LOG2E = 1.4426950408889634
INT_MIN = -(2 ** 31)

MISC_CKV = 0
MISC_IDX = 512
MISC_DT = 640
MISC_FF = 768
MISC_W = 896


def _cp(sem, vmem_mb=48):
    return pltpu.CompilerParams(dimension_semantics=sem, vmem_limit_bytes=vmem_mb * 2 ** 20)


def _dot(a, b, precision=None):
    return jnp.dot(a, b, preferred_element_type=F32, precision=precision)


def _dot_nt(a, b):
    return lax.dot_general(a, b, (((1,), (1,)), ((), ())), preferred_element_type=F32)


def _dot_tn(a, b):
    return lax.dot_general(a, b, (((0,), (0,)), ((), ())), preferred_element_type=F32)


def _tril(n):
    r = lax.broadcasted_iota(I32, (n, n), 0)
    c = lax.broadcasted_iota(I32, (n, n), 1)
    return (r >= c).astype(F32)


def _silu(x):
    return x * jax.nn.sigmoid(x)


def _rmsnorm_body(x_ref, g_ref, o_ref):
    x = x_ref[...].astype(F32)
    ms = jnp.mean(x * x, axis=-1, keepdims=True)
    o_ref[...] = (x * lax.rsqrt(ms + EPS) * g_ref[...]).astype(o_ref.dtype)


def rmsnorm(x, gain, out_dtype, width=None, col_block=0, tm=256):
    m = x.shape[0]
    width = x.shape[1] if width is None else width
    return pl.pallas_call(
        _rmsnorm_body,
        out_shape=jax.ShapeDtypeStruct((m, width), out_dtype),
        grid=(m // tm,),
        in_specs=[pl.BlockSpec((tm, width), lambda i: (i, col_block)),
                  pl.BlockSpec((1, width), lambda i: (0, 0))],
        out_specs=pl.BlockSpec((tm, width), lambda i: (i, 0)),
        compiler_params=_cp(("parallel",)),
        name="rmsnorm",
    )(x, gain.reshape(1, width).astype(F32))


def _mm_body(a_ref, b_ref, o_ref):
    o_ref[...] = _dot(a_ref[...], b_ref[...].astype(BF16)).astype(o_ref.dtype)


def _mm_add_body(a_ref, b_ref, r_ref, o_ref):
    o_ref[...] = (r_ref[...] + _dot(a_ref[...], b_ref[...].astype(BF16))).astype(o_ref.dtype)


def _mm_scale_body(a_ref, b_ref, s_ref, o_ref):
    o_ref[...] = (_dot(a_ref[...], b_ref[...].astype(BF16)) * s_ref[...]).astype(o_ref.dtype)


def matmul(a, b, out_dtype, tm, tn, residual=None, col_scale=None, n=None, b_lead=(), vmem_mb=48):
    m, k = a.shape
    n = b.shape[-1] if n is None else n
    tm, tn = min(tm, m), min(tn, n)
    in_specs = [pl.BlockSpec((tm, k), lambda i, j: (i, 0)),
                pl.BlockSpec((None,) * len(b_lead) + (k, tn), lambda i, j: tuple(b_lead) + (0, j))]
    args, body = (a, b), _mm_body
    if residual is not None:
        in_specs.append(pl.BlockSpec((tm, tn), lambda i, j: (i, j)))
        args, body = (a, b, residual), _mm_add_body
    elif col_scale is not None:
        in_specs.append(pl.BlockSpec((1, tn), lambda i, j: (0, j)))
        args, body = (a, b, col_scale), _mm_scale_body
    return pl.pallas_call(
        body,
        out_shape=jax.ShapeDtypeStruct((m, n), out_dtype),
        grid=(m // tm, n // tn),
        in_specs=in_specs,
        out_specs=pl.BlockSpec((tm, tn), lambda i, j: (i, j)),
        compiler_params=_cp(("parallel", "arbitrary"), vmem_mb),
        name="matmul",
    )(*args)


def _hgrn_body(lbl_ref, gain_ref, q_ref, f_ref, i_ref, g_ref, o_ref, st_ref, *, layer, nchunks):
    c = pl.program_id(2)

    @pl.when(c == 0)
    def _():
        st_ref[...] = jnp.zeros_like(st_ref)

    logits = lbl_ref[...]
    e = jnp.exp(logits - jnp.max(logits, axis=0, keepdims=True))
    p = e / jnp.sum(e, axis=0, keepdims=True)
    lb = jnp.sum(p[0:layer + 1], axis=0, keepdims=True) - p[0:1]
    gain = gain_ref[...]
    tril = _tril(HG_CHUNK)
    row = lax.broadcasted_iota(I32, (HG_CHUNK, 1), 0)
    row_in_sub = row % HG_SUB
    nsub = HG_CHUNK // HG_SUB

    def chunk(ci, carry):
        sl = pl.ds(pl.multiple_of(ci * HG_CHUNK, HG_CHUNK), HG_CHUNK)
        q = _silu(q_ref[sl, :])
        v = _silu(i_ref[sl, :])
        f = lb + (1.0 - lb) * jax.nn.sigmoid(f_ref[sl, :])
        logf = jnp.log(f)
        k = 1.0 - f
        b = _dot(tril, logf, HIGHEST)
        st = st_ref[...]

        o = _dot_nt((q * jnp.exp(b)).astype(BF16), st.astype(BF16))

        intra = jnp.zeros((HG_CHUNK, HG_D), F32)
        for d in range(HG_SUB):
            ks = k if d == 0 else pltpu.roll(k, d, 0)
            bs = b if d == 0 else pltpu.roll(b, d, 0)
            vs = v if d == 0 else pltpu.roll(v, d, 0)
            expo = jnp.where(row_in_sub >= d, b - bs, -jnp.inf)
            w = jnp.sum(q * ks * jnp.exp(expo), axis=-1, keepdims=True)
            intra = intra + w * vs
        o = o + intra

        parts = [jnp.zeros((HG_SUB, HG_D), F32)]
        for si in range(1, nsub):
            lo = si * HG_SUB
            r = b[lo - 1:lo, :]
            qi = (q[lo:lo + HG_SUB] * jnp.exp(b[lo:lo + HG_SUB] - r)).astype(BF16)
            kj = (k[0:lo] * jnp.exp(r - b[0:lo])).astype(BF16)
            sc = _dot_nt(qi, kj)
            parts.append(_dot(sc.astype(BF16), v[0:lo].astype(BF16)))
        o = o + jnp.concatenate(parts, axis=0)

        b_last = b[HG_CHUNK - 1:HG_CHUNK, :]
        kd = (k * jnp.exp(b_last - b)).astype(BF16)
        st_ref[...] = st * jnp.exp(b_last) + _dot_tn(v.astype(BF16), kd)

        og = o * jax.nn.sigmoid(g_ref[sl, :])
        ms = jnp.mean(og * og, axis=-1, keepdims=True)
        o_ref[sl, :] = (og * lax.rsqrt(ms + EPS) * gain).astype(o_ref.dtype)
        return carry

    lax.fori_loop(0, nchunks, chunk, 0, unroll=True)


def hgrn2(p_hg, lb_logits, norm_gain, layer, batch, seq, tt=256):
    tt = min(tt, seq)
    nt = seq // tt
    hb = HG_HEADS

    def col(sec):
        return lambda b, h, c: (b * nt + c, sec * hb + h)

    return pl.pallas_call(
        functools.partial(_hgrn_body, layer=layer, nchunks=tt // HG_CHUNK),
        out_shape=jax.ShapeDtypeStruct((batch * seq, BRANCH), BF16),
        grid=(batch, hb, nt),
        in_specs=[pl.BlockSpec((DEPTH, HG_D), lambda b, h, c: (0, h)),
                  pl.BlockSpec((1, HG_D), lambda b, h, c: (0, h)),
                  pl.BlockSpec((tt, HG_D), col(0)),
                  pl.BlockSpec((tt, HG_D), col(1)),
                  pl.BlockSpec((tt, HG_D), col(2)),
                  pl.BlockSpec((tt, HG_D), col(3))],
        out_specs=pl.BlockSpec((tt, HG_D), lambda b, h, c: (b * nt + c, h)),
        scratch_shapes=[pltpu.VMEM((HG_D, HG_D), F32)],
        compiler_params=_cp(("parallel", "parallel", "arbitrary")),
        name="hgrn2",
    )(lb_logits.astype(F32), norm_gain.reshape(1, BRANCH).astype(F32), p_hg, p_hg, p_hg, p_hg)


def _mamba_body(z_ref, xbc_ref, dt_ref, cw_ref, cb_ref, dtb_ref, alog_ref, dsk_ref, gain_ref,
                o_ref, prev_ref, st_ref):
    c = pl.program_id(1)
    L = MB_CHUNK

    @pl.when(c == 0)
    def _():
        prev_ref[...] = jnp.zeros_like(prev_ref)
        st_ref[...] = jnp.zeros_like(st_ref)

    x = xbc_ref[...]
    prev = prev_ref[...]
    row = lax.broadcasted_iota(I32, (L, 1), 0)
    cw = cw_ref[...]
    acc = x * cw[MB_CONV - 1:MB_CONV, :] + cb_ref[...]
    for j in range(1, MB_CONV):
        sh = jnp.where(row >= j, pltpu.roll(x, j, 0), pltpu.roll(prev, j, 0))
        acc = acc + sh * cw[MB_CONV - 1 - j:MB_CONV - j, :]
    prev_ref[...] = x
    xbc = _silu(acc)
    xs = xbc[:, 0:BRANCH]
    bm = xbc[:, BRANCH:BRANCH + MB_GROUPS * MB_N]
    cm = xbc[:, BRANCH + MB_GROUPS * MB_N:MB_CONV_DIM]

    raw = dt_ref[...] + dtb_ref[...]
    dt = jnp.maximum(raw, 0.0) + jnp.log1p(jnp.exp(-jnp.abs(raw)))
    a = -jnp.exp(alog_ref[...]) * dt
    tril = _tril(L)
    a_cs = _dot(tril, a, HIGHEST)
    a_cs_t = a_cs.T
    causal = tril > 0.5
    dsk = dsk_ref[...]

    hpg = MB_HEADS // MB_GROUPS
    ys = []
    for g in range(MB_GROUPS):
        bg = bm[:, g * MB_N:(g + 1) * MB_N]
        cg = cm[:, g * MB_N:(g + 1) * MB_N]
        cb = _dot_nt(cg.astype(BF16), bg.astype(BF16))
        for hh in range(hpg):
            h = g * hpg + hh
            acol = a_cs[:, h:h + 1]
            arow = a_cs_t[h:h + 1, :]
            lmat = jnp.exp(jnp.where(causal, acol - arow, -jnp.inf))
            xh = xs[:, h * MB_P:(h + 1) * MB_P]
            xdt = (xh * dt[:, h:h + 1]).astype(BF16)
            y = _dot((cb * lmat).astype(BF16), xdt)
            st = st_ref[h]
            y = y + _dot(cg.astype(BF16), st.astype(BF16)) * jnp.exp(acol)
            a_last = a_cs[L - 1:L, h:h + 1]
            bdec = (bg * jnp.exp(a_last - acol)).astype(BF16)
            st_ref[h] = st * jnp.exp(a_last) + _dot_tn(bdec, xdt)
            ys.append(y + dsk[:, h:h + 1] * xh)
    y = jnp.concatenate(ys, axis=1)
    y = y * _silu(z_ref[...])
    gw = BRANCH // MB_GROUPS
    outs = []
    for g in range(MB_GROUPS):
        yg = y[:, g * gw:(g + 1) * gw]
        ms = jnp.mean(yg * yg, axis=-1, keepdims=True)
        outs.append(yg * lax.rsqrt(ms + EPS))
    o_ref[...] = (jnp.concatenate(outs, axis=1) * gain_ref[...]).astype(o_ref.dtype)


def _pad_lanes(v, n=LANES):
    v = v.reshape(1, -1).astype(F32)
    return jnp.pad(v, ((0, 0), (0, n - v.shape[1])))


def mamba2(p_z, p_xbc, p_misc, conv_w, conv_b, dt_bias, a_log, d_skip, norm_gain, batch, seq):
    nc = seq // MB_CHUNK
    L = MB_CHUNK
    full = lambda shape: pl.BlockSpec(shape, lambda b, c: (0, 0))
    return pl.pallas_call(
        _mamba_body,
        out_shape=jax.ShapeDtypeStruct((batch * seq, BRANCH), BF16),
        grid=(batch, nc),
        in_specs=[pl.BlockSpec((L, BRANCH), lambda b, c: (b * nc + c, 0)),
                  pl.BlockSpec((L, MB_CONV_DIM), lambda b, c: (b * nc + c, 0)),
                  pl.BlockSpec((L, LANES), lambda b, c: (b * nc + c, MISC_DT // LANES)),
                  full((MB_CONV, MB_CONV_DIM)), full((1, MB_CONV_DIM)),
                  full((1, LANES)), full((1, LANES)), full((1, LANES)), full((1, BRANCH))],
        out_specs=pl.BlockSpec((L, BRANCH), lambda b, c: (b * nc + c, 0)),
        scratch_shapes=[pltpu.VMEM((L, MB_CONV_DIM), F32),
                        pltpu.VMEM((MB_HEADS, MB_N, MB_P), F32)],
        compiler_params=_cp(("parallel", "arbitrary")),
        name="mamba2",
    )(p_z, p_xbc, p_misc, conv_w.astype(F32), conv_b.reshape(1, -1).astype(F32),
      _pad_lanes(dt_bias), _pad_lanes(a_log), _pad_lanes(d_skip), norm_gain.reshape(1, BRANCH).astype(F32))


def _fox_cum_body(f_ref, bias_ref, qb_ref, kb_ref, carry_ref, *, tt):
    c = pl.program_id(1)

    @pl.when(c == 0)
    def _():
        carry_ref[...] = jnp.zeros_like(carry_ref)

    logf = jax.nn.log_sigmoid(f_ref[...] + bias_ref[...])
    cum = _dot(_tril(tt), logf, HIGHEST) + carry_ref[...]
    carry_ref[...] = cum[tt - 1:tt, :]

    c2 = cum * LOG2E
    lane = lax.broadcasted_iota(I32, (1, LANES), 1)
    ones = jnp.where(lane < 6, 1.0, 0.0)
    for h in range(FX_HEADS):
        col = c2[:, h:h + 1]
        hi = col.astype(BF16).astype(F32)
        r1 = col - hi
        mid = r1.astype(BF16).astype(F32)
        lo = r1 - mid
        qb = jnp.where(lane == 0, hi, jnp.where(lane == 1, mid, jnp.where(lane == 2, lo, ones)))
        kb = jnp.where(lane == 3, -hi, jnp.where(lane == 4, -mid, jnp.where(lane == 5, -lo, ones)))
        qb_ref[:, h * LANES:(h + 1) * LANES] = qb.astype(BF16)
        kb_ref[:, h * LANES:(h + 1) * LANES] = kb.astype(BF16)


def fox_bias_columns(p_misc, f_bias, batch, seq, tt=512):
    tt = min(tt, seq)
    nt = seq // tt
    out = jax.ShapeDtypeStruct((batch * seq, FX_HEADS * LANES), BF16)
    ospec = pl.BlockSpec((tt, FX_HEADS * LANES), lambda b, c: (b * nt + c, 0))
    return pl.pallas_call(
        functools.partial(_fox_cum_body, tt=tt),
        out_shape=(out, out),
        grid=(batch, nt),
        in_specs=[pl.BlockSpec((tt, LANES), lambda b, c: (b * nt + c, MISC_FF // LANES)),
                  pl.BlockSpec((1, LANES), lambda b, c: (0, 0))],
        out_specs=(ospec, ospec),
        scratch_shapes=[pltpu.VMEM((1, LANES), F32)],
        compiler_params=_cp(("parallel", "arbitrary")),
        name="fox_bias_columns",
    )(p_misc, _pad_lanes(f_bias))


def _fox_body(q_ref, qb_ref, k_ref, kb_ref, v_ref, o_ref, m_ref, l_ref, acc_ref, *, tq, tk):
    i = pl.program_id(1)
    j = pl.program_id(2)

    @pl.when(j == 0)
    def _():
        m_ref[...] = jnp.full_like(m_ref, NEG_BIG)
        l_ref[...] = jnp.zeros_like(l_ref)
        acc_ref[...] = jnp.zeros_like(acc_ref)

    def block(masked):
        if masked:
            t_glob = i * tq + lax.broadcasted_iota(I32, (tq, 1), 0)
            s_glob = j * tk + lax.broadcasted_iota(I32, (1, tk), 1)
            causal = s_glob <= t_glob
        for h in range(FX_HEADS):
            hs = slice(h * FX_D, (h + 1) * FX_D)
            qa = jnp.concatenate([q_ref[:, hs], qb_ref[:, hs]], axis=1)
            ka = jnp.concatenate([k_ref[:, hs], kb_ref[:, hs]], axis=1)
            s = _dot_nt(qa, ka)
            if masked:
                s = jnp.where(causal, s, -jnp.inf)
            m_prev = m_ref[h][:, 0:1]
            m_new = jnp.maximum(m_prev, jnp.max(s, axis=1, keepdims=True))
            alpha = jnp.exp2(m_prev - m_new)
            p = jnp.exp2(s - m_new)
            l_new = alpha * l_ref[h][:, 0:1] + jnp.sum(p, axis=1, keepdims=True)
            acc_ref[h] = alpha * acc_ref[h] + _dot(p.astype(BF16), v_ref[:, hs])
            m_ref[h] = jnp.broadcast_to(m_new, (tq, LANES))
            l_ref[h] = jnp.broadcast_to(l_new, (tq, LANES))

    first_q, last_q = i * tq, i * tq + tq - 1
    first_k, last_k = j * tk, j * tk + tk - 1

    @pl.when(last_k <= first_q)
    def _():
        block(False)

    @pl.when((last_k > first_q) & (first_k <= last_q))
    def _():
        block(True)

    @pl.when(j == pl.num_programs(2) - 1)
    def _():
        for h in range(FX_HEADS):
            o_ref[:, h * FX_D:(h + 1) * FX_D] = (acc_ref[h] / l_ref[h][:, 0:1]).astype(o_ref.dtype)


def fox_attention(p_fx, qb, kb, batch, seq, tq=256, tk=512):
    tq, tk = min(tq, seq), min(tk, seq)
    nq, nk = seq // tq, seq // tk

    def kv_blk(i, j):
        return jnp.minimum(j, (i * tq + tq - 1) // tk)

    qspec = lambda colblk: pl.BlockSpec((tq, BRANCH), lambda b, i, j: (b * nq + i, colblk))
    kspec = lambda colblk: pl.BlockSpec((tk, BRANCH), lambda b, i, j: (b * nk + kv_blk(i, j), colblk))
    return pl.pallas_call(
        functools.partial(_fox_body, tq=tq, tk=tk),
        out_shape=jax.ShapeDtypeStruct((batch * seq, BRANCH), BF16),
        grid=(batch, nq, nk),
        in_specs=[qspec(0), qspec(0), kspec(1), kspec(0), kspec(2)],
        out_specs=pl.BlockSpec((tq, BRANCH), lambda b, i, j: (b * nq + i, 0)),
        scratch_shapes=[pltpu.VMEM((FX_HEADS, tq, LANES), F32),
                        pltpu.VMEM((FX_HEADS, tq, LANES), F32),
                        pltpu.VMEM((FX_HEADS, tq, FX_D), F32)],
        compiler_params=_cp(("parallel", "parallel", "arbitrary")),
        name="fox_attention",
    )(p_fx, qb, p_fx, kb, p_fx)


def _qlat_body(q_ref, wuk_ref, o_ref):
    r = _dot_nt(q_ref[...], wuk_ref[...]) * (DS_HEAD_DIM ** -0.5 * LOG2E)
    o_ref[...] = r.astype(o_ref.dtype).reshape(o_ref.shape)


def dsa_qlat(qcat, w_uk, tq, tm=1024):
    m = qcat.shape[0]
    tm = min(tm, m)
    return pl.pallas_call(
        _qlat_body,
        out_shape=jax.ShapeDtypeStruct((m // tq, DS_HEADS, tq, DS_KV_LORA), BF16),
        grid=(m // tm, DS_HEADS),
        in_specs=[pl.BlockSpec((tm, DS_HEAD_DIM), lambda i, h: (i, h)),
                  pl.BlockSpec((DS_KV_LORA, DS_HEAD_DIM), lambda i, h: (0, h))],
        out_specs=pl.BlockSpec((tm // tq, 1, tq, DS_KV_LORA), lambda i, h: (i, h, 0, 0)),
        compiler_params=_cp(("parallel", "arbitrary")),
        name="dsa_qlat",
    )(qcat, w_uk)


def _dsa_body(qi_ref, misc_ref, ql_ref, kidx_ref, ckv_ref, wuv_ref, o_ref,
              keys_ref, wb_ref, thr_ref, s_ref, p_ref, m_ref, l_ref, alpha_ref, acc_ref, *, tq, tk, topk):
    i = pl.program_id(1)
    nh = DS_HEADS
    rows = nh * tq
    nkb = (i * tq + tq - 1) // tk + 1
    t_glob = i * tq + lax.broadcasted_iota(I32, (tq, 1), 0)
    col = lax.broadcasted_iota(I32, (1, tk), 1)
    w = misc_ref[:, DS_IDX_DIM:DS_IDX_DIM + DS_IDX_HEADS] * (DS_IDX_HEADS ** -0.5 * DS_IDX_DIM ** -0.5)
    for h in range(DS_IDX_HEADS):
        wb_ref[h] = jnp.broadcast_to(w[:, h:h + 1], (tq, LANES))
    qh = [qi_ref[:, h * DS_IDX_DIM:(h + 1) * DS_IDX_DIM] for h in range(DS_IDX_HEADS)]
    nlt = tk // LANES

    def score_body(j, carry):
        kb = kidx_ref[pl.ds(pl.multiple_of(j * tk, tk), tk), :]
        tiles = [jnp.zeros((tq, LANES), F32) for _ in range(nlt)]
        for h in range(DS_IDX_HEADS):
            lg = jnp.maximum(_dot_nt(qh[h], kb), 0.0)
            wbh = wb_ref[h]
            tiles = [t + lg[:, cc * LANES:(cc + 1) * LANES] * wbh for cc, t in enumerate(tiles)]
        sc = jnp.concatenate(tiles, axis=1)
        sc = jnp.where(j * tk + col <= t_glob, sc, -jnp.inf)
        bits = pltpu.bitcast(sc, I32)
        keys_ref[j] = jnp.where(bits < 0, bits ^ 0x7FFFFFFF, bits)
        return carry

    lax.fori_loop(0, nkb, score_body, 0)

    def count_ge(cand):
        def body(j, acc):
            ge = jnp.where(keys_ref[j] >= cand, 1.0, 0.0)
            part = ge[:, 0:LANES]
            for cc in range(1, tk // LANES):
                part = part + ge[:, cc * LANES:(cc + 1) * LANES]
            return acc + part
        acc = lax.fori_loop(0, nkb, body, jnp.zeros((tq, LANES), F32))
        return jnp.sum(acc, axis=1, keepdims=True)

    kf = float(topk)
    c0 = count_ge(jnp.zeros((tq, 1), I32))
    nonneg = c0 >= kf
    thr = jnp.where(nonneg, 0, INT_MIN).astype(I32)
    cnt = jnp.where(nonneg, c0, (nkb * tk).astype(F32))

    def search_bit(bi, thr, cnt):
        cand = thr | lax.shift_left(jnp.int32(1), bi)
        c = count_ge(cand)
        take = c >= kf
        return jnp.where(take, cand, thr), jnp.where(take, c, cnt)

    thr, cnt = search_bit(jnp.int32(30), thr, cnt)
    bits_per_round = 3

    def search_cond(carry):
        bi, _, cnt = carry
        return (bi >= 0) & (jnp.max(cnt) > kf)

    def search_round(carry):
        bi, thr, cnt = carry
        for r in range(bits_per_round):
            thr, cnt = search_bit(bi - r, thr, cnt)
        return bi - bits_per_round, thr, cnt

    _, thr, _ = lax.while_loop(search_cond, search_round, (jnp.int32(29), thr, cnt))

    thr_ref[...] = thr
    m_ref[...] = jnp.full_like(m_ref, NEG_BIG)
    l_ref[...] = jnp.zeros_like(l_ref)
    acc_ref[...] = jnp.zeros_like(acc_ref)
    rc = SOFTMAX_ROWS
    chunks_per_head = tq // rc

    def attn_body(j, carry):
        kv = ckv_ref[pl.ds(pl.multiple_of(j * tk, tk), tk), :]
        s_ref[...] = _dot_nt(ql_ref[0].reshape(rows, DS_KV_LORA), kv)

        for c in range(rows // rc):
            tok0 = (c // nh) * rc
            r0 = (c % nh) * tq + tok0
            rsl, tsl = slice(r0, r0 + rc), slice(tok0, tok0 + rc)
            if c % nh == 0:
                tg = i * tq + tok0 + lax.broadcasted_iota(I32, (rc, 1), 0)
                sel = (keys_ref[j, tsl, :] >= thr_ref[tsl, :]) & (j * tk + col <= tg)
            s = jnp.where(sel, s_ref[rsl, :], -jnp.inf)
            m_prev = m_ref[rsl, :]
            m_new = jnp.maximum(m_prev, jnp.max(s, axis=1, keepdims=True))
            alpha = jnp.exp2(m_prev - m_new)
            p = jnp.exp2(s - m_new)
            l_ref[rsl, :] = alpha * l_ref[rsl, :] + jnp.sum(p, axis=1, keepdims=True)
            m_ref[rsl, :] = m_new
            alpha_ref[rsl, :] = jnp.broadcast_to(alpha, (rc, LANES))
            p_ref[rsl, :] = p.astype(BF16)

        pv = _dot(p_ref[...], kv)
        alpha_b = alpha_ref[...]
        for cc in range(DS_KV_LORA // LANES):
            lsl = slice(cc * LANES, (cc + 1) * LANES)
            acc_ref[:, lsl] = alpha_b * acc_ref[:, lsl] + pv[:, lsl]
        return carry

    lax.fori_loop(0, nkb, attn_body, 0)

    for h in range(nh):
        hsl = slice(h * tq, (h + 1) * tq)
        o_lat = (acc_ref[hsl, :] / l_ref[hsl, :]).astype(BF16)
        o_ref[:, h * DS_HEAD_DIM:(h + 1) * DS_HEAD_DIM] = _dot(
            o_lat, wuv_ref[:, h * DS_HEAD_DIM:(h + 1) * DS_HEAD_DIM]).astype(o_ref.dtype)


def dsa_attention(qcat, p_misc, q_lat, k_idx, c_kv, w_uv, batch, seq, tq=256, tk=512):
    tq, tk = min(tq, seq), min(tk, seq)
    nq = seq // tq
    rows = DS_HEADS * tq
    topk = min(DS_TOPK_MAX, seq // 4)
    assert tk >= topk and tq % SOFTMAX_ROWS == 0
    return pl.pallas_call(
        functools.partial(_dsa_body, tq=tq, tk=tk, topk=topk),
        out_shape=jax.ShapeDtypeStruct((batch * seq, BRANCH), BF16),
        grid=(batch, nq),
        in_specs=[pl.BlockSpec((tq, DS_IDX_HEADS * DS_IDX_DIM), lambda b, i: (b * nq + i, 1)),
                  pl.BlockSpec((tq, LANES), lambda b, i: (b * nq + i, MISC_IDX // LANES)),
                  pl.BlockSpec((1, DS_HEADS, tq, DS_KV_LORA), lambda b, i: (b * nq + i, 0, 0, 0)),
                  pl.BlockSpec((seq, DS_IDX_DIM), lambda b, i: (b, 0)),
                  pl.BlockSpec((seq, DS_KV_LORA), lambda b, i: (b, 0)),
                  pl.BlockSpec((DS_KV_LORA, BRANCH), lambda b, i: (0, 0))],
        out_specs=pl.BlockSpec((tq, BRANCH), lambda b, i: (b * nq + i, 0)),
        scratch_shapes=[pltpu.VMEM((seq // tk, tq, tk), I32),
                        pltpu.VMEM((DS_IDX_HEADS, tq, LANES), F32),
                        pltpu.VMEM((tq, 1), I32),
                        pltpu.VMEM((rows, tk), F32),
                        pltpu.VMEM((rows, tk), BF16),
                        pltpu.VMEM((rows, 1), F32),
                        pltpu.VMEM((rows, 1), F32),
                        pltpu.VMEM((rows, LANES), F32),
                        pltpu.VMEM((rows, DS_KV_LORA), F32)],
        compiler_params=_cp(("parallel", "arbitrary")),
        name="dsa_attention",
    )(qcat, p_misc, q_lat, k_idx, c_kv, w_uv)


def _merge_body(h_ref, wg_ref, wb_ref, ya_ref, yb_ref, yc_ref, yd_ref, o_ref, acc_ref):
    n = pl.program_id(2)
    gate = jax.nn.sigmoid(_dot(h_ref[...], wg_ref[0].astype(BF16)))
    wb = wb_ref[0].astype(BF16)
    for idx, y_ref in enumerate((ya_ref, yb_ref, yc_ref, yd_ref)):
        @pl.when(n == idx)
        def _(y_ref=y_ref, idx=idx):
            contrib = gate * _dot(y_ref[...], wb)
            if idx == 0:
                acc_ref[...] = contrib
            else:
                acc_ref[...] += contrib

    @pl.when(n == pl.num_programs(2) - 1)
    def _():
        o_ref[...] = acc_ref[...].astype(o_ref.dtype)


def gated_merge(h, w_gate, w_branch, layer, ys, tm=1024, tn=256):
    m = h.shape[0]
    tm = min(tm, m)
    ymap = lambda i, j, n: (i, 0)
    return pl.pallas_call(
        _merge_body,
        out_shape=jax.ShapeDtypeStruct((m, D_MODEL), BF16),
        grid=(m // tm, D_MODEL // tn, 4),
        in_specs=[pl.BlockSpec((tm, D_MODEL), lambda i, j, n: (i, 0)),
                  pl.BlockSpec((None, 1, D_MODEL, tn), lambda i, j, n: (layer, n, 0, j)),
                  pl.BlockSpec((None, 1, BRANCH, tn), lambda i, j, n: (layer, n, 0, j)),
                  pl.BlockSpec((tm, BRANCH), ymap), pl.BlockSpec((tm, BRANCH), ymap),
                  pl.BlockSpec((tm, BRANCH), ymap), pl.BlockSpec((tm, BRANCH), ymap)],
        out_specs=pl.BlockSpec((tm, tn), lambda i, j, n: (i, j)),
        scratch_shapes=[pltpu.VMEM((tm, tn), F32)],
        compiler_params=_cp(("parallel", "parallel", "arbitrary"), 56),
        name="gated_merge",
    )(h, w_gate, w_branch, *ys)


def _ffn_up_body(h_ref, halo_ref, wg_ref, wu_ref, cw_ref, o_ref, *, tiles_per_seq):
    i = pl.program_id(0)
    wg = wg_ref[...].astype(BF16)
    g = _dot(h_ref[...], wg)
    u = _dot(h_ref[...], wu_ref[...].astype(BF16))
    gh = _dot(halo_ref[...], wg)
    gh = gh * jnp.where(i % tiles_per_seq == 0, 0.0, 1.0)
    cw = cw_ref[...]
    row8 = lax.broadcasted_iota(I32, (8, 1), 0)
    y = g * cw[FFN_CONV - 1:FFN_CONV, :]
    for j in range(1, FFN_CONV):
        rolled = pltpu.roll(g, j, 0)
        head = jnp.where(row8 < j, pltpu.roll(gh, j, 0), rolled[0:8])
        shifted = jnp.concatenate([head, rolled[8:]], axis=0)
        y = y + shifted * cw[FFN_CONV - 1 - j:FFN_CONV - j, :]
    o_ref[...] = (_silu(y) * u).astype(o_ref.dtype)


def ffn_up(h, w_gate, w_up, conv_w, layer, seq, tm=1024, tn=256):
    m = h.shape[0]
    tm = min(tm, seq)
    dff = w_gate.shape[-1]
    wspec = pl.BlockSpec((None, D_MODEL, tn), lambda i, j: (layer, 0, j))
    return pl.pallas_call(
        functools.partial(_ffn_up_body, tiles_per_seq=seq // tm),
        out_shape=jax.ShapeDtypeStruct((m, dff), BF16),
        grid=(m // tm, dff // tn),
        in_specs=[pl.BlockSpec((tm, D_MODEL), lambda i, j: (i, 0)),
                  pl.BlockSpec((8, D_MODEL), lambda i, j: (jnp.maximum(i * (tm // 8) - 1, 0), 0)),
                  wspec, wspec,
                  pl.BlockSpec((None, FFN_CONV, tn), lambda i, j: (layer, 0, j))],
        out_specs=pl.BlockSpec((tm, tn), lambda i, j: (i, j)),
        compiler_params=_cp(("parallel", "arbitrary")),
        name="ffn_up",
    )(h, h, w_gate, w_up, conv_w.astype(F32))


def _in_proj_weights(w, l):
    o = 4 * BRANCH
    secs = {}
    for name, width in (("cq", 768), ("ckv", 512), ("kidx", 64), ("widx", 16), ("z", 1024),
                        ("xbc", MB_CONV_DIM), ("dt", 16), ("fx", 3072), ("ff", 8)):
        secs[name] = w[l, :, o:o + width]
        o += width
    zeros = lambda n: jnp.zeros((w.shape[1], n), w.dtype)
    misc = jnp.concatenate([secs["ckv"], secs["kidx"], secs["widx"], zeros(LANES - 80),
                            secs["dt"], zeros(LANES - 16), secs["ff"], zeros(LANES - 8)], axis=1)
    return dict(cq=secs["cq"], misc=misc, z=secs["z"], xbc=secs["xbc"], fx=secs["fx"])


def _layer(x, l, batch, seq, prm):
    h = rmsnorm(x, prm["attn_norm"][l], BF16)
    w = _in_proj_weights(prm["w_in_bf"], l)
    p_hg = matmul(h, prm["w_in_bf"], F32, 1024, 512, n=4 * BRANCH, b_lead=(l,))
    p_cq = matmul(h, w["cq"], F32, 1024, 768)
    p_misc = matmul(h, w["misc"], F32, 1024, MISC_W)
    p_z = matmul(h, w["z"], F32, 1024, 512)
    p_xbc = matmul(h, w["xbc"], F32, 1024, 512)
    fx_scale = jnp.concatenate([jnp.full((1, BRANCH), FX_D ** -0.5 * LOG2E, F32), jnp.ones((1, 2 * BRANCH), F32)], 1)
    p_fx = matmul(h, w["fx"], BF16, 1024, 512, col_scale=fx_scale)

    y_a = hgrn2(p_hg, prm["hgrn_lb_logits"], prm["hgrn_norm"][l], l, batch, seq)

    c_q = rmsnorm(p_cq, prm["dsa_q_norm"][l], BF16)
    c_kv = rmsnorm(p_misc, prm["dsa_kv_norm"][l], BF16, width=DS_KV_LORA, col_block=0)
    k_idx = p_misc[:, MISC_IDX:MISC_IDX + DS_IDX_DIM].astype(BF16)
    w_q = jnp.concatenate([prm["dsa_w_uq"][l].astype(BF16), prm["dsa_w_iq"][l].astype(BF16)], axis=1)
    qcat = matmul(c_q, w_q, BF16, 1024, 1024)
    dsa_tq = min(256, seq)
    q_lat = dsa_qlat(qcat, prm["dsa_w_uk"][l].astype(BF16), dsa_tq)
    y_b = dsa_attention(qcat, p_misc, q_lat, k_idx, c_kv, prm["dsa_w_uv"][l].astype(BF16), batch, seq,
                        tq=dsa_tq)

    y_c = mamba2(p_z, p_xbc, p_misc, prm["ssm_conv_w"][l], prm["ssm_conv_b"][l], prm["ssm_dt_bias"][l],
                 prm["ssm_a_log"][l], prm["ssm_d"][l], prm["ssm_norm"][l], batch, seq)

    qb, kb = fox_bias_columns(p_misc, prm["fox_f_bias"][l], batch, seq)
    y_d = fox_attention(p_fx, qb, kb, batch, seq)

    merged = gated_merge(h, prm["w_gate"], prm["w_branch"], l, (y_a, y_b, y_c, y_d))
    x = matmul(merged, prm["w_out_bf"], F32, 1024, 512, residual=x, b_lead=(l,))

    h2 = rmsnorm(x, prm["ffn_norm"][l], BF16)
    act = ffn_up(h2, prm["ffn_w_gate"], prm["ffn_w_up"], prm["ffn_conv"], l, seq)
    return matmul(act, prm["w_down_bf"], F32, 512, 256, residual=x, b_lead=(l,))


def kernel(x, attn_norm, ffn_norm, final_norm, w_in, hgrn_lb_logits, hgrn_norm, dsa_q_norm, dsa_kv_norm,
           dsa_w_uq, dsa_w_iq, dsa_w_uk, dsa_w_uv, ssm_conv_w, ssm_conv_b, ssm_dt_bias, ssm_a_log, ssm_d,
           ssm_norm, fox_f_bias, w_gate, w_branch, w_out, ffn_w_gate, ffn_w_up, ffn_conv, ffn_w_down):
    batch, seq, d = x.shape
    prm = dict(attn_norm=attn_norm, ffn_norm=ffn_norm, w_in=w_in, hgrn_lb_logits=hgrn_lb_logits,
               hgrn_norm=hgrn_norm, dsa_q_norm=dsa_q_norm, dsa_kv_norm=dsa_kv_norm, dsa_w_uq=dsa_w_uq,
               dsa_w_iq=dsa_w_iq, dsa_w_uk=dsa_w_uk, dsa_w_uv=dsa_w_uv, ssm_conv_w=ssm_conv_w,
               ssm_conv_b=ssm_conv_b, ssm_dt_bias=ssm_dt_bias, ssm_a_log=ssm_a_log, ssm_d=ssm_d,
               ssm_norm=ssm_norm, fox_f_bias=fox_f_bias, w_gate=w_gate, w_branch=w_branch, w_out=w_out,
               ffn_w_gate=ffn_w_gate, ffn_w_up=ffn_w_up, ffn_conv=ffn_conv, ffn_w_down=ffn_w_down)
    prm.update(w_in_bf=w_in.astype(BF16), w_out_bf=w_out.astype(BF16), w_down_bf=ffn_w_down.astype(BF16))
    xf = x.reshape(batch * seq, d)
    for l in range(DEPTH):
        xf = _layer(xf, l, batch, seq, prm)
    return rmsnorm(xf, final_norm, x.dtype).reshape(batch, seq, d)
```

```python
import functools

import jax
import jax.numpy as jnp
from jax import lax
from jax.experimental import pallas as pl
from jax.experimental.pallas import tpu as pltpu

F32, BF16, I32 = jnp.float32, jnp.bfloat16, jnp.int32
HIGHEST = lax.Precision.HIGHEST

D_MODEL = 4096
DEPTH = 2
BRANCH = 1024
HG_HEADS, HG_D, HG_CHUNK, HG_SUB = 8, 128, 64, 16
DS_HEADS, DS_HEAD_DIM, DS_Q_LORA, DS_KV_LORA = 8, 128, 768, 512
DS_IDX_HEADS, DS_IDX_DIM, DS_TOPK_MAX = 16, 64, 256
SOFTMAX_ROWS = 64
MB_HEADS, MB_P, MB_N, MB_GROUPS, MB_CONV, MB_CHUNK = 16, 64, 128, 2, 4, 128
MB_CONV_DIM = BRANCH + 2 * MB_GROUPS * MB_N
FX_HEADS, FX_D = 8, 128
D_FF = 11008
FFN_CONV = 3
EPS = 1e-6
LANES = 128
NEG_BIG = -1e30
LOG2E = 1.4426950408889634
INT_MIN = -(2 ** 31)

MISC_CKV = 0
MISC_IDX = 512
MISC_DT = 640
MISC_FF = 768
MISC_W = 896


def _cp(sem, vmem_mb=48):
    return pltpu.CompilerParams(dimension_semantics=sem, vmem_limit_bytes=vmem_mb * 2 ** 20)


def _dot(a, b, precision=None):
    return jnp.dot(a, b, preferred_element_type=F32, precision=precision)


def _dot_nt(a, b):
    return lax.dot_general(a, b, (((1,), (1,)), ((), ())), preferred_element_type=F32)


def _dot_tn(a, b):
    return lax.dot_general(a, b, (((0,), (0,)), ((), ())), preferred_element_type=F32)


def _tril(n):
    r = lax.broadcasted_iota(I32, (n, n), 0)
    c = lax.broadcasted_iota(I32, (n, n), 1)
    return (r >= c).astype(F32)


def _silu(x):
    return x * jax.nn.sigmoid(x)


def _rmsnorm_body(x_ref, g_ref, o_ref):
    x = x_ref[...].astype(F32)
    ms = jnp.mean(x * x, axis=-1, keepdims=True)
    o_ref[...] = (x * lax.rsqrt(ms + EPS) * g_ref[...]).astype(o_ref.dtype)


def rmsnorm(x, gain, out_dtype, width=None, col_block=0, tm=256):
    m = x.shape[0]
    width = x.shape[1] if width is None else width
    return pl.pallas_call(
        _rmsnorm_body,
        out_shape=jax.ShapeDtypeStruct((m, width), out_dtype),
        grid=(m // tm,),
        in_specs=[pl.BlockSpec((tm, width), lambda i: (i, col_block)),
                  pl.BlockSpec((1, width), lambda i: (0, 0))],
        out_specs=pl.BlockSpec((tm, width), lambda i: (i, 0)),
        compiler_params=_cp(("parallel",)),
        name="rmsnorm",
    )(x, gain.reshape(1, width).astype(F32))


def _mm_body(a_ref, b_ref, o_ref):
    o_ref[...] = _dot(a_ref[...], b_ref[...].astype(BF16)).astype(o_ref.dtype)


def _mm_add_body(a_ref, b_ref, r_ref, o_ref):
    o_ref[...] = (r_ref[...] + _dot(a_ref[...], b_ref[...].astype(BF16))).astype(o_ref.dtype)


def _mm_scale_body(a_ref, b_ref, s_ref, o_ref):
    o_ref[...] = (_dot(a_ref[...], b_ref[...].astype(BF16)) * s_ref[...]).astype(o_ref.dtype)


def matmul(a, b, out_dtype, tm, tn, residual=None, col_scale=None, n=None, b_lead=(), vmem_mb=48):
    m, k = a.shape
    n = b.shape[-1] if n is None else n
    tm, tn = min(tm, m), min(tn, n)
    in_specs = [pl.BlockSpec((tm, k), lambda i, j: (i, 0)),
                pl.BlockSpec((None,) * len(b_lead) + (k, tn), lambda i, j: tuple(b_lead) + (0, j))]
    args, body = (a, b), _mm_body
    if residual is not None:
        in_specs.append(pl.BlockSpec((tm, tn), lambda i, j: (i, j)))
        args, body = (a, b, residual), _mm_add_body
    elif col_scale is not None:
        in_specs.append(pl.BlockSpec((1, tn), lambda i, j: (0, j)))
        args, body = (a, b, col_scale), _mm_scale_body
    return pl.pallas_call(
        body,
        out_shape=jax.ShapeDtypeStruct((m, n), out_dtype),
        grid=(m // tm, n // tn),
        in_specs=in_specs,
        out_specs=pl.BlockSpec((tm, tn), lambda i, j: (i, j)),
        compiler_params=_cp(("parallel", "arbitrary"), vmem_mb),
        name="matmul",
    )(*args)


def _hgrn_body(lbl_ref, gain_ref, q_ref, f_ref, i_ref, g_ref, o_ref, st_ref, *, layer, nchunks, hpb):
    c = pl.program_id(2)

    @pl.when(c == 0)
    def _():
        st_ref[...] = jnp.zeros_like(st_ref)

    logits = lbl_ref[...]
    e = jnp.exp(logits - jnp.max(logits, axis=0, keepdims=True))
    p = e / jnp.sum(e, axis=0, keepdims=True)
    lb_all = jnp.sum(p[0:layer + 1], axis=0, keepdims=True) - p[0:1]
    gain_all = gain_ref[...]
    tril = _tril(HG_CHUNK)
    row = lax.broadcasted_iota(I32, (HG_CHUNK, 1), 0)
    row_in_sub = row % HG_SUB
    nsub = HG_CHUNK // HG_SUB

    def chunk(ci, carry):
        sl = pl.ds(pl.multiple_of(ci * HG_CHUNK, HG_CHUNK), HG_CHUNK)
        for hh in range(hpb):
            hs = slice(hh * HG_D, (hh + 1) * HG_D)
            lb, gain = lb_all[:, hs], gain_all[:, hs]
            q = _silu(q_ref[sl, hs])
            v = _silu(i_ref[sl, hs])
            f = lb + (1.0 - lb) * jax.nn.sigmoid(f_ref[sl, hs])
            k = 1.0 - f
            b = _dot(tril, jnp.log(f) * LOG2E, HIGHEST)
            st = st_ref[hh]

            o = _dot_nt((q * jnp.exp2(b)).astype(BF16), st.astype(BF16))

            intra = jnp.zeros((HG_CHUNK, HG_D), F32)
            for d in range(HG_SUB):
                ks = k if d == 0 else pltpu.roll(k, d, 0)
                bs = b if d == 0 else pltpu.roll(b, d, 0)
                vs = v if d == 0 else pltpu.roll(v, d, 0)
                expo = jnp.where(row_in_sub >= d, b - bs, -jnp.inf)
                w = jnp.sum(q * ks * jnp.exp2(expo), axis=-1, keepdims=True)
                intra = intra + w * vs
            o = o + intra

            parts = [jnp.zeros((HG_SUB, HG_D), F32)]
            for si in range(1, nsub):
                lo = si * HG_SUB
                r = b[lo - 1:lo, :]
                qi = (q[lo:lo + HG_SUB] * jnp.exp2(b[lo:lo + HG_SUB] - r)).astype(BF16)
                kj = (k[0:lo] * jnp.exp2(r - b[0:lo])).astype(BF16)
                sc = _dot_nt(qi, kj)
                parts.append(_dot(sc.astype(BF16), v[0:lo].astype(BF16)))
            o = o + jnp.concatenate(parts, axis=0)

            b_last = b[HG_CHUNK - 1:HG_CHUNK, :]
            kd = (k * jnp.exp2(b_last - b)).astype(BF16)
            st_ref[hh] = st * jnp.exp2(b_last) + _dot_tn(v.astype(BF16), kd)

            og = o * jax.nn.sigmoid(g_ref[sl, hs])
            ms = jnp.mean(og * og, axis=-1, keepdims=True)
            o_ref[sl, hs] = (og * lax.rsqrt(ms + EPS) * gain).astype(o_ref.dtype)
        return carry

    lax.fori_loop(0, nchunks, chunk, 0, unroll=True)


def hgrn2(p_hg, lb_logits, norm_gain, layer, batch, seq, tt=256, hpb=4):
    tt = min(tt, seq)
    nt = seq // tt
    hb = HG_HEADS // hpb
    w = hpb * HG_D

    def col(sec):
        return lambda b, h, c: (b * nt + c, sec * hb + h)

    return pl.pallas_call(
        functools.partial(_hgrn_body, layer=layer, nchunks=tt // HG_CHUNK, hpb=hpb),
        out_shape=jax.ShapeDtypeStruct((batch * seq, BRANCH), BF16),
        grid=(batch, hb, nt),
        in_specs=[pl.BlockSpec((DEPTH, w), lambda b, h, c: (0, h)),
                  pl.BlockSpec((1, w), lambda b, h, c: (0, h)),
                  pl.BlockSpec((tt, w), col(0)),
                  pl.BlockSpec((tt, w), col(1)),
                  pl.BlockSpec((tt, w), col(2)),
                  pl.BlockSpec((tt, w), col(3))],
        out_specs=pl.BlockSpec((tt, w), lambda b, h, c: (b * nt + c, h)),
        scratch_shapes=[pltpu.VMEM((hpb, HG_D, HG_D), F32)],
        compiler_params=_cp(("parallel", "parallel", "arbitrary")),
        name="hgrn2",
    )(lb_logits.astype(F32), norm_gain.reshape(1, BRANCH).astype(F32), p_hg, p_hg, p_hg, p_hg)


def _mamba_body(z_ref, xbc_ref, dt_ref, cw_ref, cb_ref, dtb_ref, alog_ref, dsk_ref, gain_ref,
                o_ref, prev_ref, st_ref):
    c = pl.program_id(1)
    L = MB_CHUNK

    @pl.when(c == 0)
    def _():
        prev_ref[...] = jnp.zeros_like(prev_ref)
        st_ref[...] = jnp.zeros_like(st_ref)

    x = xbc_ref[...]
    prev = prev_ref[...]
    row = lax.broadcasted_iota(I32, (L, 1), 0)
    cw = cw_ref[...]
    acc = x * cw[MB_CONV - 1:MB_CONV, :] + cb_ref[...]
    for j in range(1, MB_CONV):
        sh = jnp.where(row >= j, pltpu.roll(x, j, 0), pltpu.roll(prev, j, 0))
        acc = acc + sh * cw[MB_CONV - 1 - j:MB_CONV - j, :]
    prev_ref[...] = x
    xbc = _silu(acc)
    xs = xbc[:, 0:BRANCH]
    bm = xbc[:, BRANCH:BRANCH + MB_GROUPS * MB_N]
    cm = xbc[:, BRANCH + MB_GROUPS * MB_N:MB_CONV_DIM]

    raw = dt_ref[...] + dtb_ref[...]
    dt = jnp.maximum(raw, 0.0) + jnp.log1p(jnp.exp(-jnp.abs(raw)))
    a = -jnp.exp(alog_ref[...]) * dt
    tril = _tril(L)
    a_cs = _dot(tril, a, HIGHEST)
    a_cs_t = a_cs.T
    causal = tril > 0.5
    dsk = dsk_ref[...]

    hpg = MB_HEADS // MB_GROUPS
    ys = []
    for g in range(MB_GROUPS):
        bg = bm[:, g * MB_N:(g + 1) * MB_N]
        cg = cm[:, g * MB_N:(g + 1) * MB_N]
        cb = _dot_nt(cg.astype(BF16), bg.astype(BF16))
        for hh in range(hpg):
            h = g * hpg + hh
            acol = a_cs[:, h:h + 1]
            arow = a_cs_t[h:h + 1, :]
            lmat = jnp.exp(jnp.where(causal, acol - arow, -jnp.inf))
            xh = xs[:, h * MB_P:(h + 1) * MB_P]
            xdt = (xh * dt[:, h:h + 1]).astype(BF16)
            y = _dot((cb * lmat).astype(BF16), xdt)
            st = st_ref[h]
            y = y + _dot(cg.astype(BF16), st.astype(BF16)) * jnp.exp(acol)
            a_last = a_cs[L - 1:L, h:h + 1]
            bdec = (bg * jnp.exp(a_last - acol)).astype(BF16)
            st_ref[h] = st * jnp.exp(a_last) + _dot_tn(bdec, xdt)
            ys.append(y + dsk[:, h:h + 1] * xh)
    y = jnp.concatenate(ys, axis=1)
    y = y * _silu(z_ref[...])
    gw = BRANCH // MB_GROUPS
    outs = []
    for g in range(MB_GROUPS):
        yg = y[:, g * gw:(g + 1) * gw]
        ms = jnp.mean(yg * yg, axis=-1, keepdims=True)
        outs.append(yg * lax.rsqrt(ms + EPS))
    o_ref[...] = (jnp.concatenate(outs, axis=1) * gain_ref[...]).astype(o_ref.dtype)


def _pad_lanes(v, n=LANES):
    v = v.reshape(1, -1).astype(F32)
    return jnp.pad(v, ((0, 0), (0, n - v.shape[1])))


def mamba2(p_z, p_xbc, p_misc, conv_w, conv_b, dt_bias, a_log, d_skip, norm_gain, batch, seq):
    nc = seq // MB_CHUNK
    L = MB_CHUNK
    full = lambda shape: pl.BlockSpec(shape, lambda b, c: (0, 0))
    return pl.pallas_call(
        _mamba_body,
        out_shape=jax.ShapeDtypeStruct((batch * seq, BRANCH), BF16),
        grid=(batch, nc),
        in_specs=[pl.BlockSpec((L, BRANCH), lambda b, c: (b * nc + c, 0)),
                  pl.BlockSpec((L, MB_CONV_DIM), lambda b, c: (b * nc + c, 0)),
                  pl.BlockSpec((L, LANES), lambda b, c: (b * nc + c, MISC_DT // LANES)),
                  full((MB_CONV, MB_CONV_DIM)), full((1, MB_CONV_DIM)),
                  full((1, LANES)), full((1, LANES)), full((1, LANES)), full((1, BRANCH))],
        out_specs=pl.BlockSpec((L, BRANCH), lambda b, c: (b * nc + c, 0)),
        scratch_shapes=[pltpu.VMEM((L, MB_CONV_DIM), F32),
                        pltpu.VMEM((MB_HEADS, MB_N, MB_P), F32)],
        compiler_params=_cp(("parallel", "arbitrary")),
        name="mamba2",
    )(p_z, p_xbc, p_misc, conv_w.astype(F32), conv_b.reshape(1, -1).astype(F32),
      _pad_lanes(dt_bias), _pad_lanes(a_log), _pad_lanes(d_skip), norm_gain.reshape(1, BRANCH).astype(F32))


def _fox_cum_body(f_ref, bias_ref, qb_ref, kb_ref, carry_ref, *, tt):
    c = pl.program_id(1)

    @pl.when(c == 0)
    def _():
        carry_ref[...] = jnp.zeros_like(carry_ref)

    logf = jax.nn.log_sigmoid(f_ref[...] + bias_ref[...])
    cum = _dot(_tril(tt), logf, HIGHEST) + carry_ref[...]
    carry_ref[...] = cum[tt - 1:tt, :]

    c2 = cum * LOG2E
    lane = lax.broadcasted_iota(I32, (1, LANES), 1)
    ones = jnp.where(lane < 6, 1.0, 0.0)
    for h in range(FX_HEADS):
        col = c2[:, h:h + 1]
        hi = col.astype(BF16).astype(F32)
        r1 = col - hi
        mid = r1.astype(BF16).astype(F32)
        lo = r1 - mid
        qb = jnp.where(lane == 0, hi, jnp.where(lane == 1, mid, jnp.where(lane == 2, lo, ones)))
        kb = jnp.where(lane == 3, -hi, jnp.where(lane == 4, -mid, jnp.where(lane == 5, -lo, ones)))
        qb_ref[:, h * LANES:(h + 1) * LANES] = qb.astype(BF16)
        kb_ref[:, h * LANES:(h + 1) * LANES] = kb.astype(BF16)


def fox_bias_columns(p_misc, f_bias, batch, seq, tt=512):
    tt = min(tt, seq)
    nt = seq // tt
    out = jax.ShapeDtypeStruct((batch * seq, FX_HEADS * LANES), BF16)
    ospec = pl.BlockSpec((tt, FX_HEADS * LANES), lambda b, c: (b * nt + c, 0))
    return pl.pallas_call(
        functools.partial(_fox_cum_body, tt=tt),
        out_shape=(out, out),
        grid=(batch, nt),
        in_specs=[pl.BlockSpec((tt, LANES), lambda b, c: (b * nt + c, MISC_FF // LANES)),
                  pl.BlockSpec((1, LANES), lambda b, c: (0, 0))],
        out_specs=(ospec, ospec),
        scratch_shapes=[pltpu.VMEM((1, LANES), F32)],
        compiler_params=_cp(("parallel", "arbitrary")),
        name="fox_bias_columns",
    )(p_misc, _pad_lanes(f_bias))


def _fox_body(q_ref, qb_ref, k_ref, kb_ref, v_ref, o_ref, m_ref, l_ref, acc_ref, *, tq, tk):
    i = pl.program_id(1)
    j = pl.program_id(2)

    @pl.when(j == 0)
    def _():
        m_ref[...] = jnp.full_like(m_ref, NEG_BIG)
        l_ref[...] = jnp.zeros_like(l_ref)
        acc_ref[...] = jnp.zeros_like(acc_ref)

    def block(masked):
        if masked:
            t_glob = i * tq + lax.broadcasted_iota(I32, (tq, 1), 0)
            s_glob = j * tk + lax.broadcasted_iota(I32, (1, tk), 1)
            causal = s_glob <= t_glob
        for h in range(FX_HEADS):
            hs = slice(h * FX_D, (h + 1) * FX_D)
            qa = jnp.concatenate([q_ref[:, hs], qb_ref[:, hs]], axis=1)
            ka = jnp.concatenate([k_ref[:, hs], kb_ref[:, hs]], axis=1)
            s = _dot_nt(qa, ka)
            if masked:
                s = jnp.where(causal, s, -jnp.inf)
            m_prev = m_ref[h][:, 0:1]
            m_new = jnp.maximum(m_prev, jnp.max(s, axis=1, keepdims=True))
            alpha = jnp.exp2(m_prev - m_new)
            p = jnp.exp2(s - m_new)
            l_new = alpha * l_ref[h][:, 0:1] + jnp.sum(p, axis=1, keepdims=True)
            acc_ref[h] = alpha * acc_ref[h] + _dot(p.astype(BF16), v_ref[:, hs])
            m_ref[h] = jnp.broadcast_to(m_new, (tq, LANES))
            l_ref[h] = jnp.broadcast_to(l_new, (tq, LANES))

    first_q, last_q = i * tq, i * tq + tq - 1
    first_k, last_k = j * tk, j * tk + tk - 1

    @pl.when(last_k <= first_q)
    def _():
        block(False)

    @pl.when((last_k > first_q) & (first_k <= last_q))
    def _():
        block(True)

    @pl.when(j == pl.num_programs(2) - 1)
    def _():
        for h in range(FX_HEADS):
            o_ref[:, h * FX_D:(h + 1) * FX_D] = (acc_ref[h] / l_ref[h][:, 0:1]).astype(o_ref.dtype)


def fox_attention(p_fx, qb, kb, batch, seq, tq=256, tk=512):
    tq, tk = min(tq, seq), min(tk, seq)
    nq, nk = seq // tq, seq // tk

    def kv_blk(i, j):
        return jnp.minimum(j, (i * tq + tq - 1) // tk)

    qspec = lambda colblk: pl.BlockSpec((tq, BRANCH), lambda b, i, j: (b * nq + i, colblk))
    kspec = lambda colblk: pl.BlockSpec((tk, BRANCH), lambda b, i, j: (b * nk + kv_blk(i, j), colblk))
    return pl.pallas_call(
        functools.partial(_fox_body, tq=tq, tk=tk),
        out_shape=jax.ShapeDtypeStruct((batch * seq, BRANCH), BF16),
        grid=(batch, nq, nk),
        in_specs=[qspec(0), qspec(0), kspec(1), kspec(0), kspec(2)],
        out_specs=pl.BlockSpec((tq, BRANCH), lambda b, i, j: (b * nq + i, 0)),
        scratch_shapes=[pltpu.VMEM((FX_HEADS, tq, LANES), F32),
                        pltpu.VMEM((FX_HEADS, tq, LANES), F32),
                        pltpu.VMEM((FX_HEADS, tq, FX_D), F32)],
        compiler_params=_cp(("parallel", "parallel", "arbitrary")),
        name="fox_attention",
    )(p_fx, qb, p_fx, kb, p_fx)


def _qlat_body(q_ref, wuk_ref, o_ref):
    r = _dot_nt(q_ref[...], wuk_ref[...]) * (DS_HEAD_DIM ** -0.5 * LOG2E)
    o_ref[...] = r.astype(o_ref.dtype).reshape(o_ref.shape)


def dsa_qlat(qcat, w_uk, tq, tm=1024):
    m = qcat.shape[0]
    tm = min(tm, m)
    return pl.pallas_call(
        _qlat_body,
        out_shape=jax.ShapeDtypeStruct((m // tq, DS_HEADS, tq, DS_KV_LORA), BF16),
        grid=(m // tm, DS_HEADS),
        in_specs=[pl.BlockSpec((tm, DS_HEAD_DIM), lambda i, h: (i, h)),
                  pl.BlockSpec((DS_KV_LORA, DS_HEAD_DIM), lambda i, h: (0, h))],
        out_specs=pl.BlockSpec((tm // tq, 1, tq, DS_KV_LORA), lambda i, h: (i, h, 0, 0)),
        compiler_params=_cp(("parallel", "arbitrary")),
        name="dsa_qlat",
    )(qcat, w_uk)


def _dsa_body(qi_ref, misc_ref, ql_ref, kidx_ref, ckv_ref, wuv_ref, o_ref,
              keys_ref, wb_ref, thr_ref, s_ref, p_ref, m_ref, l_ref, alpha_ref, acc_ref, *, tq, tk, topk):
    i = pl.program_id(1)
    nh = DS_HEADS
    rows = nh * tq
    nkb = (i * tq + tq - 1) // tk + 1
    t_glob = i * tq + lax.broadcasted_iota(I32, (tq, 1), 0)
    col = lax.broadcasted_iota(I32, (1, tk), 1)
    w = misc_ref[:, DS_IDX_DIM:DS_IDX_DIM + DS_IDX_HEADS] * (DS_IDX_HEADS ** -0.5 * DS_IDX_DIM ** -0.5)
    for h in range(DS_IDX_HEADS):
        wb_ref[h] = jnp.broadcast_to(w[:, h:h + 1], (tq, LANES))
    qh = [qi_ref[:, h * DS_IDX_DIM:(h + 1) * DS_IDX_DIM] for h in range(DS_IDX_HEADS)]
    nlt = tk // LANES

    def score_body(j, carry):
        kb = kidx_ref[pl.ds(pl.multiple_of(j * tk, tk), tk), :]
        tiles = [jnp.zeros((tq, LANES), F32) for _ in range(nlt)]
        for h in range(DS_IDX_HEADS):
            lg = jnp.maximum(_dot_nt(qh[h], kb), 0.0)
            wbh = wb_ref[h]
            tiles = [t + lg[:, cc * LANES:(cc + 1) * LANES] * wbh for cc, t in enumerate(tiles)]
        sc = jnp.concatenate(tiles, axis=1)
        sc = jnp.where(j * tk + col <= t_glob, sc, -jnp.inf)
        bits = pltpu.bitcast(sc, I32)
        keys_ref[j] = jnp.where(bits < 0, bits ^ 0x7FFFFFFF, bits)
        return carry

    lax.fori_loop(0, nkb, score_body, 0)

    def count_ge(cand):
        def body(j, acc):
            ge = jnp.where(keys_ref[j] >= cand, 1.0, 0.0)
            part = ge[:, 0:LANES]
            for cc in range(1, tk // LANES):
                part = part + ge[:, cc * LANES:(cc + 1) * LANES]
            return acc + part
        acc = lax.fori_loop(0, nkb, body, jnp.zeros((tq, LANES), F32))
        return jnp.sum(acc, axis=1, keepdims=True)

    kf = float(topk)
    c0 = count_ge(jnp.zeros((tq, 1), I32))
    nonneg = c0 >= kf
    thr = jnp.where(nonneg, 0, INT_MIN).astype(I32)
    cnt = jnp.where(nonneg, c0, (nkb * tk).astype(F32))

    def search_bit(bi, thr, cnt):
        cand = thr | lax.shift_left(jnp.int32(1), bi)
        c = count_ge(cand)
        take = c >= kf
        return jnp.where(take, cand, thr), jnp.where(take, c, cnt)

    thr, cnt = search_bit(jnp.int32(30), thr, cnt)
    bits_per_round = 3

    def search_cond(carry):
        bi, _, cnt = carry
        return (bi >= 0) & (jnp.max(cnt) > kf)

    def search_round(carry):
        bi, thr, cnt = carry
        for r in range(bits_per_round):
            thr, cnt = search_bit(bi - r, thr, cnt)
        return bi - bits_per_round, thr, cnt

    _, thr, _ = lax.while_loop(search_cond, search_round, (jnp.int32(29), thr, cnt))

    thr_ref[...] = thr
    m_ref[...] = jnp.full_like(m_ref, NEG_BIG)
    l_ref[...] = jnp.zeros_like(l_ref)
    acc_ref[...] = jnp.zeros_like(acc_ref)
    rc = SOFTMAX_ROWS
    chunks_per_head = tq // rc

    def attn_body(j, carry):
        kv = ckv_ref[pl.ds(pl.multiple_of(j * tk, tk), tk), :]
        s_ref[...] = _dot_nt(ql_ref[0].reshape(rows, DS_KV_LORA), kv)

        for c in range(rows // rc):
            tok0 = (c // nh) * rc
            r0 = (c % nh) * tq + tok0
            rsl, tsl = slice(r0, r0 + rc), slice(tok0, tok0 + rc)
            if c % nh == 0:
                tg = i * tq + tok0 + lax.broadcasted_iota(I32, (rc, 1), 0)
                sel = (keys_ref[j, tsl, :] >= thr_ref[tsl, :]) & (j * tk + col <= tg)
            s = jnp.where(sel, s_ref[rsl, :], -jnp.inf)
            m_prev = m_ref[rsl, :]
            m_new = jnp.maximum(m_prev, jnp.max(s, axis=1, keepdims=True))
            alpha = jnp.exp2(m_prev - m_new)
            p = jnp.exp2(s - m_new)
            l_ref[rsl, :] = alpha * l_ref[rsl, :] + jnp.sum(p, axis=1, keepdims=True)
            m_ref[rsl, :] = m_new
            alpha_ref[rsl, :] = jnp.broadcast_to(alpha, (rc, LANES))
            p_ref[rsl, :] = p.astype(BF16)

        pv = _dot(p_ref[...], kv)
        alpha_b = alpha_ref[...]
        for cc in range(DS_KV_LORA // LANES):
            lsl = slice(cc * LANES, (cc + 1) * LANES)
            acc_ref[:, lsl] = alpha_b * acc_ref[:, lsl] + pv[:, lsl]
        return carry

    lax.fori_loop(0, nkb, attn_body, 0)

    for h in range(nh):
        hsl = slice(h * tq, (h + 1) * tq)
        o_lat = (acc_ref[hsl, :] / l_ref[hsl, :]).astype(BF16)
        o_ref[:, h * DS_HEAD_DIM:(h + 1) * DS_HEAD_DIM] = _dot(
            o_lat, wuv_ref[:, h * DS_HEAD_DIM:(h + 1) * DS_HEAD_DIM]).astype(o_ref.dtype)


def dsa_attention(qcat, p_misc, q_lat, k_idx, c_kv, w_uv, batch, seq, tq=256, tk=512):
    tq, tk = min(tq, seq), min(tk, seq)
    nq = seq // tq
    rows = DS_HEADS * tq
    topk = min(DS_TOPK_MAX, seq // 4)
    assert tk >= topk and tq % SOFTMAX_ROWS == 0
    return pl.pallas_call(
        functools.partial(_dsa_body, tq=tq, tk=tk, topk=topk),
        out_shape=jax.ShapeDtypeStruct((batch * seq, BRANCH), BF16),
        grid=(batch, nq),
        in_specs=[pl.BlockSpec((tq, DS_IDX_HEADS * DS_IDX_DIM), lambda b, i: (b * nq + i, 1)),
                  pl.BlockSpec((tq, LANES), lambda b, i: (b * nq + i, MISC_IDX // LANES)),
                  pl.BlockSpec((1, DS_HEADS, tq, DS_KV_LORA), lambda b, i: (b * nq + i, 0, 0, 0)),
                  pl.BlockSpec((seq, DS_IDX_DIM), lambda b, i: (b, 0)),
                  pl.BlockSpec((seq, DS_KV_LORA), lambda b, i: (b, 0)),
                  pl.BlockSpec((DS_KV_LORA, BRANCH), lambda b, i: (0, 0))],
        out_specs=pl.BlockSpec((tq, BRANCH), lambda b, i: (b * nq + i, 0)),
        scratch_shapes=[pltpu.VMEM((seq // tk, tq, tk), I32),
                        pltpu.VMEM((DS_IDX_HEADS, tq, LANES), F32),
                        pltpu.VMEM((tq, 1), I32),
                        pltpu.VMEM((rows, tk), F32),
                        pltpu.VMEM((rows, tk), BF16),
                        pltpu.VMEM((rows, 1), F32),
                        pltpu.VMEM((rows, 1), F32),
                        pltpu.VMEM((rows, LANES), F32),
                        pltpu.VMEM((rows, DS_KV_LORA), F32)],
        compiler_params=_cp(("parallel", "arbitrary")),
        name="dsa_attention",
    )(qcat, p_misc, q_lat, k_idx, c_kv, w_uv)


def _merge_body(h_ref, wg_ref, wb_ref, ya_ref, yb_ref, yc_ref, yd_ref, o_ref, acc_ref):
    n = pl.program_id(2)
    gate = jax.nn.sigmoid(_dot(h_ref[...], wg_ref[0].astype(BF16)))
    wb = wb_ref[0].astype(BF16)
    for idx, y_ref in enumerate((ya_ref, yb_ref, yc_ref, yd_ref)):
        @pl.when(n == idx)
        def _(y_ref=y_ref, idx=idx):
            contrib = gate * _dot(y_ref[...], wb)
            if idx == 0:
                acc_ref[...] = contrib
            else:
                acc_ref[...] += contrib

    @pl.when(n == pl.num_programs(2) - 1)
    def _():
        o_ref[...] = acc_ref[...].astype(o_ref.dtype)


def gated_merge(h, w_gate, w_branch, layer, ys, tm=1024, tn=512):
    m = h.shape[0]
    tm = min(tm, m)
    ymap = lambda i, j, n: (i, 0)
    once = pl.Buffered(1)
    return pl.pallas_call(
        _merge_body,
        out_shape=jax.ShapeDtypeStruct((m, D_MODEL), BF16),
        grid=(m // tm, D_MODEL // tn, 4),
        in_specs=[pl.BlockSpec((tm, D_MODEL), lambda i, j, n: (i, 0), pipeline_mode=once),
                  pl.BlockSpec((None, 1, D_MODEL, tn), lambda i, j, n: (layer, n, 0, j)),
                  pl.BlockSpec((None, 1, BRANCH, tn), lambda i, j, n: (layer, n, 0, j))]
                 + [pl.BlockSpec((tm, BRANCH), ymap, pipeline_mode=once) for _ in range(4)],
        out_specs=pl.BlockSpec((tm, tn), lambda i, j, n: (i, j)),
        scratch_shapes=[pltpu.VMEM((tm, tn), F32)],
        compiler_params=_cp(("parallel", "parallel", "arbitrary"), 56),
        name="gated_merge",
    )(h, w_gate, w_branch, *ys)


def _ffn_up_body(h_ref, halo_ref, wg_ref, wu_ref, cw_ref, o_ref, *, tiles_per_seq):
    i = pl.program_id(0)
    wg = wg_ref[...].astype(BF16)
    g = _dot(h_ref[...], wg)
    u = _dot(h_ref[...], wu_ref[...].astype(BF16))
    gh = _dot(halo_ref[...], wg)
    gh = gh * jnp.where(i % tiles_per_seq == 0, 0.0, 1.0)
    cw = cw_ref[...]
    row8 = lax.broadcasted_iota(I32, (8, 1), 0)
    y = g * cw[FFN_CONV - 1:FFN_CONV, :]
    for j in range(1, FFN_CONV):
        rolled = pltpu.roll(g, j, 0)
        head = jnp.where(row8 < j, pltpu.roll(gh, j, 0), rolled[0:8])
        shifted = jnp.concatenate([head, rolled[8:]], axis=0)
        y = y + shifted * cw[FFN_CONV - 1 - j:FFN_CONV - j, :]
    o_ref[...] = (_silu(y) * u).astype(o_ref.dtype)


def ffn_up(h, w_gate, w_up, conv_w, layer, seq, tm=1024, tn=256):
    m = h.shape[0]
    tm = min(tm, seq)
    dff = w_gate.shape[-1]
    wspec = pl.BlockSpec((None, D_MODEL, tn), lambda i, j: (layer, 0, j))
    return pl.pallas_call(
        functools.partial(_ffn_up_body, tiles_per_seq=seq // tm),
        out_shape=jax.ShapeDtypeStruct((m, dff), BF16),
        grid=(m // tm, dff // tn),
        in_specs=[pl.BlockSpec((tm, D_MODEL), lambda i, j: (i, 0)),
                  pl.BlockSpec((8, D_MODEL), lambda i, j: (jnp.maximum(i * (tm // 8) - 1, 0), 0)),
                  wspec, wspec,
                  pl.BlockSpec((None, FFN_CONV, tn), lambda i, j: (layer, 0, j))],
        out_specs=pl.BlockSpec((tm, tn), lambda i, j: (i, j)),
        compiler_params=_cp(("parallel", "arbitrary")),
        name="ffn_up",
    )(h, h, w_gate, w_up, conv_w.astype(F32))


def _in_proj_weights(w, l):
    o = 4 * BRANCH
    secs = {}
    for name, width in (("cq", 768), ("ckv", 512), ("kidx", 64), ("widx", 16), ("z", 1024),
                        ("xbc", MB_CONV_DIM), ("dt", 16), ("fx", 3072), ("ff", 8)):
        secs[name] = w[l, :, o:o + width]
        o += width
    zeros = lambda n: jnp.zeros((w.shape[1], n), w.dtype)
    misc = jnp.concatenate([secs["ckv"], secs["kidx"], secs["widx"], zeros(LANES - 80),
                            secs["dt"], zeros(LANES - 16), secs["ff"], zeros(LANES - 8)], axis=1)
    return dict(cq=secs["cq"], misc=misc, z=secs["z"], xbc=secs["xbc"], fx=secs["fx"])


def _layer(x, l, batch, seq, prm):
    h = rmsnorm(x, prm["attn_norm"][l], BF16)
    w = _in_proj_weights(prm["w_in_bf"], l)
    p_hg = matmul(h, prm["w_in_bf"], F32, 1024, 1024, n=4 * BRANCH, b_lead=(l,))
    p_cq = matmul(h, w["cq"], F32, 1024, 768)
    p_misc = matmul(h, w["misc"], F32, 1024, MISC_W)
    p_z = matmul(h, w["z"], F32, 1024, 1024)
    p_xbc = matmul(h, w["xbc"], F32, 1024, 768)
    fx_scale = jnp.concatenate([jnp.full((1, BRANCH), FX_D ** -0.5 * LOG2E, F32), jnp.ones((1, 2 * BRANCH), F32)], 1)
    p_fx = matmul(h, w["fx"], BF16, 1024, 1024, col_scale=fx_scale)

    y_a = hgrn2(p_hg, prm["hgrn_lb_logits"], prm["hgrn_norm"][l], l, batch, seq)

    c_q = rmsnorm(p_cq, prm["dsa_q_norm"][l], BF16)
    c_kv = rmsnorm(p_misc, prm["dsa_kv_norm"][l], BF16, width=DS_KV_LORA, col_block=0)
    k_idx = p_misc[:, MISC_IDX:MISC_IDX + DS_IDX_DIM].astype(BF16)
    w_q = jnp.concatenate([prm["dsa_w_uq"][l].astype(BF16), prm["dsa_w_iq"][l].astype(BF16)], axis=1)
    qcat = matmul(c_q, w_q, BF16, 1024, 1024)
    dsa_tq = min(256, seq)
    q_lat = dsa_qlat(qcat, prm["dsa_w_uk"][l].astype(BF16), dsa_tq)
    y_b = dsa_attention(qcat, p_misc, q_lat, k_idx, c_kv, prm["dsa_w_uv"][l].astype(BF16), batch, seq,
                        tq=dsa_tq)

    y_c = mamba2(p_z, p_xbc, p_misc, prm["ssm_conv_w"][l], prm["ssm_conv_b"][l], prm["ssm_dt_bias"][l],
                 prm["ssm_a_log"][l], prm["ssm_d"][l], prm["ssm_norm"][l], batch, seq)

    qb, kb = fox_bias_columns(p_misc, prm["fox_f_bias"][l], batch, seq)
    y_d = fox_attention(p_fx, qb, kb, batch, seq)

    merged = gated_merge(h, prm["w_gate"], prm["w_branch"], l, (y_a, y_b, y_c, y_d))
    x = matmul(merged, prm["w_out_bf"], F32, 1024, 1024, residual=x, b_lead=(l,), vmem_mb=56)

    h2 = rmsnorm(x, prm["ffn_norm"][l], BF16)
    act = ffn_up(h2, prm["ffn_w_gate"], prm["ffn_w_up"], prm["ffn_conv"], l, seq)
    return matmul(act, prm["w_down_bf"], F32, 512, 512, residual=x, b_lead=(l,), vmem_mb=56)


def kernel(x, attn_norm, ffn_norm, final_norm, w_in, hgrn_lb_logits, hgrn_norm, dsa_q_norm, dsa_kv_norm,
           dsa_w_uq, dsa_w_iq, dsa_w_uk, dsa_w_uv, ssm_conv_w, ssm_conv_b, ssm_dt_bias, ssm_a_log, ssm_d,
           ssm_norm, fox_f_bias, w_gate, w_branch, w_out, ffn_w_gate, ffn_w_up, ffn_conv, ffn_w_down):
    batch, seq, d = x.shape
    prm = dict(attn_norm=attn_norm, ffn_norm=ffn_norm, w_in=w_in, hgrn_lb_logits=hgrn_lb_logits,
               hgrn_norm=hgrn_norm, dsa_q_norm=dsa_q_norm, dsa_kv_norm=dsa_kv_norm, dsa_w_uq=dsa_w_uq,
               dsa_w_iq=dsa_w_iq, dsa_w_uk=dsa_w_uk, dsa_w_uv=dsa_w_uv, ssm_conv_w=ssm_conv_w,
               ssm_conv_b=ssm_conv_b, ssm_dt_bias=ssm_dt_bias, ssm_a_log=ssm_a_log, ssm_d=ssm_d,
               ssm_norm=ssm_norm, fox_f_bias=fox_f_bias, w_gate=w_gate, w_branch=w_branch, w_out=w_out,
               ffn_w_gate=ffn_w_gate, ffn_w_up=ffn_w_up, ffn_conv=ffn_conv, ffn_w_down=ffn_w_down)
    prm.update(w_in_bf=w_in.astype(BF16), w_out_bf=w_out.astype(BF16), w_down_bf=ffn_w_down.astype(BF16))
    xf = x.reshape(batch * seq, d)
    for l in range(DEPTH):
        xf = _layer(xf, l, batch, seq, prm)
    return rmsnorm(xf, final_norm, x.dtype).reshape(batch, seq, d)
```

```python
import functools

import jax
import jax.numpy as jnp
from jax import lax
from jax.experimental import pallas as pl
from jax.experimental.pallas import tpu as pltpu

F32, BF16, I32 = jnp.float32, jnp.bfloat16, jnp.int32
HIGHEST = lax.Precision.HIGHEST

D_MODEL = 4096
DEPTH = 2
BRANCH = 1024
HG_HEADS, HG_D, HG_CHUNK, HG_SUB = 8, 128, 64, 16
DS_HEADS, DS_HEAD_DIM, DS_Q_LORA, DS_KV_LORA = 8, 128, 768, 512
DS_IDX_HEADS, DS_IDX_DIM, DS_TOPK_MAX = 16, 64, 256
SOFTMAX_ROWS = 64
MB_HEADS, MB_P, MB_N, MB_GROUPS, MB_CONV, MB_CHUNK = 16, 64, 128, 2, 4, 128
MB_CONV_DIM = BRANCH + 2 * MB_GROUPS * MB_N
FX_HEADS, FX_D = 8, 128
D_FF = 11008
FFN_CONV = 3
EPS = 1e-6
LANES = 128
NEG_BIG = -1e30
LOG2E = 1.4426950408889634
FOX_SKIP_MARGIN = 160.0
FOX_NORM_SLACK = 1.02
INT_MIN = -(2 ** 31)

MISC_CKV = 0
MISC_IDX = 512
MISC_DT = 640
MISC_FF = 768
MISC_W = 896


def _cp(sem, vmem_mb=48):
    return pltpu.CompilerParams(dimension_semantics=sem, vmem_limit_bytes=vmem_mb * 2 ** 20)


def _dot(a, b, precision=None):
    return jnp.dot(a, b, preferred_element_type=F32, precision=precision)


def _dot_nt(a, b):
    return lax.dot_general(a, b, (((1,), (1,)), ((), ())), preferred_element_type=F32)


def _dot_tn(a, b):
    return lax.dot_general(a, b, (((0,), (0,)), ((), ())), preferred_element_type=F32)


def _tril(n):
    r = lax.broadcasted_iota(I32, (n, n), 0)
    c = lax.broadcasted_iota(I32, (n, n), 1)
    return (r >= c).astype(F32)


def _silu(x):
    return x * jax.nn.sigmoid(x)


def _rmsnorm_body(x_ref, g_ref, o_ref):
    x = x_ref[...].astype(F32)
    ms = jnp.mean(x * x, axis=-1, keepdims=True)
    o_ref[...] = (x * lax.rsqrt(ms + EPS) * g_ref[...]).astype(o_ref.dtype)


def rmsnorm(x, gain, out_dtype, width=None, col_block=0, tm=256):
    m = x.shape[0]
    width = x.shape[1] if width is None else width
    return pl.pallas_call(
        _rmsnorm_body,
        out_shape=jax.ShapeDtypeStruct((m, width), out_dtype),
        grid=(m // tm,),
        in_specs=[pl.BlockSpec((tm, width), lambda i: (i, col_block)),
                  pl.BlockSpec((1, width), lambda i: (0, 0))],
        out_specs=pl.BlockSpec((tm, width), lambda i: (i, 0)),
        compiler_params=_cp(("parallel",)),
        name="rmsnorm",
    )(x, gain.reshape(1, width).astype(F32))


def _mm_body(a_ref, b_ref, o_ref):
    o_ref[...] = _dot(a_ref[...], b_ref[...].astype(BF16)).astype(o_ref.dtype)


def _mm_add_body(a_ref, b_ref, r_ref, o_ref):
    o_ref[...] = (r_ref[...] + _dot(a_ref[...], b_ref[...].astype(BF16))).astype(o_ref.dtype)


def _mm_scale_body(a_ref, b_ref, s_ref, o_ref):
    o_ref[...] = (_dot(a_ref[...], b_ref[...].astype(BF16)) * s_ref[...]).astype(o_ref.dtype)


def matmul(a, b, out_dtype, tm, tn, residual=None, col_scale=None, n=None, b_lead=(), vmem_mb=48):
    m, k = a.shape
    n = b.shape[-1] if n is None else n
    tm, tn = min(tm, m), min(tn, n)
    in_specs = [pl.BlockSpec((tm, k), lambda i, j: (i, 0)),
                pl.BlockSpec((None,) * len(b_lead) + (k, tn), lambda i, j: tuple(b_lead) + (0, j))]
    args, body = (a, b), _mm_body
    if residual is not None:
        in_specs.append(pl.BlockSpec((tm, tn), lambda i, j: (i, j)))
        args, body = (a, b, residual), _mm_add_body
    elif col_scale is not None:
        in_specs.append(pl.BlockSpec((1, tn), lambda i, j: (0, j)))
        args, body = (a, b, col_scale), _mm_scale_body
    return pl.pallas_call(
        body,
        out_shape=jax.ShapeDtypeStruct((m, n), out_dtype),
        grid=(m // tm, n // tn),
        in_specs=in_specs,
        out_specs=pl.BlockSpec((tm, tn), lambda i, j: (i, j)),
        compiler_params=_cp(("parallel", "arbitrary"), vmem_mb),
        name="matmul",
    )(*args)


def _hgrn_body(lbl_ref, gain_ref, q_ref, f_ref, i_ref, g_ref, o_ref, st_ref, *, layer, nchunks, hpb):
    c = pl.program_id(2)

    @pl.when(c == 0)
    def _():
        st_ref[...] = jnp.zeros_like(st_ref)

    logits = lbl_ref[...]
    e = jnp.exp(logits - jnp.max(logits, axis=0, keepdims=True))
    p = e / jnp.sum(e, axis=0, keepdims=True)
    lb_all = jnp.sum(p[0:layer + 1], axis=0, keepdims=True) - p[0:1]
    gain_all = gain_ref[...]
    tril = _tril(HG_CHUNK)
    row = lax.broadcasted_iota(I32, (HG_CHUNK, 1), 0)
    row_in_sub = row % HG_SUB
    nsub = HG_CHUNK // HG_SUB

    def chunk(ci, carry):
        sl = pl.ds(pl.multiple_of(ci * HG_CHUNK, HG_CHUNK), HG_CHUNK)
        for hh in range(hpb):
            hs = slice(hh * HG_D, (hh + 1) * HG_D)
            lb, gain = lb_all[:, hs], gain_all[:, hs]
            q = _silu(q_ref[sl, hs])
            v = _silu(i_ref[sl, hs])
            f = lb + (1.0 - lb) * jax.nn.sigmoid(f_ref[sl, hs])
            k = 1.0 - f
            b = _dot(tril, jnp.log(f) * LOG2E, HIGHEST)
            st = st_ref[hh]

            o = _dot_nt((q * jnp.exp2(b)).astype(BF16), st.astype(BF16))

            intra = jnp.zeros((HG_CHUNK, HG_D), F32)
            for d in range(HG_SUB):
                ks = k if d == 0 else pltpu.roll(k, d, 0)
                bs = b if d == 0 else pltpu.roll(b, d, 0)
                vs = v if d == 0 else pltpu.roll(v, d, 0)
                expo = jnp.where(row_in_sub >= d, b - bs, -jnp.inf)
                w = jnp.sum(q * ks * jnp.exp2(expo), axis=-1, keepdims=True)
                intra = intra + w * vs
            o = o + intra

            parts = [jnp.zeros((HG_SUB, HG_D), F32)]
            for si in range(1, nsub):
                lo = si * HG_SUB
                r = b[lo - 1:lo, :]
                qi = (q[lo:lo + HG_SUB] * jnp.exp2(b[lo:lo + HG_SUB] - r)).astype(BF16)
                kj = (k[0:lo] * jnp.exp2(r - b[0:lo])).astype(BF16)
                sc = _dot_nt(qi, kj)
                parts.append(_dot(sc.astype(BF16), v[0:lo].astype(BF16)))
            o = o + jnp.concatenate(parts, axis=0)

            b_last = b[HG_CHUNK - 1:HG_CHUNK, :]
            kd = (k * jnp.exp2(b_last - b)).astype(BF16)
            st_ref[hh] = st * jnp.exp2(b_last) + _dot_tn(v.astype(BF16), kd)

            og = o * jax.nn.sigmoid(g_ref[sl, hs])
            ms = jnp.mean(og * og, axis=-1, keepdims=True)
            o_ref[sl, hs] = (og * lax.rsqrt(ms + EPS) * gain).astype(o_ref.dtype)
        return carry

    lax.fori_loop(0, nchunks, chunk, 0, unroll=True)


def hgrn2(p_hg, lb_logits, norm_gain, layer, batch, seq, tt=256, hpb=8):
    tt = min(tt, seq)
    nt = seq // tt
    hb = HG_HEADS // hpb
    w = hpb * HG_D

    def col(sec):
        return lambda b, h, c: (b * nt + c, sec * hb + h)

    return pl.pallas_call(
        functools.partial(_hgrn_body, layer=layer, nchunks=tt // HG_CHUNK, hpb=hpb),
        out_shape=jax.ShapeDtypeStruct((batch * seq, BRANCH), BF16),
        grid=(batch, hb, nt),
        in_specs=[pl.BlockSpec((DEPTH, w), lambda b, h, c: (0, h)),
                  pl.BlockSpec((1, w), lambda b, h, c: (0, h)),
                  pl.BlockSpec((tt, w), col(0)),
                  pl.BlockSpec((tt, w), col(1)),
                  pl.BlockSpec((tt, w), col(2)),
                  pl.BlockSpec((tt, w), col(3))],
        out_specs=pl.BlockSpec((tt, w), lambda b, h, c: (b * nt + c, h)),
        scratch_shapes=[pltpu.VMEM((hpb, HG_D, HG_D), F32)],
        compiler_params=_cp(("parallel", "parallel", "arbitrary")),
        name="hgrn2",
    )(lb_logits.astype(F32), norm_gain.reshape(1, BRANCH).astype(F32), p_hg, p_hg, p_hg, p_hg)


def _mamba_body(z_ref, xbc_ref, dt_ref, cw_ref, cb_ref, dtb_ref, alog_ref, dsk_ref, gain_ref,
                o_ref, prev_ref, st_ref):
    c = pl.program_id(1)
    L = MB_CHUNK

    @pl.when(c == 0)
    def _():
        prev_ref[...] = jnp.zeros_like(prev_ref)
        st_ref[...] = jnp.zeros_like(st_ref)

    x = xbc_ref[...]
    prev = prev_ref[...]
    row = lax.broadcasted_iota(I32, (L, 1), 0)
    cw = cw_ref[...]
    acc = x * cw[MB_CONV - 1:MB_CONV, :] + cb_ref[...]
    for j in range(1, MB_CONV):
        sh = jnp.where(row >= j, pltpu.roll(x, j, 0), pltpu.roll(prev, j, 0))
        acc = acc + sh * cw[MB_CONV - 1 - j:MB_CONV - j, :]
    prev_ref[...] = x
    xbc = _silu(acc)
    xs = xbc[:, 0:BRANCH]
    bm = xbc[:, BRANCH:BRANCH + MB_GROUPS * MB_N]
    cm = xbc[:, BRANCH + MB_GROUPS * MB_N:MB_CONV_DIM]

    raw = dt_ref[...] + dtb_ref[...]
    dt = jnp.maximum(raw, 0.0) + jnp.log1p(jnp.exp(-jnp.abs(raw)))
    a = -jnp.exp(alog_ref[...]) * dt
    tril = _tril(L)
    a_cs = _dot(tril, a, HIGHEST)
    a_cs_t = a_cs.T
    causal = tril > 0.5
    dsk = dsk_ref[...]

    hpg = MB_HEADS // MB_GROUPS
    ys = []
    for g in range(MB_GROUPS):
        bg = bm[:, g * MB_N:(g + 1) * MB_N]
        cg = cm[:, g * MB_N:(g + 1) * MB_N]
        cb = _dot_nt(cg.astype(BF16), bg.astype(BF16))
        for hh in range(hpg):
            h = g * hpg + hh
            acol = a_cs[:, h:h + 1]
            arow = a_cs_t[h:h + 1, :]
            lmat = jnp.exp(jnp.where(causal, acol - arow, -jnp.inf))
            xh = xs[:, h * MB_P:(h + 1) * MB_P]
            xdt = (xh * dt[:, h:h + 1]).astype(BF16)
            y = _dot((cb * lmat).astype(BF16), xdt)
            st = st_ref[h]
            y = y + _dot(cg.astype(BF16), st.astype(BF16)) * jnp.exp(acol)
            a_last = a_cs[L - 1:L, h:h + 1]
            bdec = (bg * jnp.exp(a_last - acol)).astype(BF16)
            st_ref[h] = st * jnp.exp(a_last) + _dot_tn(bdec, xdt)
            ys.append(y + dsk[:, h:h + 1] * xh)
    y = jnp.concatenate(ys, axis=1)
    y = y * _silu(z_ref[...])
    gw = BRANCH // MB_GROUPS
    outs = []
    for g in range(MB_GROUPS):
        yg = y[:, g * gw:(g + 1) * gw]
        ms = jnp.mean(yg * yg, axis=-1, keepdims=True)
        outs.append(yg * lax.rsqrt(ms + EPS))
    o_ref[...] = (jnp.concatenate(outs, axis=1) * gain_ref[...]).astype(o_ref.dtype)


def _pad_lanes(v, n=LANES):
    v = v.reshape(1, -1).astype(F32)
    return jnp.pad(v, ((0, 0), (0, n - v.shape[1])))


def mamba2(p_z, p_xbc, p_misc, conv_w, conv_b, dt_bias, a_log, d_skip, norm_gain, batch, seq):
    nc = seq // MB_CHUNK
    L = MB_CHUNK
    full = lambda shape: pl.BlockSpec(shape, lambda b, c: (0, 0))
    return pl.pallas_call(
        _mamba_body,
        out_shape=jax.ShapeDtypeStruct((batch * seq, BRANCH), BF16),
        grid=(batch, nc),
        in_specs=[pl.BlockSpec((L, BRANCH), lambda b, c: (b * nc + c, 0)),
                  pl.BlockSpec((L, MB_CONV_DIM), lambda b, c: (b * nc + c, 0)),
                  pl.BlockSpec((L, LANES), lambda b, c: (b * nc + c, MISC_DT // LANES)),
                  full((MB_CONV, MB_CONV_DIM)), full((1, MB_CONV_DIM)),
                  full((1, LANES)), full((1, LANES)), full((1, LANES)), full((1, BRANCH))],
        out_specs=pl.BlockSpec((L, BRANCH), lambda b, c: (b * nc + c, 0)),
        scratch_shapes=[pltpu.VMEM((L, MB_CONV_DIM), F32),
                        pltpu.VMEM((MB_HEADS, MB_N, MB_P), F32)],
        compiler_params=_cp(("parallel", "arbitrary")),
        name="mamba2",
    )(p_z, p_xbc, p_misc, conv_w.astype(F32), conv_b.reshape(1, -1).astype(F32),
      _pad_lanes(dt_bias), _pad_lanes(a_log), _pad_lanes(d_skip), norm_gain.reshape(1, BRANCH).astype(F32))


def _fox_cum_body(f_ref, bias_ref, qb_ref, kb_ref, cum_ref, carry_ref, *, tt):
    c = pl.program_id(1)

    @pl.when(c == 0)
    def _():
        carry_ref[...] = jnp.zeros_like(carry_ref)

    logf = jax.nn.log_sigmoid(f_ref[...] + bias_ref[...])
    cum = _dot(_tril(tt), logf, HIGHEST) + carry_ref[...]
    carry_ref[...] = cum[tt - 1:tt, :]

    c2 = cum * LOG2E
    cum_ref[...] = c2
    lane = lax.broadcasted_iota(I32, (1, LANES), 1)
    ones = jnp.where(lane < 6, 1.0, 0.0)
    for h in range(FX_HEADS):
        col = c2[:, h:h + 1]
        hi = col.astype(BF16).astype(F32)
        r1 = col - hi
        mid = r1.astype(BF16).astype(F32)
        lo = r1 - mid
        qb = jnp.where(lane == 0, hi, jnp.where(lane == 1, mid, jnp.where(lane == 2, lo, ones)))
        kb = jnp.where(lane == 3, -hi, jnp.where(lane == 4, -mid, jnp.where(lane == 5, -lo, ones)))
        qb_ref[:, h * LANES:(h + 1) * LANES] = qb.astype(BF16)
        kb_ref[:, h * LANES:(h + 1) * LANES] = kb.astype(BF16)


def fox_bias_columns(p_misc, f_bias, batch, seq, tt=512):
    tt = min(tt, seq)
    nt = seq // tt
    m = batch * seq
    out = jax.ShapeDtypeStruct((m, FX_HEADS * LANES), BF16)
    ospec = pl.BlockSpec((tt, FX_HEADS * LANES), lambda b, c: (b * nt + c, 0))
    return pl.pallas_call(
        functools.partial(_fox_cum_body, tt=tt),
        out_shape=(out, out, jax.ShapeDtypeStruct((m, LANES), F32)),
        grid=(batch, nt),
        in_specs=[pl.BlockSpec((tt, LANES), lambda b, c: (b * nt + c, MISC_FF // LANES)),
                  pl.BlockSpec((1, LANES), lambda b, c: (0, 0))],
        out_specs=(ospec, ospec, pl.BlockSpec((tt, LANES), lambda b, c: (b * nt + c, 0))),
        scratch_shapes=[pltpu.VMEM((1, LANES), F32)],
        compiler_params=_cp(("parallel", "arbitrary")),
        name="fox_bias_columns",
    )(p_misc, _pad_lanes(f_bias))


def _fox_norm_body(q_ref, k_ref, o_ref):
    c = lax.broadcasted_iota(I32, (BRANCH, LANES), 0) // FX_D
    l = lax.broadcasted_iota(I32, (BRANCH, LANES), 1)
    q = q_ref[...].astype(F32)
    k = k_ref[...].astype(F32)
    o_ref[...] = (_dot((q * q).astype(BF16), jnp.where(c == l, 1.0, 0.0).astype(BF16))
                  + _dot((k * k).astype(BF16), jnp.where(c + FX_HEADS == l, 1.0, 0.0).astype(BF16)))


def fox_row_norms(p_fx, tm=512):
    m = p_fx.shape[0]
    tm = min(tm, m)
    return pl.pallas_call(
        _fox_norm_body,
        out_shape=jax.ShapeDtypeStruct((m, LANES), F32),
        grid=(m // tm,),
        in_specs=[pl.BlockSpec((tm, BRANCH), lambda i: (i, 0)), pl.BlockSpec((tm, BRANCH), lambda i: (i, 1))],
        out_specs=pl.BlockSpec((tm, LANES), lambda i: (i, 0)),
        compiler_params=_cp(("parallel",)),
        name="fox_row_norms",
    )(p_fx, p_fx)


def _fox_bounds_body(nsq_ref, cum_ref, o_ref, *, tq, tk):
    seq = nsq_ref.shape[0]
    nq, nk = seq // tq, seq // tk
    nsq = nsq_ref[...]
    qn = jnp.sqrt(jnp.max(nsq.reshape(nq, tq, LANES), axis=1))
    kn = jnp.sqrt(jnp.max(nsq, axis=0, keepdims=True))
    kn = pltpu.roll(kn, LANES - FX_HEADS, 1)
    cum = cum_ref[...]
    cq_first = cum.reshape(nq, tq, LANES)[:, 0, :]
    ck_last = cum.reshape(nk, tk, LANES)[:, tk - 1, :]
    thr = cq_first + 2.0 * FOX_NORM_SLACK * qn * kn + FOX_SKIP_MARGIN
    start = jnp.zeros((nq, LANES), I32)
    for j in range(nk):
        start = start + jnp.where(ck_last[j:j + 1, :] > thr, 1, 0)
    o_ref[0] = start


def fox_first_blocks(nsq, cum, batch, seq, tq, tk):
    return pl.pallas_call(
        functools.partial(_fox_bounds_body, tq=tq, tk=tk),
        out_shape=jax.ShapeDtypeStruct((batch, seq // tq, LANES), I32),
        grid=(batch,),
        in_specs=[pl.BlockSpec((seq, LANES), lambda b: (b, 0)), pl.BlockSpec((seq, LANES), lambda b: (b, 0))],
        out_specs=pl.BlockSpec((1, seq // tq, LANES), lambda b: (b, 0, 0)),
        compiler_params=_cp(("parallel",)),
        name="fox_first_blocks",
    )(nsq, cum)


def _fox_body(first_ref, q_ref, qb_ref, k_ref, kb_ref, v_ref, o_ref, m_ref, l_ref, acc_ref, *, tq, tk, nq):
    b = pl.program_id(0)
    i = pl.program_id(1)
    j_diag = (i * tq + tq - 1) // tk
    m_ref[...] = jnp.full_like(m_ref, NEG_BIG)
    l_ref[...] = jnp.zeros_like(l_ref)
    acc_ref[...] = jnp.zeros_like(acc_ref)

    def head_block(h, j, masked):
        hs = slice(h * FX_D, (h + 1) * FX_D)
        ksl = pl.ds(pl.multiple_of(j * tk, tk), tk)
        qa = jnp.concatenate([q_ref[:, hs], qb_ref[:, hs]], axis=1)
        ka = jnp.concatenate([k_ref[ksl, hs], kb_ref[ksl, hs]], axis=1)
        s = _dot_nt(qa, ka)
        if masked:
            t_glob = i * tq + lax.broadcasted_iota(I32, (tq, 1), 0)
            s_glob = j * tk + lax.broadcasted_iota(I32, (1, tk), 1)
            s = jnp.where(s_glob <= t_glob, s, -jnp.inf)
        m_prev = m_ref[h][:, 0:1]
        m_new = jnp.maximum(m_prev, jnp.max(s, axis=1, keepdims=True))
        alpha = jnp.exp2(m_prev - m_new)
        p = jnp.exp2(s - m_new)
        l_new = alpha * l_ref[h][:, 0:1] + jnp.sum(p, axis=1, keepdims=True)
        acc_ref[h] = alpha * acc_ref[h] + _dot(p.astype(BF16), v_ref[ksl, hs])
        m_ref[h] = jnp.broadcast_to(m_new, (tq, LANES))
        l_ref[h] = jnp.broadcast_to(l_new, (tq, LANES))

    base = (b * nq + i) * FX_HEADS
    for h0 in range(0, FX_HEADS, 2):
        j0 = jnp.minimum(first_ref[base + h0], first_ref[base + h0 + 1])

        def body(j, carry, h0=h0):
            head_block(h0, j, False)
            head_block(h0 + 1, j, False)
            return carry

        lax.fori_loop(j0, j_diag, body, 0)
        head_block(h0, j_diag, True)
        head_block(h0 + 1, j_diag, True)

    for h in range(FX_HEADS):
        o_ref[:, h * FX_D:(h + 1) * FX_D] = (acc_ref[h] / l_ref[h][:, 0:1]).astype(o_ref.dtype)


def fox_attention(p_fx, qb, kb, first_blocks, batch, seq, tq=256, tk=512):
    tq, tk = min(tq, seq), min(tk, seq)
    assert tk % tq == 0
    nq = seq // tq
    once = pl.Buffered(1)
    qspec = lambda colblk: pl.BlockSpec((tq, BRANCH), lambda b, i, f: (b * nq + i, colblk))
    kspec = lambda colblk: pl.BlockSpec((seq, BRANCH), lambda b, i, f: (b, colblk), pipeline_mode=once)
    return pl.pallas_call(
        functools.partial(_fox_body, tq=tq, tk=tk, nq=nq),
        out_shape=jax.ShapeDtypeStruct((batch * seq, BRANCH), BF16),
        grid_spec=pltpu.PrefetchScalarGridSpec(
            num_scalar_prefetch=1,
            grid=(batch, nq),
            in_specs=[qspec(0), qspec(0), kspec(1), kspec(0), kspec(2)],
            out_specs=pl.BlockSpec((tq, BRANCH), lambda b, i, f: (b * nq + i, 0)),
            scratch_shapes=[pltpu.VMEM((FX_HEADS, tq, LANES), F32),
                            pltpu.VMEM((FX_HEADS, tq, LANES), F32),
                            pltpu.VMEM((FX_HEADS, tq, FX_D), F32)]),
        compiler_params=_cp(("parallel", "arbitrary")),
        name="fox_attention",
    )(first_blocks, p_fx, qb, p_fx, kb, p_fx)


def fox_mixer(p_fx, p_misc, f_bias, batch, seq, tq=256, tk=512):
    tq, tk = min(tq, seq), min(tk, seq)
    qb, kb, cum = fox_bias_columns(p_misc, f_bias, batch, seq)
    first = fox_first_blocks(fox_row_norms(p_fx), cum, batch, seq, tq, tk)
    first = first[:, :, :FX_HEADS].reshape(-1)
    return fox_attention(p_fx, qb, kb, first, batch, seq, tq, tk)


def _qlat_body(q_ref, wuk_ref, o_ref):
    r = _dot_nt(q_ref[...], wuk_ref[...]) * (DS_HEAD_DIM ** -0.5 * LOG2E)
    o_ref[...] = r.astype(o_ref.dtype).reshape(o_ref.shape)


def dsa_qlat(qcat, w_uk, tq, tm=1024):
    m = qcat.shape[0]
    tm = min(tm, m)
    return pl.pallas_call(
        _qlat_body,
        out_shape=jax.ShapeDtypeStruct((m // tq, DS_HEADS, tq, DS_KV_LORA), BF16),
        grid=(m // tm, DS_HEADS),
        in_specs=[pl.BlockSpec((tm, DS_HEAD_DIM), lambda i, h: (i, h)),
                  pl.BlockSpec((DS_KV_LORA, DS_HEAD_DIM), lambda i, h: (0, h))],
        out_specs=pl.BlockSpec((tm // tq, 1, tq, DS_KV_LORA), lambda i, h: (i, h, 0, 0)),
        compiler_params=_cp(("parallel", "arbitrary")),
        name="dsa_qlat",
    )(qcat, w_uk)


def _dsa_body(qi_ref, misc_ref, ql_ref, kidx_ref, ckv_ref, wuv_ref, o_ref,
              keys_ref, wb_ref, thr_ref, s_ref, p_ref, m_ref, l_ref, alpha_ref, acc_ref, *, tq, tk, topk):
    i = pl.program_id(1)
    nh = DS_HEADS
    rows = nh * tq
    nkb = (i * tq + tq - 1) // tk + 1
    t_glob = i * tq + lax.broadcasted_iota(I32, (tq, 1), 0)
    col = lax.broadcasted_iota(I32, (1, tk), 1)
    w = misc_ref[:, DS_IDX_DIM:DS_IDX_DIM + DS_IDX_HEADS] * (DS_IDX_HEADS ** -0.5 * DS_IDX_DIM ** -0.5)
    for h in range(DS_IDX_HEADS):
        wb_ref[h] = jnp.broadcast_to(w[:, h:h + 1], (tq, LANES))
    qh = [qi_ref[:, h * DS_IDX_DIM:(h + 1) * DS_IDX_DIM] for h in range(DS_IDX_HEADS)]
    nlt = tk // LANES

    def score_body(j, carry):
        kb = kidx_ref[pl.ds(pl.multiple_of(j * tk, tk), tk), :]
        tiles = [jnp.zeros((tq, LANES), F32) for _ in range(nlt)]
        for h in range(DS_IDX_HEADS):
            lg = jnp.maximum(_dot_nt(qh[h], kb), 0.0)
            wbh = wb_ref[h]
            tiles = [t + lg[:, cc * LANES:(cc + 1) * LANES] * wbh for cc, t in enumerate(tiles)]
        sc = jnp.concatenate(tiles, axis=1)
        sc = jnp.where(j * tk + col <= t_glob, sc, -jnp.inf)
        bits = pltpu.bitcast(sc, I32)
        keys_ref[j] = jnp.where(bits < 0, bits ^ 0x7FFFFFFF, bits)
        return carry

    lax.fori_loop(0, nkb, score_body, 0)

    def count_ge(cand):
        def body(j, acc):
            ge = jnp.where(keys_ref[j] >= cand, 1.0, 0.0)
            part = ge[:, 0:LANES]
            for cc in range(1, tk // LANES):
                part = part + ge[:, cc * LANES:(cc + 1) * LANES]
            return acc + part
        acc = lax.fori_loop(0, nkb, body, jnp.zeros((tq, LANES), F32))
        return jnp.sum(acc, axis=1, keepdims=True)

    kf = float(topk)
    c0 = count_ge(jnp.zeros((tq, 1), I32))
    nonneg = c0 >= kf
    thr = jnp.where(nonneg, 0, INT_MIN).astype(I32)
    cnt = jnp.where(nonneg, c0, (nkb * tk).astype(F32))

    def search_bit(bi, thr, cnt):
        cand = thr | lax.shift_left(jnp.int32(1), bi)
        c = count_ge(cand)
        take = c >= kf
        return jnp.where(take, cand, thr), jnp.where(take, c, cnt)

    thr, cnt = search_bit(jnp.int32(30), thr, cnt)
    bits_per_round = 3

    def search_cond(carry):
        bi, _, cnt = carry
        return (bi >= 0) & (jnp.max(cnt) > kf)

    def search_round(carry):
        bi, thr, cnt = carry
        for r in range(bits_per_round):
            thr, cnt = search_bit(bi - r, thr, cnt)
        return bi - bits_per_round, thr, cnt

    _, thr, _ = lax.while_loop(search_cond, search_round, (jnp.int32(29), thr, cnt))

    thr_ref[...] = thr
    m_ref[...] = jnp.full_like(m_ref, NEG_BIG)
    l_ref[...] = jnp.zeros_like(l_ref)
    acc_ref[...] = jnp.zeros_like(acc_ref)
    rc = SOFTMAX_ROWS
    chunks_per_head = tq // rc

    def attn_body(j, carry):
        kv = ckv_ref[pl.ds(pl.multiple_of(j * tk, tk), tk), :]
        s_ref[...] = _dot_nt(ql_ref[0].reshape(rows, DS_KV_LORA), kv)

        for c in range(rows // rc):
            tok0 = (c // nh) * rc
            r0 = (c % nh) * tq + tok0
            rsl, tsl = slice(r0, r0 + rc), slice(tok0, tok0 + rc)
            if c % nh == 0:
                tg = i * tq + tok0 + lax.broadcasted_iota(I32, (rc, 1), 0)
                sel = (keys_ref[j, tsl, :] >= thr_ref[tsl, :]) & (j * tk + col <= tg)
            s = jnp.where(sel, s_ref[rsl, :], -jnp.inf)
            m_prev = m_ref[rsl, :]
            m_new = jnp.maximum(m_prev, jnp.max(s, axis=1, keepdims=True))
            alpha = jnp.exp2(m_prev - m_new)
            p = jnp.exp2(s - m_new)
            l_ref[rsl, :] = alpha * l_ref[rsl, :] + jnp.sum(p, axis=1, keepdims=True)
            m_ref[rsl, :] = m_new
            alpha_ref[rsl, :] = jnp.broadcast_to(alpha, (rc, LANES))
            p_ref[rsl, :] = p.astype(BF16)

        pv = _dot(p_ref[...], kv)
        alpha_b = alpha_ref[...]
        for cc in range(DS_KV_LORA // LANES):
            lsl = slice(cc * LANES, (cc + 1) * LANES)
            acc_ref[:, lsl] = alpha_b * acc_ref[:, lsl] + pv[:, lsl]
        return carry

    lax.fori_loop(0, nkb, attn_body, 0)

    for h in range(nh):
        hsl = slice(h * tq, (h + 1) * tq)
        o_lat = (acc_ref[hsl, :] / l_ref[hsl, :]).astype(BF16)
        o_ref[:, h * DS_HEAD_DIM:(h + 1) * DS_HEAD_DIM] = _dot(
            o_lat, wuv_ref[:, h * DS_HEAD_DIM:(h + 1) * DS_HEAD_DIM]).astype(o_ref.dtype)


def dsa_attention(qcat, p_misc, q_lat, k_idx, c_kv, w_uv, batch, seq, tq=256, tk=512):
    tq, tk = min(tq, seq), min(tk, seq)
    nq = seq // tq
    rows = DS_HEADS * tq
    topk = min(DS_TOPK_MAX, seq // 4)
    assert tk >= topk and tq % SOFTMAX_ROWS == 0
    return pl.pallas_call(
        functools.partial(_dsa_body, tq=tq, tk=tk, topk=topk),
        out_shape=jax.ShapeDtypeStruct((batch * seq, BRANCH), BF16),
        grid=(batch, nq),
        in_specs=[pl.BlockSpec((tq, DS_IDX_HEADS * DS_IDX_DIM), lambda b, i: (b * nq + i, 1)),
                  pl.BlockSpec((tq, LANES), lambda b, i: (b * nq + i, MISC_IDX // LANES)),
                  pl.BlockSpec((1, DS_HEADS, tq, DS_KV_LORA), lambda b, i: (b * nq + i, 0, 0, 0)),
                  pl.BlockSpec((seq, DS_IDX_DIM), lambda b, i: (b, 0)),
                  pl.BlockSpec((seq, DS_KV_LORA), lambda b, i: (b, 0)),
                  pl.BlockSpec((DS_KV_LORA, BRANCH), lambda b, i: (0, 0))],
        out_specs=pl.BlockSpec((tq, BRANCH), lambda b, i: (b * nq + i, 0)),
        scratch_shapes=[pltpu.VMEM((seq // tk, tq, tk), I32),
                        pltpu.VMEM((DS_IDX_HEADS, tq, LANES), F32),
                        pltpu.VMEM((tq, 1), I32),
                        pltpu.VMEM((rows, tk), F32),
                        pltpu.VMEM((rows, tk), BF16),
                        pltpu.VMEM((rows, 1), F32),
                        pltpu.VMEM((rows, 1), F32),
                        pltpu.VMEM((rows, LANES), F32),
                        pltpu.VMEM((rows, DS_KV_LORA), F32)],
        compiler_params=_cp(("parallel", "arbitrary")),
        name="dsa_attention",
    )(qcat, p_misc, q_lat, k_idx, c_kv, w_uv)


def _merge_body(h_ref, wg_ref, wb_ref, ya_ref, yb_ref, yc_ref, yd_ref, o_ref, acc_ref):
    n = pl.program_id(2)
    gate = jax.nn.sigmoid(_dot(h_ref[...], wg_ref[0].astype(BF16)))
    wb = wb_ref[0].astype(BF16)
    for idx, y_ref in enumerate((ya_ref, yb_ref, yc_ref, yd_ref)):
        @pl.when(n == idx)
        def _(y_ref=y_ref, idx=idx):
            contrib = gate * _dot(y_ref[...], wb)
            if idx == 0:
                acc_ref[...] = contrib
            else:
                acc_ref[...] += contrib

    @pl.when(n == pl.num_programs(2) - 1)
    def _():
        o_ref[...] = acc_ref[...].astype(o_ref.dtype)


def gated_merge(h, w_gate, w_branch, layer, ys, tm=1024, tn=512):
    m = h.shape[0]
    tm = min(tm, m)
    ymap = lambda i, j, n: (i, 0)
    once = pl.Buffered(1)
    return pl.pallas_call(
        _merge_body,
        out_shape=jax.ShapeDtypeStruct((m, D_MODEL), BF16),
        grid=(m // tm, D_MODEL // tn, 4),
        in_specs=[pl.BlockSpec((tm, D_MODEL), lambda i, j, n: (i, 0), pipeline_mode=once),
                  pl.BlockSpec((None, 1, D_MODEL, tn), lambda i, j, n: (layer, n, 0, j)),
                  pl.BlockSpec((None, 1, BRANCH, tn), lambda i, j, n: (layer, n, 0, j))]
                 + [pl.BlockSpec((tm, BRANCH), ymap, pipeline_mode=once) for _ in range(4)],
        out_specs=pl.BlockSpec((tm, tn), lambda i, j, n: (i, j)),
        scratch_shapes=[pltpu.VMEM((tm, tn), F32)],
        compiler_params=_cp(("parallel", "parallel", "arbitrary"), 56),
        name="gated_merge",
    )(h, w_gate, w_branch, *ys)


def _ffn_up_body(h_ref, halo_ref, wg_ref, wu_ref, cw_ref, o_ref, *, tiles_per_seq):
    i = pl.program_id(0)
    wg = wg_ref[...].astype(BF16)
    g = _dot(h_ref[...], wg)
    u = _dot(h_ref[...], wu_ref[...].astype(BF16))
    gh = _dot(halo_ref[...], wg)
    gh = gh * jnp.where(i % tiles_per_seq == 0, 0.0, 1.0)
    cw = cw_ref[...]
    row8 = lax.broadcasted_iota(I32, (8, 1), 0)
    y = g * cw[FFN_CONV - 1:FFN_CONV, :]
    for j in range(1, FFN_CONV):
        rolled = pltpu.roll(g, j, 0)
        head = jnp.where(row8 < j, pltpu.roll(gh, j, 0), rolled[0:8])
        shifted = jnp.concatenate([head, rolled[8:]], axis=0)
        y = y + shifted * cw[FFN_CONV - 1 - j:FFN_CONV - j, :]
    o_ref[...] = (_silu(y) * u).astype(o_ref.dtype)


def ffn_up(h, w_gate, w_up, conv_w, layer, seq, tm=1024, tn=256):
    m = h.shape[0]
    tm = min(tm, seq)
    dff = w_gate.shape[-1]
    wspec = pl.BlockSpec((None, D_MODEL, tn), lambda i, j: (layer, 0, j))
    return pl.pallas_call(
        functools.partial(_ffn_up_body, tiles_per_seq=seq // tm),
        out_shape=jax.ShapeDtypeStruct((m, dff), BF16),
        grid=(m // tm, dff // tn),
        in_specs=[pl.BlockSpec((tm, D_MODEL), lambda i, j: (i, 0)),
                  pl.BlockSpec((8, D_MODEL), lambda i, j: (jnp.maximum(i * (tm // 8) - 1, 0), 0)),
                  wspec, wspec,
                  pl.BlockSpec((None, FFN_CONV, tn), lambda i, j: (layer, 0, j))],
        out_specs=pl.BlockSpec((tm, tn), lambda i, j: (i, j)),
        compiler_params=_cp(("parallel", "arbitrary")),
        name="ffn_up",
    )(h, h, w_gate, w_up, conv_w.astype(F32))


def _in_proj_weights(w, l):
    o = 4 * BRANCH
    secs = {}
    for name, width in (("cq", 768), ("ckv", 512), ("kidx", 64), ("widx", 16), ("z", 1024),
                        ("xbc", MB_CONV_DIM), ("dt", 16), ("fx", 3072), ("ff", 8)):
        secs[name] = w[l, :, o:o + width]
        o += width
    zeros = lambda n: jnp.zeros((w.shape[1], n), w.dtype)
    misc = jnp.concatenate([secs["ckv"], secs["kidx"], secs["widx"], zeros(LANES - 80),
                            secs["dt"], zeros(LANES - 16), secs["ff"], zeros(LANES - 8)], axis=1)
    return dict(cq=secs["cq"], misc=misc, z=secs["z"], xbc=secs["xbc"], fx=secs["fx"])


def _layer(x, l, batch, seq, prm):
    h = rmsnorm(x, prm["attn_norm"][l], BF16)
    w = _in_proj_weights(prm["w_in_bf"], l)
    p_hg = matmul(h, prm["w_in_bf"], F32, 1024, 1024, n=4 * BRANCH, b_lead=(l,))
    p_cq = matmul(h, w["cq"], F32, 1024, 768)
    p_misc = matmul(h, w["misc"], F32, 1024, MISC_W)
    p_z = matmul(h, w["z"], F32, 1024, 1024)
    p_xbc = matmul(h, w["xbc"], F32, 1024, 768)
    fx_scale = jnp.concatenate([jnp.full((1, BRANCH), FX_D ** -0.5 * LOG2E, F32), jnp.ones((1, 2 * BRANCH), F32)], 1)
    p_fx = matmul(h, w["fx"], BF16, 1024, 1024, col_scale=fx_scale)

    y_a = hgrn2(p_hg, prm["hgrn_lb_logits"], prm["hgrn_norm"][l], l, batch, seq)

    c_q = rmsnorm(p_cq, prm["dsa_q_norm"][l], BF16)
    c_kv = rmsnorm(p_misc, prm["dsa_kv_norm"][l], BF16, width=DS_KV_LORA, col_block=0)
    k_idx = p_misc[:, MISC_IDX:MISC_IDX + DS_IDX_DIM].astype(BF16)
    w_q = jnp.concatenate([prm["dsa_w_uq"][l].astype(BF16), prm["dsa_w_iq"][l].astype(BF16)], axis=1)
    qcat = matmul(c_q, w_q, BF16, 1024, 1024)
    dsa_tq = min(256, seq)
    q_lat = dsa_qlat(qcat, prm["dsa_w_uk"][l].astype(BF16), dsa_tq)
    y_b = dsa_attention(qcat, p_misc, q_lat, k_idx, c_kv, prm["dsa_w_uv"][l].astype(BF16), batch, seq,
                        tq=dsa_tq)

    y_c = mamba2(p_z, p_xbc, p_misc, prm["ssm_conv_w"][l], prm["ssm_conv_b"][l], prm["ssm_dt_bias"][l],
                 prm["ssm_a_log"][l], prm["ssm_d"][l], prm["ssm_norm"][l], batch, seq)

    y_d = fox_mixer(p_fx, p_misc, prm["fox_f_bias"][l], batch, seq)

    merged = gated_merge(h, prm["w_gate"], prm["w_branch"], l, (y_a, y_b, y_c, y_d))
    x = matmul(merged, prm["w_out_bf"], F32, 1024, 1024, residual=x, b_lead=(l,), vmem_mb=56)

    h2 = rmsnorm(x, prm["ffn_norm"][l], BF16)
    act = ffn_up(h2, prm["ffn_w_gate"], prm["ffn_w_up"], prm["ffn_conv"], l, seq)
    return matmul(act, prm["w_down_bf"], F32, 512, 512, residual=x, b_lead=(l,), vmem_mb=56)


def kernel(x, attn_norm, ffn_norm, final_norm, w_in, hgrn_lb_logits, hgrn_norm, dsa_q_norm, dsa_kv_norm,
           dsa_w_uq, dsa_w_iq, dsa_w_uk, dsa_w_uv, ssm_conv_w, ssm_conv_b, ssm_dt_bias, ssm_a_log, ssm_d,
           ssm_norm, fox_f_bias, w_gate, w_branch, w_out, ffn_w_gate, ffn_w_up, ffn_conv, ffn_w_down):
    batch, seq, d = x.shape
    prm = dict(attn_norm=attn_norm, ffn_norm=ffn_norm, w_in=w_in, hgrn_lb_logits=hgrn_lb_logits,
               hgrn_norm=hgrn_norm, dsa_q_norm=dsa_q_norm, dsa_kv_norm=dsa_kv_norm, dsa_w_uq=dsa_w_uq,
               dsa_w_iq=dsa_w_iq, dsa_w_uk=dsa_w_uk, dsa_w_uv=dsa_w_uv, ssm_conv_w=ssm_conv_w,
               ssm_conv_b=ssm_conv_b, ssm_dt_bias=ssm_dt_bias, ssm_a_log=ssm_a_log, ssm_d=ssm_d,
               ssm_norm=ssm_norm, fox_f_bias=fox_f_bias, w_gate=w_gate, w_branch=w_branch, w_out=w_out,
               ffn_w_gate=ffn_w_gate, ffn_w_up=ffn_w_up, ffn_conv=ffn_conv, ffn_w_down=ffn_w_down)
    prm.update(w_in_bf=w_in.astype(BF16), w_out_bf=w_out.astype(BF16), w_down_bf=ffn_w_down.astype(BF16))
    xf = x.reshape(batch * seq, d)
    for l in range(DEPTH):
        xf = _layer(xf, l, batch, seq, prm)
    return rmsnorm(xf, final_norm, x.dtype).reshape(batch, seq, d)
```

```python
import functools

import jax
import jax.numpy as jnp
from jax import lax
from jax.experimental import pallas as pl
from jax.experimental.pallas import tpu as pltpu

F32, BF16, I32 = jnp.float32, jnp.bfloat16, jnp.int32
HIGHEST = lax.Precision.HIGHEST

D_MODEL = 4096
DEPTH = 2
BRANCH = 1024
HG_HEADS, HG_D, HG_CHUNK, HG_SUB = 8, 128, 64, 16
DS_HEADS, DS_HEAD_DIM, DS_Q_LORA, DS_KV_LORA = 8, 128, 768, 512
DS_IDX_HEADS, DS_IDX_DIM, DS_TOPK_MAX = 16, 64, 256
SOFTMAX_ROWS = 64
MB_HEADS, MB_P, MB_N, MB_GROUPS, MB_CONV, MB_CHUNK = 16, 64, 128, 2, 4, 128
MB_CONV_DIM = BRANCH + 2 * MB_GROUPS * MB_N
FX_HEADS, FX_D = 8, 128
D_FF = 11008
FFN_CONV = 3
EPS = 1e-6
LANES = 128
NEG_BIG = -1e30
LOG2E = 1.4426950408889634
FOX_SKIP_MARGIN = 160.0
FOX_NORM_SLACK = 1.02
INT_MIN = -(2 ** 31)

MISC_CKV = 0
MISC_IDX = 512
MISC_DT = 640
MISC_FF = 768
MISC_W = 896


def _cp(sem, vmem_mb=48):
    return pltpu.CompilerParams(dimension_semantics=sem, vmem_limit_bytes=vmem_mb * 2 ** 20)


def _dot(a, b, precision=None):
    return jnp.dot(a, b, preferred_element_type=F32, precision=precision)


def _dot_nt(a, b):
    return lax.dot_general(a, b, (((1,), (1,)), ((), ())), preferred_element_type=F32)


def _dot_tn(a, b):
    return lax.dot_general(a, b, (((0,), (0,)), ((), ())), preferred_element_type=F32)


def _tril(n):
    r = lax.broadcasted_iota(I32, (n, n), 0)
    c = lax.broadcasted_iota(I32, (n, n), 1)
    return (r >= c).astype(F32)


def _silu(x):
    return x * jax.nn.sigmoid(x)


def _rmsnorm_body(x_ref, g_ref, o_ref):
    x = x_ref[...].astype(F32)
    ms = jnp.mean(x * x, axis=-1, keepdims=True)
    o_ref[...] = (x * lax.rsqrt(ms + EPS) * g_ref[...]).astype(o_ref.dtype)


def rmsnorm(x, gain, out_dtype, width=None, col_block=0, tm=256):
    m = x.shape[0]
    width = x.shape[1] if width is None else width
    return pl.pallas_call(
        _rmsnorm_body,
        out_shape=jax.ShapeDtypeStruct((m, width), out_dtype),
        grid=(m // tm,),
        in_specs=[pl.BlockSpec((tm, width), lambda i: (i, col_block)),
                  pl.BlockSpec((1, width), lambda i: (0, 0))],
        out_specs=pl.BlockSpec((tm, width), lambda i: (i, 0)),
        compiler_params=_cp(("parallel",)),
        name="rmsnorm",
    )(x, gain.reshape(1, width).astype(F32))


def _mm_body(a_ref, b_ref, o_ref):
    o_ref[...] = _dot(a_ref[...], b_ref[...].astype(BF16)).astype(o_ref.dtype)


def _mm_add_body(a_ref, b_ref, r_ref, o_ref):
    o_ref[...] = (r_ref[...] + _dot(a_ref[...], b_ref[...].astype(BF16))).astype(o_ref.dtype)


def _mm_scale_body(a_ref, b_ref, s_ref, o_ref):
    o_ref[...] = (_dot(a_ref[...], b_ref[...].astype(BF16)) * s_ref[...]).astype(o_ref.dtype)


def matmul(a, b, out_dtype, tm, tn, residual=None, col_scale=None, n=None, b_lead=(), vmem_mb=48):
    m, k = a.shape
    n = b.shape[-1] if n is None else n
    tm, tn = min(tm, m), min(tn, n)
    in_specs = [pl.BlockSpec((tm, k), lambda i, j: (i, 0)),
                pl.BlockSpec((None,) * len(b_lead) + (k, tn), lambda i, j: tuple(b_lead) + (0, j))]
    args, body = (a, b), _mm_body
    if residual is not None:
        in_specs.append(pl.BlockSpec((tm, tn), lambda i, j: (i, j)))
        args, body = (a, b, residual), _mm_add_body
    elif col_scale is not None:
        in_specs.append(pl.BlockSpec((1, tn), lambda i, j: (0, j)))
        args, body = (a, b, col_scale), _mm_scale_body
    return pl.pallas_call(
        body,
        out_shape=jax.ShapeDtypeStruct((m, n), out_dtype),
        grid=(m // tm, n // tn),
        in_specs=in_specs,
        out_specs=pl.BlockSpec((tm, tn), lambda i, j: (i, j)),
        compiler_params=_cp(("parallel", "arbitrary"), vmem_mb),
        name="matmul",
    )(*args)


def _hgrn_body(lbl_ref, gain_ref, q_ref, f_ref, i_ref, g_ref, o_ref, st_ref, *, layer, nchunks, hpb):
    c = pl.program_id(2)

    @pl.when(c == 0)
    def _():
        st_ref[...] = jnp.zeros_like(st_ref)

    logits = lbl_ref[...]
    e = jnp.exp(logits - jnp.max(logits, axis=0, keepdims=True))
    p = e / jnp.sum(e, axis=0, keepdims=True)
    lb_all = jnp.sum(p[0:layer + 1], axis=0, keepdims=True) - p[0:1]
    gain_all = gain_ref[...]
    tril = _tril(HG_CHUNK)
    row = lax.broadcasted_iota(I32, (HG_CHUNK, 1), 0)
    row_in_sub = row % HG_SUB
    nsub = HG_CHUNK // HG_SUB

    def chunk(ci, carry):
        sl = pl.ds(pl.multiple_of(ci * HG_CHUNK, HG_CHUNK), HG_CHUNK)
        for hh in range(hpb):
            hs = slice(hh * HG_D, (hh + 1) * HG_D)
            lb, gain = lb_all[:, hs], gain_all[:, hs]
            q = _silu(q_ref[sl, hs])
            v = _silu(i_ref[sl, hs])
            f = lb + (1.0 - lb) * jax.nn.sigmoid(f_ref[sl, hs])
            k = 1.0 - f
            b = _dot(tril, jnp.log(f) * LOG2E, HIGHEST)
            st = st_ref[hh]

            o = _dot_nt((q * jnp.exp2(b)).astype(BF16), st.astype(BF16))

            intra = jnp.zeros((HG_CHUNK, HG_D), F32)
            for d in range(HG_SUB):
                ks = k if d == 0 else pltpu.roll(k, d, 0)
                bs = b if d == 0 else pltpu.roll(b, d, 0)
                vs = v if d == 0 else pltpu.roll(v, d, 0)
                expo = jnp.where(row_in_sub >= d, b - bs, -jnp.inf)
                w = jnp.sum(q * ks * jnp.exp2(expo), axis=-1, keepdims=True)
                intra = intra + w * vs
            o = o + intra

            parts = [jnp.zeros((HG_SUB, HG_D), F32)]
            for si in range(1, nsub):
                lo = si * HG_SUB
                r = b[lo - 1:lo, :]
                qi = (q[lo:lo + HG_SUB] * jnp.exp2(b[lo:lo + HG_SUB] - r)).astype(BF16)
                kj = (k[0:lo] * jnp.exp2(r - b[0:lo])).astype(BF16)
                sc = _dot_nt(qi, kj)
                parts.append(_dot(sc.astype(BF16), v[0:lo].astype(BF16)))
            o = o + jnp.concatenate(parts, axis=0)

            b_last = b[HG_CHUNK - 1:HG_CHUNK, :]
            kd = (k * jnp.exp2(b_last - b)).astype(BF16)
            st_ref[hh] = st * jnp.exp2(b_last) + _dot_tn(v.astype(BF16), kd)

            og = o * jax.nn.sigmoid(g_ref[sl, hs])
            ms = jnp.mean(og * og, axis=-1, keepdims=True)
            o_ref[sl, hs] = (og * lax.rsqrt(ms + EPS) * gain).astype(o_ref.dtype)
        return carry

    lax.fori_loop(0, nchunks, chunk, 0, unroll=True)


def hgrn2(p_hg, lb_logits, norm_gain, layer, batch, seq, tt=256, hpb=8):
    tt = min(tt, seq)
    nt = seq // tt
    hb = HG_HEADS // hpb
    w = hpb * HG_D

    def col(sec):
        return lambda b, h, c: (b * nt + c, sec * hb + h)

    return pl.pallas_call(
        functools.partial(_hgrn_body, layer=layer, nchunks=tt // HG_CHUNK, hpb=hpb),
        out_shape=jax.ShapeDtypeStruct((batch * seq, BRANCH), BF16),
        grid=(batch, hb, nt),
        in_specs=[pl.BlockSpec((DEPTH, w), lambda b, h, c: (0, h)),
                  pl.BlockSpec((1, w), lambda b, h, c: (0, h)),
                  pl.BlockSpec((tt, w), col(0)),
                  pl.BlockSpec((tt, w), col(1)),
                  pl.BlockSpec((tt, w), col(2)),
                  pl.BlockSpec((tt, w), col(3))],
        out_specs=pl.BlockSpec((tt, w), lambda b, h, c: (b * nt + c, h)),
        scratch_shapes=[pltpu.VMEM((hpb, HG_D, HG_D), F32)],
        compiler_params=_cp(("parallel", "parallel", "arbitrary")),
        name="hgrn2",
    )(lb_logits.astype(F32), norm_gain.reshape(1, BRANCH).astype(F32), p_hg, p_hg, p_hg, p_hg)


def _mamba_body(z_ref, xbc_ref, dt_ref, cw_ref, cb_ref, dtb_ref, alog_ref, dsk_ref, gain_ref,
                o_ref, prev_ref, st_ref):
    c = pl.program_id(1)
    L = MB_CHUNK

    @pl.when(c == 0)
    def _():
        prev_ref[...] = jnp.zeros_like(prev_ref)
        st_ref[...] = jnp.zeros_like(st_ref)

    x = xbc_ref[...]
    prev = prev_ref[...]
    row = lax.broadcasted_iota(I32, (L, 1), 0)
    cw = cw_ref[...]
    acc = x * cw[MB_CONV - 1:MB_CONV, :] + cb_ref[...]
    for j in range(1, MB_CONV):
        sh = jnp.where(row >= j, pltpu.roll(x, j, 0), pltpu.roll(prev, j, 0))
        acc = acc + sh * cw[MB_CONV - 1 - j:MB_CONV - j, :]
    prev_ref[...] = x
    xbc = _silu(acc)
    xs = xbc[:, 0:BRANCH]
    bm = xbc[:, BRANCH:BRANCH + MB_GROUPS * MB_N]
    cm = xbc[:, BRANCH + MB_GROUPS * MB_N:MB_CONV_DIM]

    raw = dt_ref[...] + dtb_ref[...]
    dt = jnp.maximum(raw, 0.0) + jnp.log1p(jnp.exp(-jnp.abs(raw)))
    a = -jnp.exp(alog_ref[...]) * dt
    tril = _tril(L)
    a_cs = _dot(tril, a, HIGHEST)
    a_cs_t = a_cs.T
    causal = tril > 0.5
    dsk = dsk_ref[...]

    hpg = MB_HEADS // MB_GROUPS
    ys = []
    for g in range(MB_GROUPS):
        bg = bm[:, g * MB_N:(g + 1) * MB_N]
        cg = cm[:, g * MB_N:(g + 1) * MB_N]
        cb = _dot_nt(cg.astype(BF16), bg.astype(BF16))
        for hh in range(hpg):
            h = g * hpg + hh
            acol = a_cs[:, h:h + 1]
            arow = a_cs_t[h:h + 1, :]
            lmat = jnp.exp(jnp.where(causal, acol - arow, -jnp.inf))
            xh = xs[:, h * MB_P:(h + 1) * MB_P]
            xdt = (xh * dt[:, h:h + 1]).astype(BF16)
            y = _dot((cb * lmat).astype(BF16), xdt)
            st = st_ref[h]
            y = y + _dot(cg.astype(BF16), st.astype(BF16)) * jnp.exp(acol)
            a_last = a_cs[L - 1:L, h:h + 1]
            bdec = (bg * jnp.exp(a_last - acol)).astype(BF16)
            st_ref[h] = st * jnp.exp(a_last) + _dot_tn(bdec, xdt)
            ys.append(y + dsk[:, h:h + 1] * xh)
    y = jnp.concatenate(ys, axis=1)
    y = y * _silu(z_ref[...])
    gw = BRANCH // MB_GROUPS
    outs = []
    for g in range(MB_GROUPS):
        yg = y[:, g * gw:(g + 1) * gw]
        ms = jnp.mean(yg * yg, axis=-1, keepdims=True)
        outs.append(yg * lax.rsqrt(ms + EPS))
    o_ref[...] = (jnp.concatenate(outs, axis=1) * gain_ref[...]).astype(o_ref.dtype)


def _pad_lanes(v, n=LANES):
    v = v.reshape(1, -1).astype(F32)
    return jnp.pad(v, ((0, 0), (0, n - v.shape[1])))


def mamba2(p_z, p_xbc, p_misc, conv_w, conv_b, dt_bias, a_log, d_skip, norm_gain, batch, seq):
    nc = seq // MB_CHUNK
    L = MB_CHUNK
    full = lambda shape: pl.BlockSpec(shape, lambda b, c: (0, 0))
    return pl.pallas_call(
        _mamba_body,
        out_shape=jax.ShapeDtypeStruct((batch * seq, BRANCH), BF16),
        grid=(batch, nc),
        in_specs=[pl.BlockSpec((L, BRANCH), lambda b, c: (b * nc + c, 0)),
                  pl.BlockSpec((L, MB_CONV_DIM), lambda b, c: (b * nc + c, 0)),
                  pl.BlockSpec((L, LANES), lambda b, c: (b * nc + c, MISC_DT // LANES)),
                  full((MB_CONV, MB_CONV_DIM)), full((1, MB_CONV_DIM)),
                  full((1, LANES)), full((1, LANES)), full((1, LANES)), full((1, BRANCH))],
        out_specs=pl.BlockSpec((L, BRANCH), lambda b, c: (b * nc + c, 0)),
        scratch_shapes=[pltpu.VMEM((L, MB_CONV_DIM), F32),
                        pltpu.VMEM((MB_HEADS, MB_N, MB_P), F32)],
        compiler_params=_cp(("parallel", "arbitrary")),
        name="mamba2",
    )(p_z, p_xbc, p_misc, conv_w.astype(F32), conv_b.reshape(1, -1).astype(F32),
      _pad_lanes(dt_bias), _pad_lanes(a_log), _pad_lanes(d_skip), norm_gain.reshape(1, BRANCH).astype(F32))


def _fox_cum_body(f_ref, bias_ref, qb_ref, kb_ref, cum_ref, carry_ref, *, tt):
    c = pl.program_id(1)

    @pl.when(c == 0)
    def _():
        carry_ref[...] = jnp.zeros_like(carry_ref)

    logf = jax.nn.log_sigmoid(f_ref[...] + bias_ref[...])
    cum = _dot(_tril(tt), logf, HIGHEST) + carry_ref[...]
    carry_ref[...] = cum[tt - 1:tt, :]

    c2 = cum * LOG2E
    cum_ref[...] = c2
    lane = lax.broadcasted_iota(I32, (1, LANES), 1)
    ones = jnp.where(lane < 6, 1.0, 0.0)
    for h in range(FX_HEADS):
        col = c2[:, h:h + 1]
        hi = col.astype(BF16).astype(F32)
        r1 = col - hi
        mid = r1.astype(BF16).astype(F32)
        lo = r1 - mid
        qb = jnp.where(lane == 0, hi, jnp.where(lane == 1, mid, jnp.where(lane == 2, lo, ones)))
        kb = jnp.where(lane == 3, -hi, jnp.where(lane == 4, -mid, jnp.where(lane == 5, -lo, ones)))
        qb_ref[:, h * LANES:(h + 1) * LANES] = qb.astype(BF16)
        kb_ref[:, h * LANES:(h + 1) * LANES] = kb.astype(BF16)


def fox_bias_columns(p_misc, f_bias, batch, seq, tt=512):
    tt = min(tt, seq)
    nt = seq // tt
    m = batch * seq
    out = jax.ShapeDtypeStruct((m, FX_HEADS * LANES), BF16)
    ospec = pl.BlockSpec((tt, FX_HEADS * LANES), lambda b, c: (b * nt + c, 0))
    return pl.pallas_call(
        functools.partial(_fox_cum_body, tt=tt),
        out_shape=(out, out, jax.ShapeDtypeStruct((m, LANES), F32)),
        grid=(batch, nt),
        in_specs=[pl.BlockSpec((tt, LANES), lambda b, c: (b * nt + c, MISC_FF // LANES)),
                  pl.BlockSpec((1, LANES), lambda b, c: (0, 0))],
        out_specs=(ospec, ospec, pl.BlockSpec((tt, LANES), lambda b, c: (b * nt + c, 0))),
        scratch_shapes=[pltpu.VMEM((1, LANES), F32)],
        compiler_params=_cp(("parallel", "arbitrary")),
        name="fox_bias_columns",
    )(p_misc, _pad_lanes(f_bias))


def _fox_norm_body(q_ref, k_ref, o_ref):
    c = lax.broadcasted_iota(I32, (BRANCH, LANES), 0) // FX_D
    l = lax.broadcasted_iota(I32, (BRANCH, LANES), 1)
    q = q_ref[...].astype(F32)
    k = k_ref[...].astype(F32)
    o_ref[...] = (_dot((q * q).astype(BF16), jnp.where(c == l, 1.0, 0.0).astype(BF16))
                  + _dot((k * k).astype(BF16), jnp.where(c + FX_HEADS == l, 1.0, 0.0).astype(BF16)))


def fox_row_norms(p_fx, tm=512):
    m = p_fx.shape[0]
    tm = min(tm, m)
    return pl.pallas_call(
        _fox_norm_body,
        out_shape=jax.ShapeDtypeStruct((m, LANES), F32),
        grid=(m // tm,),
        in_specs=[pl.BlockSpec((tm, BRANCH), lambda i: (i, 0)), pl.BlockSpec((tm, BRANCH), lambda i: (i, 1))],
        out_specs=pl.BlockSpec((tm, LANES), lambda i: (i, 0)),
        compiler_params=_cp(("parallel",)),
        name="fox_row_norms",
    )(p_fx, p_fx)


def _fox_bounds_body(nsq_ref, cum_ref, o_ref, *, tq, tk):
    seq = nsq_ref.shape[0]
    nq, nk = seq // tq, seq // tk
    nsq = nsq_ref[...]
    qn = jnp.sqrt(jnp.max(nsq.reshape(nq, tq, LANES), axis=1))
    kn = jnp.sqrt(jnp.max(nsq, axis=0, keepdims=True))
    kn = pltpu.roll(kn, LANES - FX_HEADS, 1)
    cum = cum_ref[...]
    cq_first = cum.reshape(nq, tq, LANES)[:, 0, :]
    ck_last = cum.reshape(nk, tk, LANES)[:, tk - 1, :]
    thr = cq_first + 2.0 * FOX_NORM_SLACK * qn * kn + FOX_SKIP_MARGIN
    start = jnp.zeros((nq, LANES), I32)
    for j in range(nk):
        start = start + jnp.where(ck_last[j:j + 1, :] > thr, 1, 0)
    o_ref[0] = start


def fox_first_blocks(nsq, cum, batch, seq, tq, tk):
    return pl.pallas_call(
        functools.partial(_fox_bounds_body, tq=tq, tk=tk),
        out_shape=jax.ShapeDtypeStruct((batch, seq // tq, LANES), I32),
        grid=(batch,),
        in_specs=[pl.BlockSpec((seq, LANES), lambda b: (b, 0)), pl.BlockSpec((seq, LANES), lambda b: (b, 0))],
        out_specs=pl.BlockSpec((1, seq // tq, LANES), lambda b: (b, 0, 0)),
        compiler_params=_cp(("parallel",)),
        name="fox_first_blocks",
    )(nsq, cum)


def _fox_body(first_ref, q_ref, qb_ref, k_ref, kb_ref, v_ref, o_ref, m_ref, l_ref, acc_ref, *, tq, tk, nq):
    b = pl.program_id(0)
    i = pl.program_id(1)
    j_diag = (i * tq + tq - 1) // tk
    m_ref[...] = jnp.full_like(m_ref, NEG_BIG)
    l_ref[...] = jnp.zeros_like(l_ref)
    acc_ref[...] = jnp.zeros_like(acc_ref)

    def head_block(h, j, masked):
        hs = slice(h * FX_D, (h + 1) * FX_D)
        ksl = pl.ds(pl.multiple_of(j * tk, tk), tk)
        qa = jnp.concatenate([q_ref[:, hs], qb_ref[:, hs]], axis=1)
        ka = jnp.concatenate([k_ref[ksl, hs], kb_ref[ksl, hs]], axis=1)
        s = _dot_nt(qa, ka)
        if masked:
            t_glob = i * tq + lax.broadcasted_iota(I32, (tq, 1), 0)
            s_glob = j * tk + lax.broadcasted_iota(I32, (1, tk), 1)
            s = jnp.where(s_glob <= t_glob, s, -jnp.inf)
        m_prev = m_ref[h][:, 0:1]
        m_new = jnp.maximum(m_prev, jnp.max(s, axis=1, keepdims=True))
        alpha = jnp.exp2(m_prev - m_new)
        p = jnp.exp2(s - m_new)
        l_new = alpha * l_ref[h][:, 0:1] + jnp.sum(p, axis=1, keepdims=True)
        acc_ref[h] = alpha * acc_ref[h] + _dot(p.astype(BF16), v_ref[ksl, hs])
        m_ref[h] = jnp.broadcast_to(m_new, (tq, LANES))
        l_ref[h] = jnp.broadcast_to(l_new, (tq, LANES))

    base = (b * nq + i) * FX_HEADS
    for h0 in range(0, FX_HEADS, 2):
        j0 = jnp.minimum(first_ref[base + h0], first_ref[base + h0 + 1])

        def body(j, carry, h0=h0):
            head_block(h0, j, False)
            head_block(h0 + 1, j, False)
            return carry

        lax.fori_loop(j0, j_diag, body, 0)
        head_block(h0, j_diag, True)
        head_block(h0 + 1, j_diag, True)

    for h in range(FX_HEADS):
        o_ref[:, h * FX_D:(h + 1) * FX_D] = (acc_ref[h] / l_ref[h][:, 0:1]).astype(o_ref.dtype)


def fox_attention(p_fx, qb, kb, first_blocks, batch, seq, tq=256, tk=512):
    tq, tk = min(tq, seq), min(tk, seq)
    assert tk % tq == 0
    nq = seq // tq
    once = pl.Buffered(1)
    qspec = lambda colblk: pl.BlockSpec((tq, BRANCH), lambda b, i, f: (b * nq + i, colblk))
    kspec = lambda colblk: pl.BlockSpec((seq, BRANCH), lambda b, i, f: (b, colblk), pipeline_mode=once)
    return pl.pallas_call(
        functools.partial(_fox_body, tq=tq, tk=tk, nq=nq),
        out_shape=jax.ShapeDtypeStruct((batch * seq, BRANCH), BF16),
        grid_spec=pltpu.PrefetchScalarGridSpec(
            num_scalar_prefetch=1,
            grid=(batch, nq),
            in_specs=[qspec(0), qspec(0), kspec(1), kspec(0), kspec(2)],
            out_specs=pl.BlockSpec((tq, BRANCH), lambda b, i, f: (b * nq + i, 0)),
            scratch_shapes=[pltpu.VMEM((FX_HEADS, tq, LANES), F32),
                            pltpu.VMEM((FX_HEADS, tq, LANES), F32),
                            pltpu.VMEM((FX_HEADS, tq, FX_D), F32)]),
        compiler_params=_cp(("parallel", "arbitrary")),
        name="fox_attention",
    )(first_blocks, p_fx, qb, p_fx, kb, p_fx)


def fox_mixer(p_fx, p_misc, f_bias, batch, seq, tq=256, tk=512):
    tq, tk = min(tq, seq), min(tk, seq)
    qb, kb, cum = fox_bias_columns(p_misc, f_bias, batch, seq)
    first = fox_first_blocks(fox_row_norms(p_fx), cum, batch, seq, tq, tk)
    first = first[:, :, :FX_HEADS].reshape(-1)
    return fox_attention(p_fx, qb, kb, first, batch, seq, tq, tk)


def _qlat_body(q_ref, wuk_ref, o_ref):
    r = _dot_nt(q_ref[...], wuk_ref[...]) * (DS_HEAD_DIM ** -0.5 * LOG2E)
    o_ref[...] = r.astype(o_ref.dtype).reshape(o_ref.shape)


def dsa_qlat(qcat, w_uk, tq, tm=1024):
    m = qcat.shape[0]
    tm = min(tm, m)
    return pl.pallas_call(
        _qlat_body,
        out_shape=jax.ShapeDtypeStruct((m // tq, DS_HEADS, tq, DS_KV_LORA), BF16),
        grid=(m // tm, DS_HEADS),
        in_specs=[pl.BlockSpec((tm, DS_HEAD_DIM), lambda i, h: (i, h)),
                  pl.BlockSpec((DS_KV_LORA, DS_HEAD_DIM), lambda i, h: (0, h))],
        out_specs=pl.BlockSpec((tm // tq, 1, tq, DS_KV_LORA), lambda i, h: (i, h, 0, 0)),
        compiler_params=_cp(("parallel", "arbitrary")),
        name="dsa_qlat",
    )(qcat, w_uk)


def _dsa_body(qi_ref, misc_ref, ql_ref, kidx_ref, ckv_ref, wuv_ref, o_ref,
              keys_ref, wb_ref, thr_ref, s_ref, p_ref, m_ref, l_ref, alpha_ref, acc_ref, *, tq, tk, topk):
    seq_len = kidx_ref.shape[0]
    i = pl.program_id(1)
    nh = DS_HEADS
    rows = nh * tq
    nkb = (i * tq + tq - 1) // tk + 1
    t_glob = i * tq + lax.broadcasted_iota(I32, (tq, 1), 0)
    col = lax.broadcasted_iota(I32, (1, tk), 1)
    w = misc_ref[:, DS_IDX_DIM:DS_IDX_DIM + DS_IDX_HEADS] * (DS_IDX_HEADS ** -0.5 * DS_IDX_DIM ** -0.5)
    for h in range(DS_IDX_HEADS):
        wb_ref[h] = jnp.broadcast_to(w[:, h:h + 1], (tq, LANES))
    qh = [qi_ref[:, h * DS_IDX_DIM:(h + 1) * DS_IDX_DIM] for h in range(DS_IDX_HEADS)]
    nlt = tk // LANES

    def score_body(j, carry):
        kb = kidx_ref[pl.ds(pl.multiple_of(j * tk, tk), tk), :]
        tiles = [jnp.zeros((tq, LANES), F32) for _ in range(nlt)]
        for h in range(DS_IDX_HEADS):
            lg = jnp.maximum(_dot_nt(qh[h], kb), 0.0)
            wbh = wb_ref[h]
            tiles = [t + lg[:, cc * LANES:(cc + 1) * LANES] * wbh for cc, t in enumerate(tiles)]
        sc = jnp.concatenate(tiles, axis=1)
        sc = jnp.where(j * tk + col <= t_glob, sc, -jnp.inf)
        bits = pltpu.bitcast(sc, I32)
        keys_ref[j] = jnp.where(bits < 0, bits ^ 0x7FFFFFFF, bits)
        return carry

    lax.fori_loop(0, nkb, score_body, 0)

    def count_ge(cand):
        def body(j, acc):
            ge = jnp.where(keys_ref[j] >= cand, 1.0, 0.0)
            part = ge[:, 0:LANES]
            for cc in range(1, tk // LANES):
                part = part + ge[:, cc * LANES:(cc + 1) * LANES]
            return acc + part
        acc = lax.fori_loop(0, nkb, body, jnp.zeros((tq, LANES), F32))
        return jnp.sum(acc, axis=1, keepdims=True)

    def top2_body(j, carry):
        m1, m2 = carry
        for cc in range(nlt):
            x = keys_ref[j, :, cc * LANES:(cc + 1) * LANES]
            m2 = jnp.maximum(m2, jnp.minimum(m1, x))
            m1 = jnp.maximum(m1, x)
        return m1, m2

    init = jnp.full((tq, LANES), INT_MIN, I32)
    m1, m2 = lax.fori_loop(0, nkb, top2_body, (init, init))

    def key_to_score(key):
        return pltpu.bitcast(jnp.where(key < 0, key ^ 0x7FFFFFFF, key), F32)

    def score_to_key(sc):
        bits = pltpu.bitcast(sc, I32)
        return jnp.where(bits < 0, bits ^ 0x7FFFFFFF, bits)

    lo = score_to_key(jnp.min(key_to_score(m2), axis=1, keepdims=True))
    hi = score_to_key(jnp.max(key_to_score(m1), axis=1, keepdims=True))

    kf = float(topk)

    def bisect(lo, hi, cnt):
        mid = (lo | hi) - ((lo ^ hi) >> 1)
        c = count_ge(mid)
        take = c >= kf
        return jnp.where(take, mid, lo), jnp.where(take, hi, mid - 1), jnp.where(take, c, cnt)

    passes_per_round = 3

    def search_cond(carry):
        lo, hi, cnt = carry
        return jnp.max(jnp.where(lo < hi, cnt, kf)) > kf

    def search_round(carry):
        lo, hi, cnt = carry
        for _ in range(passes_per_round):
            lo, hi, cnt = bisect(lo, hi, cnt)
        return lo, hi, cnt

    unknown = jnp.full((tq, 1), 2.0 * kf + float(seq_len), F32)
    thr, _, _ = lax.while_loop(search_cond, search_round, (lo, hi, unknown))

    thr_ref[...] = thr
    m_ref[...] = jnp.full_like(m_ref, NEG_BIG)
    l_ref[...] = jnp.zeros_like(l_ref)
    acc_ref[...] = jnp.zeros_like(acc_ref)
    rc = SOFTMAX_ROWS
    chunks_per_head = tq // rc

    def attn_body(j, carry):
        kv = ckv_ref[pl.ds(pl.multiple_of(j * tk, tk), tk), :]
        s_ref[...] = _dot_nt(ql_ref[0].reshape(rows, DS_KV_LORA), kv)

        for c in range(rows // rc):
            tok0 = (c // nh) * rc
            r0 = (c % nh) * tq + tok0
            rsl, tsl = slice(r0, r0 + rc), slice(tok0, tok0 + rc)
            if c % nh == 0:
                tg = i * tq + tok0 + lax.broadcasted_iota(I32, (rc, 1), 0)
                sel = (keys_ref[j, tsl, :] >= thr_ref[tsl, :]) & (j * tk + col <= tg)
            s = jnp.where(sel, s_ref[rsl, :], -jnp.inf)
            m_prev = m_ref[rsl, :]
            m_new = jnp.maximum(m_prev, jnp.max(s, axis=1, keepdims=True))
            alpha = jnp.exp2(m_prev - m_new)
            p = jnp.exp2(s - m_new)
            l_ref[rsl, :] = alpha * l_ref[rsl, :] + jnp.sum(p, axis=1, keepdims=True)
            m_ref[rsl, :] = m_new
            alpha_ref[rsl, :] = jnp.broadcast_to(alpha, (rc, LANES))
            p_ref[rsl, :] = p.astype(BF16)

        pv = _dot(p_ref[...], kv)
        alpha_b = alpha_ref[...]
        for cc in range(DS_KV_LORA // LANES):
            lsl = slice(cc * LANES, (cc + 1) * LANES)
            acc_ref[:, lsl] = alpha_b * acc_ref[:, lsl] + pv[:, lsl]
        return carry

    lax.fori_loop(0, nkb, attn_body, 0)

    for h in range(nh):
        hsl = slice(h * tq, (h + 1) * tq)
        o_lat = (acc_ref[hsl, :] / l_ref[hsl, :]).astype(BF16)
        o_ref[:, h * DS_HEAD_DIM:(h + 1) * DS_HEAD_DIM] = _dot(
            o_lat, wuv_ref[:, h * DS_HEAD_DIM:(h + 1) * DS_HEAD_DIM]).astype(o_ref.dtype)


def dsa_attention(qcat, p_misc, q_lat, k_idx, c_kv, w_uv, batch, seq, tq=256, tk=512):
    tq, tk = min(tq, seq), min(tk, seq)
    nq = seq // tq
    rows = DS_HEADS * tq
    topk = min(DS_TOPK_MAX, seq // 4)
    assert tk >= topk and topk <= 2 * LANES and tq % SOFTMAX_ROWS == 0
    return pl.pallas_call(
        functools.partial(_dsa_body, tq=tq, tk=tk, topk=topk),
        out_shape=jax.ShapeDtypeStruct((batch * seq, BRANCH), BF16),
        grid=(batch, nq),
        in_specs=[pl.BlockSpec((tq, DS_IDX_HEADS * DS_IDX_DIM), lambda b, i: (b * nq + i, 1)),
                  pl.BlockSpec((tq, LANES), lambda b, i: (b * nq + i, MISC_IDX // LANES)),
                  pl.BlockSpec((1, DS_HEADS, tq, DS_KV_LORA), lambda b, i: (b * nq + i, 0, 0, 0)),
                  pl.BlockSpec((seq, DS_IDX_DIM), lambda b, i: (b, 0)),
                  pl.BlockSpec((seq, DS_KV_LORA), lambda b, i: (b, 0)),
                  pl.BlockSpec((DS_KV_LORA, BRANCH), lambda b, i: (0, 0))],
        out_specs=pl.BlockSpec((tq, BRANCH), lambda b, i: (b * nq + i, 0)),
        scratch_shapes=[pltpu.VMEM((seq // tk, tq, tk), I32),
                        pltpu.VMEM((DS_IDX_HEADS, tq, LANES), F32),
                        pltpu.VMEM((tq, 1), I32),
                        pltpu.VMEM((rows, tk), F32),
                        pltpu.VMEM((rows, tk), BF16),
                        pltpu.VMEM((rows, 1), F32),
                        pltpu.VMEM((rows, 1), F32),
                        pltpu.VMEM((rows, LANES), F32),
                        pltpu.VMEM((rows, DS_KV_LORA), F32)],
        compiler_params=_cp(("parallel", "arbitrary")),
        name="dsa_attention",
    )(qcat, p_misc, q_lat, k_idx, c_kv, w_uv)


def _merge_body(h_ref, wg_ref, wb_ref, ya_ref, yb_ref, yc_ref, yd_ref, o_ref, acc_ref):
    n = pl.program_id(2)
    gate = jax.nn.sigmoid(_dot(h_ref[...], wg_ref[0].astype(BF16)))
    wb = wb_ref[0].astype(BF16)
    for idx, y_ref in enumerate((ya_ref, yb_ref, yc_ref, yd_ref)):
        @pl.when(n == idx)
        def _(y_ref=y_ref, idx=idx):
            contrib = gate * _dot(y_ref[...], wb)
            if idx == 0:
                acc_ref[...] = contrib
            else:
                acc_ref[...] += contrib

    @pl.when(n == pl.num_programs(2) - 1)
    def _():
        o_ref[...] = acc_ref[...].astype(o_ref.dtype)


def gated_merge(h, w_gate, w_branch, layer, ys, tm=1024, tn=512):
    m = h.shape[0]
    tm = min(tm, m)
    ymap = lambda i, j, n: (i, 0)
    once = pl.Buffered(1)
    return pl.pallas_call(
        _merge_body,
        out_shape=jax.ShapeDtypeStruct((m, D_MODEL), BF16),
        grid=(m // tm, D_MODEL // tn, 4),
        in_specs=[pl.BlockSpec((tm, D_MODEL), lambda i, j, n: (i, 0), pipeline_mode=once),
                  pl.BlockSpec((None, 1, D_MODEL, tn), lambda i, j, n: (layer, n, 0, j)),
                  pl.BlockSpec((None, 1, BRANCH, tn), lambda i, j, n: (layer, n, 0, j))]
                 + [pl.BlockSpec((tm, BRANCH), ymap, pipeline_mode=once) for _ in range(4)],
        out_specs=pl.BlockSpec((tm, tn), lambda i, j, n: (i, j)),
        scratch_shapes=[pltpu.VMEM((tm, tn), F32)],
        compiler_params=_cp(("parallel", "parallel", "arbitrary"), 56),
        name="gated_merge",
    )(h, w_gate, w_branch, *ys)


def _ffn_up_body(h_ref, halo_ref, wg_ref, wu_ref, cw_ref, o_ref, *, tiles_per_seq):
    i = pl.program_id(0)
    wg = wg_ref[...].astype(BF16)
    g = _dot(h_ref[...], wg)
    u = _dot(h_ref[...], wu_ref[...].astype(BF16))
    gh = _dot(halo_ref[...], wg)
    gh = gh * jnp.where(i % tiles_per_seq == 0, 0.0, 1.0)
    cw = cw_ref[...]
    row8 = lax.broadcasted_iota(I32, (8, 1), 0)
    y = g * cw[FFN_CONV - 1:FFN_CONV, :]
    for j in range(1, FFN_CONV):
        rolled = pltpu.roll(g, j, 0)
        head = jnp.where(row8 < j, pltpu.roll(gh, j, 0), rolled[0:8])
        shifted = jnp.concatenate([head, rolled[8:]], axis=0)
        y = y + shifted * cw[FFN_CONV - 1 - j:FFN_CONV - j, :]
    o_ref[...] = (_silu(y) * u).astype(o_ref.dtype)


def ffn_up(h, w_gate, w_up, conv_w, layer, seq, tm=2048, tn=256):
    m = h.shape[0]
    tm = min(tm, seq)
    dff = w_gate.shape[-1]
    wspec = pl.BlockSpec((None, D_MODEL, tn), lambda i, j: (layer, 0, j))
    return pl.pallas_call(
        functools.partial(_ffn_up_body, tiles_per_seq=seq // tm),
        out_shape=jax.ShapeDtypeStruct((m, dff), BF16),
        grid=(m // tm, dff // tn),
        in_specs=[pl.BlockSpec((tm, D_MODEL), lambda i, j: (i, 0), pipeline_mode=pl.Buffered(1)),
                  pl.BlockSpec((8, D_MODEL), lambda i, j: (jnp.maximum(i * (tm // 8) - 1, 0), 0)),
                  wspec, wspec,
                  pl.BlockSpec((None, FFN_CONV, tn), lambda i, j: (layer, 0, j))],
        out_specs=pl.BlockSpec((tm, tn), lambda i, j: (i, j)),
        compiler_params=_cp(("parallel", "arbitrary")),
        name="ffn_up",
    )(h, h, w_gate, w_up, conv_w.astype(F32))


def _in_proj_weights(w, l):
    o = 4 * BRANCH
    secs = {}
    for name, width in (("cq", 768), ("ckv", 512), ("kidx", 64), ("widx", 16), ("z", 1024),
                        ("xbc", MB_CONV_DIM), ("dt", 16), ("fx", 3072), ("ff", 8)):
        secs[name] = w[l, :, o:o + width]
        o += width
    zeros = lambda n: jnp.zeros((w.shape[1], n), w.dtype)
    misc = jnp.concatenate([secs["ckv"], secs["kidx"], secs["widx"], zeros(LANES - 80),
                            secs["dt"], zeros(LANES - 16), secs["ff"], zeros(LANES - 8)], axis=1)
    return dict(cq=secs["cq"], misc=misc, z=secs["z"], xbc=secs["xbc"], fx=secs["fx"])


def _layer(x, l, batch, seq, prm):
    h = rmsnorm(x, prm["attn_norm"][l], BF16)
    w = _in_proj_weights(prm["w_in_bf"], l)
    p_hg = matmul(h, prm["w_in_bf"], F32, 1024, 1024, n=4 * BRANCH, b_lead=(l,))
    p_cq = matmul(h, w["cq"], F32, 1024, 768)
    p_misc = matmul(h, w["misc"], F32, 1024, MISC_W)
    p_z = matmul(h, w["z"], F32, 1024, 1024)
    p_xbc = matmul(h, w["xbc"], F32, 1024, 768)
    fx_scale = jnp.concatenate([jnp.full((1, BRANCH), FX_D ** -0.5 * LOG2E, F32), jnp.ones((1, 2 * BRANCH), F32)], 1)
    p_fx = matmul(h, w["fx"], BF16, 1024, 1024, col_scale=fx_scale)

    y_a = hgrn2(p_hg, prm["hgrn_lb_logits"], prm["hgrn_norm"][l], l, batch, seq)

    c_q = rmsnorm(p_cq, prm["dsa_q_norm"][l], BF16)
    c_kv = rmsnorm(p_misc, prm["dsa_kv_norm"][l], BF16, width=DS_KV_LORA, col_block=0)
    k_idx = p_misc[:, MISC_IDX:MISC_IDX + DS_IDX_DIM].astype(BF16)
    w_q = jnp.concatenate([prm["dsa_w_uq"][l].astype(BF16), prm["dsa_w_iq"][l].astype(BF16)], axis=1)
    qcat = matmul(c_q, w_q, BF16, 1024, 1024)
    dsa_tq = min(256, seq)
    q_lat = dsa_qlat(qcat, prm["dsa_w_uk"][l].astype(BF16), dsa_tq)
    y_b = dsa_attention(qcat, p_misc, q_lat, k_idx, c_kv, prm["dsa_w_uv"][l].astype(BF16), batch, seq,
                        tq=dsa_tq)

    y_c = mamba2(p_z, p_xbc, p_misc, prm["ssm_conv_w"][l], prm["ssm_conv_b"][l], prm["ssm_dt_bias"][l],
                 prm["ssm_a_log"][l], prm["ssm_d"][l], prm["ssm_norm"][l], batch, seq)

    y_d = fox_mixer(p_fx, p_misc, prm["fox_f_bias"][l], batch, seq)

    merged = gated_merge(h, prm["w_gate"], prm["w_branch"], l, (y_a, y_b, y_c, y_d))
    x = matmul(merged, prm["w_out_bf"], F32, 1024, 1024, residual=x, b_lead=(l,), vmem_mb=56)

    h2 = rmsnorm(x, prm["ffn_norm"][l], BF16)
    act = ffn_up(h2, prm["ffn_w_gate"], prm["ffn_w_up"], prm["ffn_conv"], l, seq)
    return matmul(act, prm["w_down_bf"], F32, 512, 512, residual=x, b_lead=(l,), vmem_mb=56)


def kernel(x, attn_norm, ffn_norm, final_norm, w_in, hgrn_lb_logits, hgrn_norm, dsa_q_norm, dsa_kv_norm,
           dsa_w_uq, dsa_w_iq, dsa_w_uk, dsa_w_uv, ssm_conv_w, ssm_conv_b, ssm_dt_bias, ssm_a_log, ssm_d,
           ssm_norm, fox_f_bias, w_gate, w_branch, w_out, ffn_w_gate, ffn_w_up, ffn_conv, ffn_w_down):
    batch, seq, d = x.shape
    prm = dict(attn_norm=attn_norm, ffn_norm=ffn_norm, w_in=w_in, hgrn_lb_logits=hgrn_lb_logits,
               hgrn_norm=hgrn_norm, dsa_q_norm=dsa_q_norm, dsa_kv_norm=dsa_kv_norm, dsa_w_uq=dsa_w_uq,
               dsa_w_iq=dsa_w_iq, dsa_w_uk=dsa_w_uk, dsa_w_uv=dsa_w_uv, ssm_conv_w=ssm_conv_w,
               ssm_conv_b=ssm_conv_b, ssm_dt_bias=ssm_dt_bias, ssm_a_log=ssm_a_log, ssm_d=ssm_d,
               ssm_norm=ssm_norm, fox_f_bias=fox_f_bias, w_gate=w_gate, w_branch=w_branch, w_out=w_out,
               ffn_w_gate=ffn_w_gate, ffn_w_up=ffn_w_up, ffn_conv=ffn_conv, ffn_w_down=ffn_w_down)
    prm.update(w_in_bf=w_in.astype(BF16), w_out_bf=w_out.astype(BF16), w_down_bf=ffn_w_down.astype(BF16))
    xf = x.reshape(batch * seq, d)
    for l in range(DEPTH):
        xf = _layer(xf, l, batch, seq, prm)
    return rmsnorm(xf, final_norm, x.dtype).reshape(batch, seq, d)
```

```python
import functools

import jax
import jax.numpy as jnp
from jax import lax
from jax.experimental import pallas as pl
from jax.experimental.pallas import tpu as pltpu

F32, BF16, I32 = jnp.float32, jnp.bfloat16, jnp.int32
HIGHEST = lax.Precision.HIGHEST

D_MODEL = 4096
DEPTH = 2
BRANCH = 1024
HG_HEADS, HG_D, HG_CHUNK, HG_SUB = 8, 128, 64, 16
DS_HEADS, DS_HEAD_DIM, DS_Q_LORA, DS_KV_LORA = 8, 128, 768, 512
DS_IDX_HEADS, DS_IDX_DIM, DS_TOPK_MAX = 16, 64, 256
SOFTMAX_ROWS = 64
MB_HEADS, MB_P, MB_N, MB_GROUPS, MB_CONV, MB_CHUNK = 16, 64, 128, 2, 4, 128
MB_CONV_DIM = BRANCH + 2 * MB_GROUPS * MB_N
FX_HEADS, FX_D = 8, 128
D_FF = 11008
FFN_CONV = 3
EPS = 1e-6
LANES = 128
NEG_BIG = -1e30
LOG2E = 1.4426950408889634
FOX_SKIP_MARGIN = 160.0
FOX_NORM_SLACK = 1.02
INT_MIN = -(2 ** 31)

MISC_CKV = 0
MISC_IDX = 512
MISC_DT = 640
MISC_FF = 768
MISC_W = 896


VMEM_MB, VMEM_BIG_MB = 48, 56
PROJ_ROWS, PROJ_COLS = 1024, 1024
DOWN_ROWS, DOWN_COLS = 512, 512
MERGE_ROWS, MERGE_COLS = 1024, 512
FFN_ROWS, FFN_COLS = 2048, 256
ATTN_Q_ROWS, ATTN_KEYS = 256, 512
HG_ROWS, HG_HEADS_PER_STEP = 256, 8
NORM_ROWS, SCAN_ROWS = 256, 512


def _cp(sem, vmem_mb=VMEM_MB):
    return pltpu.CompilerParams(dimension_semantics=sem, vmem_limit_bytes=vmem_mb * 2 ** 20)


def _dot(a, b, precision=None):
    return jnp.dot(a, b, preferred_element_type=F32, precision=precision)


def _dot_nt(a, b):
    return lax.dot_general(a, b, (((1,), (1,)), ((), ())), preferred_element_type=F32)


def _dot_tn(a, b):
    return lax.dot_general(a, b, (((0,), (0,)), ((), ())), preferred_element_type=F32)


def _tril(n):
    r = lax.broadcasted_iota(I32, (n, n), 0)
    c = lax.broadcasted_iota(I32, (n, n), 1)
    return (r >= c).astype(F32)


def _silu(x):
    return x * jax.nn.sigmoid(x)


def _rmsnorm_body(x_ref, g_ref, o_ref):
    x = x_ref[...].astype(F32)
    ms = jnp.mean(x * x, axis=-1, keepdims=True)
    o_ref[...] = (x * lax.rsqrt(ms + EPS) * g_ref[...]).astype(o_ref.dtype)


def rmsnorm(x, gain, out_dtype, width=None, col_block=0, tm=NORM_ROWS):
    m = x.shape[0]
    width = x.shape[1] if width is None else width
    return pl.pallas_call(
        _rmsnorm_body,
        out_shape=jax.ShapeDtypeStruct((m, width), out_dtype),
        grid=(m // tm,),
        in_specs=[pl.BlockSpec((tm, width), lambda i: (i, col_block)),
                  pl.BlockSpec((1, width), lambda i: (0, 0))],
        out_specs=pl.BlockSpec((tm, width), lambda i: (i, 0)),
        compiler_params=_cp(("parallel",)),
        name="rmsnorm",
    )(x, gain.reshape(1, width).astype(F32))


def _mm_body(a_ref, b_ref, o_ref):
    o_ref[...] = _dot(a_ref[...], b_ref[...].astype(BF16)).astype(o_ref.dtype)


def _mm_add_body(a_ref, b_ref, r_ref, o_ref):
    o_ref[...] = (r_ref[...] + _dot(a_ref[...], b_ref[...].astype(BF16))).astype(o_ref.dtype)


def _mm_scale_body(a_ref, b_ref, s_ref, o_ref):
    o_ref[...] = (_dot(a_ref[...], b_ref[...].astype(BF16)) * s_ref[...]).astype(o_ref.dtype)


def matmul(a, b, out_dtype, tm=PROJ_ROWS, tn=PROJ_COLS, residual=None, col_scale=None, n=None, b_lead=(),
           vmem_mb=VMEM_MB):
    m, k = a.shape
    n = b.shape[-1] if n is None else n
    tm, tn = min(tm, m), min(tn, n)
    in_specs = [pl.BlockSpec((tm, k), lambda i, j: (i, 0)),
                pl.BlockSpec((None,) * len(b_lead) + (k, tn), lambda i, j: tuple(b_lead) + (0, j))]
    args, body = (a, b), _mm_body
    if residual is not None:
        in_specs.append(pl.BlockSpec((tm, tn), lambda i, j: (i, j)))
        args, body = (a, b, residual), _mm_add_body
    elif col_scale is not None:
        in_specs.append(pl.BlockSpec((1, tn), lambda i, j: (0, j)))
        args, body = (a, b, col_scale), _mm_scale_body
    return pl.pallas_call(
        body,
        out_shape=jax.ShapeDtypeStruct((m, n), out_dtype),
        grid=(m // tm, n // tn),
        in_specs=in_specs,
        out_specs=pl.BlockSpec((tm, tn), lambda i, j: (i, j)),
        compiler_params=_cp(("parallel", "arbitrary"), vmem_mb),
        name="matmul",
    )(*args)


def _hgrn_body(lbl_ref, gain_ref, q_ref, f_ref, i_ref, g_ref, o_ref, st_ref, *, layer, nchunks, hpb):
    c = pl.program_id(2)

    @pl.when(c == 0)
    def _():
        st_ref[...] = jnp.zeros_like(st_ref)

    logits = lbl_ref[...]
    e = jnp.exp(logits - jnp.max(logits, axis=0, keepdims=True))
    p = e / jnp.sum(e, axis=0, keepdims=True)
    lb_all = jnp.sum(p[0:layer + 1], axis=0, keepdims=True) - p[0:1]
    gain_all = gain_ref[...]
    tril = _tril(HG_CHUNK)
    row = lax.broadcasted_iota(I32, (HG_CHUNK, 1), 0)
    row_in_sub = row % HG_SUB
    nsub = HG_CHUNK // HG_SUB

    def chunk(ci, carry):
        sl = pl.ds(pl.multiple_of(ci * HG_CHUNK, HG_CHUNK), HG_CHUNK)
        for hh in range(hpb):
            hs = slice(hh * HG_D, (hh + 1) * HG_D)
            lb, gain = lb_all[:, hs], gain_all[:, hs]
            q = _silu(q_ref[sl, hs])
            v = _silu(i_ref[sl, hs])
            f = lb + (1.0 - lb) * jax.nn.sigmoid(f_ref[sl, hs])
            k = 1.0 - f
            b = _dot(tril, jnp.log(f) * LOG2E, HIGHEST)
            st = st_ref[hh]

            o = _dot_nt((q * jnp.exp2(b)).astype(BF16), st.astype(BF16))

            intra = jnp.zeros((HG_CHUNK, HG_D), F32)
            for d in range(HG_SUB):
                ks = k if d == 0 else pltpu.roll(k, d, 0)
                bs = b if d == 0 else pltpu.roll(b, d, 0)
                vs = v if d == 0 else pltpu.roll(v, d, 0)
                expo = jnp.where(row_in_sub >= d, b - bs, -jnp.inf)
                w = jnp.sum(q * ks * jnp.exp2(expo), axis=-1, keepdims=True)
                intra = intra + w * vs
            o = o + intra

            parts = [jnp.zeros((HG_SUB, HG_D), F32)]
            for si in range(1, nsub):
                lo = si * HG_SUB
                r = b[lo - 1:lo, :]
                qi = (q[lo:lo + HG_SUB] * jnp.exp2(b[lo:lo + HG_SUB] - r)).astype(BF16)
                kj = (k[0:lo] * jnp.exp2(r - b[0:lo])).astype(BF16)
                sc = _dot_nt(qi, kj)
                parts.append(_dot(sc.astype(BF16), v[0:lo].astype(BF16)))
            o = o + jnp.concatenate(parts, axis=0)

            b_last = b[HG_CHUNK - 1:HG_CHUNK, :]
            kd = (k * jnp.exp2(b_last - b)).astype(BF16)
            st_ref[hh] = st * jnp.exp2(b_last) + _dot_tn(v.astype(BF16), kd)

            og = o * jax.nn.sigmoid(g_ref[sl, hs])
            ms = jnp.mean(og * og, axis=-1, keepdims=True)
            o_ref[sl, hs] = (og * lax.rsqrt(ms + EPS) * gain).astype(o_ref.dtype)
        return carry

    lax.fori_loop(0, nchunks, chunk, 0, unroll=True)


def hgrn2(p_hg, lb_logits, norm_gain, layer, batch, seq, tt=HG_ROWS, hpb=HG_HEADS_PER_STEP):
    tt = min(tt, seq)
    nt = seq // tt
    hb = HG_HEADS // hpb
    w = hpb * HG_D

    def col(sec):
        return lambda b, h, c: (b * nt + c, sec * hb + h)

    return pl.pallas_call(
        functools.partial(_hgrn_body, layer=layer, nchunks=tt // HG_CHUNK, hpb=hpb),
        out_shape=jax.ShapeDtypeStruct((batch * seq, BRANCH), BF16),
        grid=(batch, hb, nt),
        in_specs=[pl.BlockSpec((DEPTH, w), lambda b, h, c: (0, h)),
                  pl.BlockSpec((1, w), lambda b, h, c: (0, h)),
                  pl.BlockSpec((tt, w), col(0)),
                  pl.BlockSpec((tt, w), col(1)),
                  pl.BlockSpec((tt, w), col(2)),
                  pl.BlockSpec((tt, w), col(3))],
        out_specs=pl.BlockSpec((tt, w), lambda b, h, c: (b * nt + c, h)),
        scratch_shapes=[pltpu.VMEM((hpb, HG_D, HG_D), F32)],
        compiler_params=_cp(("parallel", "parallel", "arbitrary")),
        name="hgrn2",
    )(lb_logits.astype(F32), norm_gain.reshape(1, BRANCH).astype(F32), p_hg, p_hg, p_hg, p_hg)


def _mamba_body(z_ref, xbc_ref, dt_ref, cw_ref, cb_ref, dtb_ref, alog_ref, dsk_ref, gain_ref,
                o_ref, prev_ref, st_ref):
    c = pl.program_id(1)
    L = MB_CHUNK

    @pl.when(c == 0)
    def _():
        prev_ref[...] = jnp.zeros_like(prev_ref)
        st_ref[...] = jnp.zeros_like(st_ref)

    x = xbc_ref[...]
    prev = prev_ref[...]
    row = lax.broadcasted_iota(I32, (L, 1), 0)
    cw = cw_ref[...]
    acc = x * cw[MB_CONV - 1:MB_CONV, :] + cb_ref[...]
    for j in range(1, MB_CONV):
        sh = jnp.where(row >= j, pltpu.roll(x, j, 0), pltpu.roll(prev, j, 0))
        acc = acc + sh * cw[MB_CONV - 1 - j:MB_CONV - j, :]
    prev_ref[...] = x
    xbc = _silu(acc)
    xs = xbc[:, 0:BRANCH]
    bm = xbc[:, BRANCH:BRANCH + MB_GROUPS * MB_N]
    cm = xbc[:, BRANCH + MB_GROUPS * MB_N:MB_CONV_DIM]

    raw = dt_ref[...] + dtb_ref[...]
    dt = jnp.maximum(raw, 0.0) + jnp.log1p(jnp.exp(-jnp.abs(raw)))
    a = -jnp.exp(alog_ref[...]) * dt
    tril = _tril(L)
    a_cs = _dot(tril, a, HIGHEST)
    a_cs_t = a_cs.T
    causal = tril > 0.5
    dsk = dsk_ref[...]

    hpg = MB_HEADS // MB_GROUPS
    ys = []
    for g in range(MB_GROUPS):
        bg = bm[:, g * MB_N:(g + 1) * MB_N]
        cg = cm[:, g * MB_N:(g + 1) * MB_N]
        cb = _dot_nt(cg.astype(BF16), bg.astype(BF16))
        for hh in range(hpg):
            h = g * hpg + hh
            acol = a_cs[:, h:h + 1]
            arow = a_cs_t[h:h + 1, :]
            lmat = jnp.exp(jnp.where(causal, acol - arow, -jnp.inf))
            xh = xs[:, h * MB_P:(h + 1) * MB_P]
            xdt = (xh * dt[:, h:h + 1]).astype(BF16)
            y = _dot((cb * lmat).astype(BF16), xdt)
            st = st_ref[h]
            y = y + _dot(cg.astype(BF16), st.astype(BF16)) * jnp.exp(acol)
            a_last = a_cs[L - 1:L, h:h + 1]
            bdec = (bg * jnp.exp(a_last - acol)).astype(BF16)
            st_ref[h] = st * jnp.exp(a_last) + _dot_tn(bdec, xdt)
            ys.append(y + dsk[:, h:h + 1] * xh)
    y = jnp.concatenate(ys, axis=1)
    y = y * _silu(z_ref[...])
    gw = BRANCH // MB_GROUPS
    outs = []
    for g in range(MB_GROUPS):
        yg = y[:, g * gw:(g + 1) * gw]
        ms = jnp.mean(yg * yg, axis=-1, keepdims=True)
        outs.append(yg * lax.rsqrt(ms + EPS))
    o_ref[...] = (jnp.concatenate(outs, axis=1) * gain_ref[...]).astype(o_ref.dtype)


def _pad_lanes(v, n=LANES):
    v = v.reshape(1, -1).astype(F32)
    return jnp.pad(v, ((0, 0), (0, n - v.shape[1])))


def mamba2(p_z, p_xbc, p_misc, conv_w, conv_b, dt_bias, a_log, d_skip, norm_gain, batch, seq):
    nc = seq // MB_CHUNK
    L = MB_CHUNK
    full = lambda shape: pl.BlockSpec(shape, lambda b, c: (0, 0))
    return pl.pallas_call(
        _mamba_body,
        out_shape=jax.ShapeDtypeStruct((batch * seq, BRANCH), BF16),
        grid=(batch, nc),
        in_specs=[pl.BlockSpec((L, BRANCH), lambda b, c: (b * nc + c, 0)),
                  pl.BlockSpec((L, MB_CONV_DIM), lambda b, c: (b * nc + c, 0)),
                  pl.BlockSpec((L, LANES), lambda b, c: (b * nc + c, MISC_DT // LANES)),
                  full((MB_CONV, MB_CONV_DIM)), full((1, MB_CONV_DIM)),
                  full((1, LANES)), full((1, LANES)), full((1, LANES)), full((1, BRANCH))],
        out_specs=pl.BlockSpec((L, BRANCH), lambda b, c: (b * nc + c, 0)),
        scratch_shapes=[pltpu.VMEM((L, MB_CONV_DIM), F32),
                        pltpu.VMEM((MB_HEADS, MB_N, MB_P), F32)],
        compiler_params=_cp(("parallel", "arbitrary")),
        name="mamba2",
    )(p_z, p_xbc, p_misc, conv_w.astype(F32), conv_b.reshape(1, -1).astype(F32),
      _pad_lanes(dt_bias), _pad_lanes(a_log), _pad_lanes(d_skip), norm_gain.reshape(1, BRANCH).astype(F32))


def _fox_cum_body(f_ref, bias_ref, qb_ref, kb_ref, cum_ref, carry_ref, *, tt):
    c = pl.program_id(1)

    @pl.when(c == 0)
    def _():
        carry_ref[...] = jnp.zeros_like(carry_ref)

    logf = jax.nn.log_sigmoid(f_ref[...] + bias_ref[...])
    cum = _dot(_tril(tt), logf, HIGHEST) + carry_ref[...]
    carry_ref[...] = cum[tt - 1:tt, :]

    c2 = cum * LOG2E
    cum_ref[...] = c2
    lane = lax.broadcasted_iota(I32, (1, LANES), 1)
    ones = jnp.where(lane < 6, 1.0, 0.0)
    for h in range(FX_HEADS):
        col = c2[:, h:h + 1]
        hi = col.astype(BF16).astype(F32)
        r1 = col - hi
        mid = r1.astype(BF16).astype(F32)
        lo = r1 - mid
        qb = jnp.where(lane == 0, hi, jnp.where(lane == 1, mid, jnp.where(lane == 2, lo, ones)))
        kb = jnp.where(lane == 3, -hi, jnp.where(lane == 4, -mid, jnp.where(lane == 5, -lo, ones)))
        qb_ref[:, h * LANES:(h + 1) * LANES] = qb.astype(BF16)
        kb_ref[:, h * LANES:(h + 1) * LANES] = kb.astype(BF16)


def fox_bias_columns(p_misc, f_bias, batch, seq, tt=SCAN_ROWS):
    tt = min(tt, seq)
    nt = seq // tt
    m = batch * seq
    out = jax.ShapeDtypeStruct((m, FX_HEADS * LANES), BF16)
    ospec = pl.BlockSpec((tt, FX_HEADS * LANES), lambda b, c: (b * nt + c, 0))
    return pl.pallas_call(
        functools.partial(_fox_cum_body, tt=tt),
        out_shape=(out, out, jax.ShapeDtypeStruct((m, LANES), F32)),
        grid=(batch, nt),
        in_specs=[pl.BlockSpec((tt, LANES), lambda b, c: (b * nt + c, MISC_FF // LANES)),
                  pl.BlockSpec((1, LANES), lambda b, c: (0, 0))],
        out_specs=(ospec, ospec, pl.BlockSpec((tt, LANES), lambda b, c: (b * nt + c, 0))),
        scratch_shapes=[pltpu.VMEM((1, LANES), F32)],
        compiler_params=_cp(("parallel", "arbitrary")),
        name="fox_bias_columns",
    )(p_misc, _pad_lanes(f_bias))


def _fox_norm_body(q_ref, k_ref, o_ref):
    c = lax.broadcasted_iota(I32, (BRANCH, LANES), 0) // FX_D
    l = lax.broadcasted_iota(I32, (BRANCH, LANES), 1)
    q = q_ref[...].astype(F32)
    k = k_ref[...].astype(F32)
    o_ref[...] = (_dot((q * q).astype(BF16), jnp.where(c == l, 1.0, 0.0).astype(BF16))
                  + _dot((k * k).astype(BF16), jnp.where(c + FX_HEADS == l, 1.0, 0.0).astype(BF16)))


def fox_row_norms(p_fx, tm=SCAN_ROWS):
    m = p_fx.shape[0]
    tm = min(tm, m)
    return pl.pallas_call(
        _fox_norm_body,
        out_shape=jax.ShapeDtypeStruct((m, LANES), F32),
        grid=(m // tm,),
        in_specs=[pl.BlockSpec((tm, BRANCH), lambda i: (i, 0)), pl.BlockSpec((tm, BRANCH), lambda i: (i, 1))],
        out_specs=pl.BlockSpec((tm, LANES), lambda i: (i, 0)),
        compiler_params=_cp(("parallel",)),
        name="fox_row_norms",
    )(p_fx, p_fx)


def _fox_bounds_body(nsq_ref, cum_ref, o_ref, *, tq, tk):
    seq = nsq_ref.shape[0]
    nq, nk = seq // tq, seq // tk
    nsq = nsq_ref[...]
    qn = jnp.sqrt(jnp.max(nsq.reshape(nq, tq, LANES), axis=1))
    kn = jnp.sqrt(jnp.max(nsq, axis=0, keepdims=True))
    kn = pltpu.roll(kn, LANES - FX_HEADS, 1)
    cum = cum_ref[...]
    cq_first = cum.reshape(nq, tq, LANES)[:, 0, :]
    ck_last = cum.reshape(nk, tk, LANES)[:, tk - 1, :]
    thr = cq_first + 2.0 * FOX_NORM_SLACK * qn * kn + FOX_SKIP_MARGIN
    start = jnp.zeros((nq, LANES), I32)
    for j in range(nk):
        start = start + jnp.where(ck_last[j:j + 1, :] > thr, 1, 0)
    o_ref[0] = start


def fox_first_blocks(nsq, cum, batch, seq, tq, tk):
    return pl.pallas_call(
        functools.partial(_fox_bounds_body, tq=tq, tk=tk),
        out_shape=jax.ShapeDtypeStruct((batch, seq // tq, LANES), I32),
        grid=(batch,),
        in_specs=[pl.BlockSpec((seq, LANES), lambda b: (b, 0)), pl.BlockSpec((seq, LANES), lambda b: (b, 0))],
        out_specs=pl.BlockSpec((1, seq // tq, LANES), lambda b: (b, 0, 0)),
        compiler_params=_cp(("parallel",)),
        name="fox_first_blocks",
    )(nsq, cum)


def _fox_body(first_ref, q_ref, qb_ref, k_ref, kb_ref, v_ref, o_ref, m_ref, l_ref, acc_ref, *, tq, tk, nq):
    b = pl.program_id(0)
    i = pl.program_id(1)
    j_diag = (i * tq + tq - 1) // tk
    m_ref[...] = jnp.full_like(m_ref, NEG_BIG)
    l_ref[...] = jnp.zeros_like(l_ref)
    acc_ref[...] = jnp.zeros_like(acc_ref)

    def head_block(h, j, masked):
        hs = slice(h * FX_D, (h + 1) * FX_D)
        ksl = pl.ds(pl.multiple_of(j * tk, tk), tk)
        qa = jnp.concatenate([q_ref[:, hs], qb_ref[:, hs]], axis=1)
        ka = jnp.concatenate([k_ref[ksl, hs], kb_ref[ksl, hs]], axis=1)
        s = _dot_nt(qa, ka)
        if masked:
            t_glob = i * tq + lax.broadcasted_iota(I32, (tq, 1), 0)
            s_glob = j * tk + lax.broadcasted_iota(I32, (1, tk), 1)
            s = jnp.where(s_glob <= t_glob, s, -jnp.inf)
        m_prev = m_ref[h][:, 0:1]
        m_new = jnp.maximum(m_prev, jnp.max(s, axis=1, keepdims=True))
        alpha = jnp.exp2(m_prev - m_new)
        p = jnp.exp2(s - m_new)
        l_new = alpha * l_ref[h][:, 0:1] + jnp.sum(p, axis=1, keepdims=True)
        acc_ref[h] = alpha * acc_ref[h] + _dot(p.astype(BF16), v_ref[ksl, hs])
        m_ref[h] = jnp.broadcast_to(m_new, (tq, LANES))
        l_ref[h] = jnp.broadcast_to(l_new, (tq, LANES))

    base = (b * nq + i) * FX_HEADS
    for h0 in range(0, FX_HEADS, 2):
        j0 = jnp.minimum(first_ref[base + h0], first_ref[base + h0 + 1])

        def body(j, carry, h0=h0):
            head_block(h0, j, False)
            head_block(h0 + 1, j, False)
            return carry

        lax.fori_loop(j0, j_diag, body, 0)
        head_block(h0, j_diag, True)
        head_block(h0 + 1, j_diag, True)

    for h in range(FX_HEADS):
        o_ref[:, h * FX_D:(h + 1) * FX_D] = (acc_ref[h] / l_ref[h][:, 0:1]).astype(o_ref.dtype)


def fox_attention(p_fx, qb, kb, first_blocks, batch, seq, tq=ATTN_Q_ROWS, tk=ATTN_KEYS):
    tq, tk = min(tq, seq), min(tk, seq)
    assert tk % tq == 0
    nq = seq // tq
    once = pl.Buffered(1)
    qspec = lambda colblk: pl.BlockSpec((tq, BRANCH), lambda b, i, f: (b * nq + i, colblk))
    kspec = lambda colblk: pl.BlockSpec((seq, BRANCH), lambda b, i, f: (b, colblk), pipeline_mode=once)
    return pl.pallas_call(
        functools.partial(_fox_body, tq=tq, tk=tk, nq=nq),
        out_shape=jax.ShapeDtypeStruct((batch * seq, BRANCH), BF16),
        grid_spec=pltpu.PrefetchScalarGridSpec(
            num_scalar_prefetch=1,
            grid=(batch, nq),
            in_specs=[qspec(0), qspec(0), kspec(1), kspec(0), kspec(2)],
            out_specs=pl.BlockSpec((tq, BRANCH), lambda b, i, f: (b * nq + i, 0)),
            scratch_shapes=[pltpu.VMEM((FX_HEADS, tq, LANES), F32),
                            pltpu.VMEM((FX_HEADS, tq, LANES), F32),
                            pltpu.VMEM((FX_HEADS, tq, FX_D), F32)]),
        compiler_params=_cp(("parallel", "arbitrary")),
        name="fox_attention",
    )(first_blocks, p_fx, qb, p_fx, kb, p_fx)


def fox_mixer(p_fx, p_misc, f_bias, batch, seq, tq=ATTN_Q_ROWS, tk=ATTN_KEYS):
    tq, tk = min(tq, seq), min(tk, seq)
    qb, kb, cum = fox_bias_columns(p_misc, f_bias, batch, seq)
    first = fox_first_blocks(fox_row_norms(p_fx), cum, batch, seq, tq, tk)
    first = first[:, :, :FX_HEADS].reshape(-1)
    return fox_attention(p_fx, qb, kb, first, batch, seq, tq, tk)


def _qlat_body(q_ref, wuk_ref, o_ref):
    r = _dot_nt(q_ref[...], wuk_ref[...]) * (DS_HEAD_DIM ** -0.5 * LOG2E)
    o_ref[...] = r.astype(o_ref.dtype).reshape(o_ref.shape)


def dsa_qlat(qcat, w_uk, tq, tm=PROJ_ROWS):
    m = qcat.shape[0]
    tm = min(tm, m)
    return pl.pallas_call(
        _qlat_body,
        out_shape=jax.ShapeDtypeStruct((m // tq, DS_HEADS, tq, DS_KV_LORA), BF16),
        grid=(m // tm, DS_HEADS),
        in_specs=[pl.BlockSpec((tm, DS_HEAD_DIM), lambda i, h: (i, h)),
                  pl.BlockSpec((DS_KV_LORA, DS_HEAD_DIM), lambda i, h: (0, h))],
        out_specs=pl.BlockSpec((tm // tq, 1, tq, DS_KV_LORA), lambda i, h: (i, h, 0, 0)),
        compiler_params=_cp(("parallel", "arbitrary")),
        name="dsa_qlat",
    )(qcat, w_uk)


def _dsa_body(qi_ref, misc_ref, ql_ref, kidx_ref, ckv_ref, wuv_ref, o_ref,
              keys_ref, wb_ref, thr_ref, s_ref, p_ref, m_ref, l_ref, alpha_ref, acc_ref, *, tq, tk, topk):
    seq_len = kidx_ref.shape[0]
    i = pl.program_id(1)
    nh = DS_HEADS
    rows = nh * tq
    nkb = (i * tq + tq - 1) // tk + 1
    t_glob = i * tq + lax.broadcasted_iota(I32, (tq, 1), 0)
    col = lax.broadcasted_iota(I32, (1, tk), 1)
    w = misc_ref[:, DS_IDX_DIM:DS_IDX_DIM + DS_IDX_HEADS] * (DS_IDX_HEADS ** -0.5 * DS_IDX_DIM ** -0.5)
    for h in range(DS_IDX_HEADS):
        wb_ref[h] = jnp.broadcast_to(w[:, h:h + 1], (tq, LANES))
    qh = [qi_ref[:, h * DS_IDX_DIM:(h + 1) * DS_IDX_DIM] for h in range(DS_IDX_HEADS)]
    nlt = tk // LANES

    def score_body(j, carry):
        kb = kidx_ref[pl.ds(pl.multiple_of(j * tk, tk), tk), :]
        tiles = [jnp.zeros((tq, LANES), F32) for _ in range(nlt)]
        for h in range(DS_IDX_HEADS):
            lg = jnp.maximum(_dot_nt(qh[h], kb), 0.0)
            wbh = wb_ref[h]
            tiles = [t + lg[:, cc * LANES:(cc + 1) * LANES] * wbh for cc, t in enumerate(tiles)]
        sc = jnp.concatenate(tiles, axis=1)
        sc = jnp.where(sc == 0.0, 0.0, sc)
        sc = jnp.where(j * tk + col <= t_glob, sc, -jnp.inf)
        bits = pltpu.bitcast(sc, I32)
        keys_ref[j] = jnp.where(bits < 0, bits ^ 0x7FFFFFFF, bits)
        return carry

    lax.fori_loop(0, nkb, score_body, 0)

    def count_ge(cand):
        def body(j, acc):
            ge = jnp.where(keys_ref[j] >= cand, 1.0, 0.0)
            part = ge[:, 0:LANES]
            for cc in range(1, tk // LANES):
                part = part + ge[:, cc * LANES:(cc + 1) * LANES]
            return acc + part
        acc = lax.fori_loop(0, nkb, body, jnp.zeros((tq, LANES), F32))
        return jnp.sum(acc, axis=1, keepdims=True)

    def top2_body(j, carry):
        m1, m2 = carry
        for cc in range(nlt):
            x = keys_ref[j, :, cc * LANES:(cc + 1) * LANES]
            m2 = jnp.maximum(m2, jnp.minimum(m1, x))
            m1 = jnp.maximum(m1, x)
        return m1, m2

    init = jnp.full((tq, LANES), INT_MIN, I32)
    m1, m2 = lax.fori_loop(0, nkb, top2_body, (init, init))

    def key_to_score(key):
        return pltpu.bitcast(jnp.where(key < 0, key ^ 0x7FFFFFFF, key), F32)

    def score_to_key(sc):
        bits = pltpu.bitcast(sc, I32)
        return jnp.where(bits < 0, bits ^ 0x7FFFFFFF, bits)

    lo = score_to_key(jnp.min(key_to_score(m2), axis=1, keepdims=True))
    hi = score_to_key(jnp.max(key_to_score(m1), axis=1, keepdims=True))

    kf = float(topk)

    def bisect(lo, hi, cnt):
        mid = (lo | hi) - ((lo ^ hi) >> 1)
        c = count_ge(mid)
        take = c >= kf
        return jnp.where(take, mid, lo), jnp.where(take, hi, mid - 1), jnp.where(take, c, cnt)

    passes_per_round = 3

    def search_cond(carry):
        lo, hi, cnt = carry
        return jnp.max(jnp.where(lo < hi, cnt, kf)) > kf

    def search_round(carry):
        lo, hi, cnt = carry
        for _ in range(passes_per_round):
            lo, hi, cnt = bisect(lo, hi, cnt)
        return lo, hi, cnt

    unknown = jnp.full((tq, 1), 2.0 * kf + float(seq_len), F32)
    thr, _, cnt = lax.while_loop(search_cond, search_round, (lo, hi, unknown))
    thr_ref[...] = thr

    @pl.when(jnp.max(cnt) > kf)
    def _():
        def count_tied(x):
            def body(j, acc):
                hit = (keys_ref[j] == thr) & (j * tk + col <= x)
                ge = jnp.where(hit, 1.0, 0.0)
                part = ge[:, 0:LANES]
                for cc in range(1, nlt):
                    part = part + ge[:, cc * LANES:(cc + 1) * LANES]
                return acc + part
            acc = lax.fori_loop(0, nkb, body, jnp.zeros((tq, LANES), F32))
            return jnp.sum(acc, axis=1, keepdims=True)

        need = kf - count_ge(thr + 1)

        def cut_bit(bi, t):
            cand = t + lax.shift_left(jnp.int32(1), bi)
            return jnp.where(count_tied(cand - 1) < need, cand, t)

        nbits = max(1, (seq_len - 1).bit_length())
        cut = lax.fori_loop(0, nbits, lambda r, t: cut_bit(nbits - 1 - r, t), jnp.zeros((tq, 1), I32))

        def demote(j, carry):
            kk = keys_ref[j]
            keys_ref[j] = jnp.where((kk == thr) & (j * tk + col > cut), thr - 1, kk)
            return carry

        lax.fori_loop(0, nkb, demote, 0)

    m_ref[...] = jnp.full_like(m_ref, NEG_BIG)
    l_ref[...] = jnp.zeros_like(l_ref)
    acc_ref[...] = jnp.zeros_like(acc_ref)
    rc = SOFTMAX_ROWS

    def attn_body(j, carry):
        kv = ckv_ref[pl.ds(pl.multiple_of(j * tk, tk), tk), :]
        s_ref[...] = _dot_nt(ql_ref[0].reshape(rows, DS_KV_LORA), kv)

        for c in range(rows // rc):
            tok0 = (c // nh) * rc
            r0 = (c % nh) * tq + tok0
            rsl, tsl = slice(r0, r0 + rc), slice(tok0, tok0 + rc)
            if c % nh == 0:
                tg = i * tq + tok0 + lax.broadcasted_iota(I32, (rc, 1), 0)
                sel = (keys_ref[j, tsl, :] >= thr_ref[tsl, :]) & (j * tk + col <= tg)
            s = jnp.where(sel, s_ref[rsl, :], -jnp.inf)
            m_prev = m_ref[rsl, :]
            m_new = jnp.maximum(m_prev, jnp.max(s, axis=1, keepdims=True))
            alpha = jnp.exp2(m_prev - m_new)
            p = jnp.exp2(s - m_new)
            l_ref[rsl, :] = alpha * l_ref[rsl, :] + jnp.sum(p, axis=1, keepdims=True)
            m_ref[rsl, :] = m_new
            alpha_ref[rsl, :] = jnp.broadcast_to(alpha, (rc, LANES))
            p_ref[rsl, :] = p.astype(BF16)

        pv = _dot(p_ref[...], kv)
        alpha_b = alpha_ref[...]
        for cc in range(DS_KV_LORA // LANES):
            lsl = slice(cc * LANES, (cc + 1) * LANES)
            acc_ref[:, lsl] = alpha_b * acc_ref[:, lsl] + pv[:, lsl]
        return carry

    lax.fori_loop(0, nkb, attn_body, 0)

    for h in range(nh):
        hsl = slice(h * tq, (h + 1) * tq)
        o_lat = (acc_ref[hsl, :] / l_ref[hsl, :]).astype(BF16)
        o_ref[:, h * DS_HEAD_DIM:(h + 1) * DS_HEAD_DIM] = _dot(
            o_lat, wuv_ref[:, h * DS_HEAD_DIM:(h + 1) * DS_HEAD_DIM]).astype(o_ref.dtype)


def dsa_attention(qcat, p_misc, q_lat, k_idx, c_kv, w_uv, batch, seq, tq=ATTN_Q_ROWS, tk=ATTN_KEYS):
    tq, tk = min(tq, seq), min(tk, seq)
    nq = seq // tq
    rows = DS_HEADS * tq
    topk = min(DS_TOPK_MAX, seq // 4)
    assert tk >= topk and topk <= 2 * LANES and tq % SOFTMAX_ROWS == 0
    return pl.pallas_call(
        functools.partial(_dsa_body, tq=tq, tk=tk, topk=topk),
        out_shape=jax.ShapeDtypeStruct((batch * seq, BRANCH), BF16),
        grid=(batch, nq),
        in_specs=[pl.BlockSpec((tq, DS_IDX_HEADS * DS_IDX_DIM), lambda b, i: (b * nq + i, 1)),
                  pl.BlockSpec((tq, LANES), lambda b, i: (b * nq + i, MISC_IDX // LANES)),
                  pl.BlockSpec((1, DS_HEADS, tq, DS_KV_LORA), lambda b, i: (b * nq + i, 0, 0, 0)),
                  pl.BlockSpec((seq, DS_IDX_DIM), lambda b, i: (b, 0)),
                  pl.BlockSpec((seq, DS_KV_LORA), lambda b, i: (b, 0)),
                  pl.BlockSpec((DS_KV_LORA, BRANCH), lambda b, i: (0, 0))],
        out_specs=pl.BlockSpec((tq, BRANCH), lambda b, i: (b * nq + i, 0)),
        scratch_shapes=[pltpu.VMEM((seq // tk, tq, tk), I32),
                        pltpu.VMEM((DS_IDX_HEADS, tq, LANES), F32),
                        pltpu.VMEM((tq, 1), I32),
                        pltpu.VMEM((rows, tk), F32),
                        pltpu.VMEM((rows, tk), BF16),
                        pltpu.VMEM((rows, 1), F32),
                        pltpu.VMEM((rows, 1), F32),
                        pltpu.VMEM((rows, LANES), F32),
                        pltpu.VMEM((rows, DS_KV_LORA), F32)],
        compiler_params=_cp(("parallel", "arbitrary")),
        name="dsa_attention",
    )(qcat, p_misc, q_lat, k_idx, c_kv, w_uv)


def _merge_body(h_ref, wg_ref, wb_ref, ya_ref, yb_ref, yc_ref, yd_ref, o_ref, acc_ref):
    n = pl.program_id(2)
    gate = jax.nn.sigmoid(_dot(h_ref[...], wg_ref[0].astype(BF16)))
    wb = wb_ref[0].astype(BF16)
    for idx, y_ref in enumerate((ya_ref, yb_ref, yc_ref, yd_ref)):
        @pl.when(n == idx)
        def _(y_ref=y_ref, idx=idx):
            contrib = gate * _dot(y_ref[...], wb)
            if idx == 0:
                acc_ref[...] = contrib
            else:
                acc_ref[...] += contrib

    @pl.when(n == pl.num_programs(2) - 1)
    def _():
        o_ref[...] = acc_ref[...].astype(o_ref.dtype)


def gated_merge(h, w_gate, w_branch, layer, ys, tm=MERGE_ROWS, tn=MERGE_COLS):
    m = h.shape[0]
    tm = min(tm, m)
    ymap = lambda i, j, n: (i, 0)
    once = pl.Buffered(1)
    return pl.pallas_call(
        _merge_body,
        out_shape=jax.ShapeDtypeStruct((m, D_MODEL), BF16),
        grid=(m // tm, D_MODEL // tn, 4),
        in_specs=[pl.BlockSpec((tm, D_MODEL), lambda i, j, n: (i, 0), pipeline_mode=once),
                  pl.BlockSpec((None, 1, D_MODEL, tn), lambda i, j, n: (layer, n, 0, j)),
                  pl.BlockSpec((None, 1, BRANCH, tn), lambda i, j, n: (layer, n, 0, j))]
                 + [pl.BlockSpec((tm, BRANCH), ymap, pipeline_mode=once) for _ in range(4)],
        out_specs=pl.BlockSpec((tm, tn), lambda i, j, n: (i, j)),
        scratch_shapes=[pltpu.VMEM((tm, tn), F32)],
        compiler_params=_cp(("parallel", "parallel", "arbitrary"), VMEM_BIG_MB),
        name="gated_merge",
    )(h, w_gate, w_branch, *ys)


def _ffn_up_body(h_ref, halo_ref, wg_ref, wu_ref, cw_ref, o_ref, *, tiles_per_seq):
    i = pl.program_id(0)
    wg = wg_ref[...].astype(BF16)
    g = _dot(h_ref[...], wg)
    u = _dot(h_ref[...], wu_ref[...].astype(BF16))
    gh = _dot(halo_ref[...], wg)
    gh = gh * jnp.where(i % tiles_per_seq == 0, 0.0, 1.0)
    cw = cw_ref[...]
    row8 = lax.broadcasted_iota(I32, (8, 1), 0)
    y = g * cw[FFN_CONV - 1:FFN_CONV, :]
    for j in range(1, FFN_CONV):
        rolled = pltpu.roll(g, j, 0)
        head = jnp.where(row8 < j, pltpu.roll(gh, j, 0), rolled[0:8])
        shifted = jnp.concatenate([head, rolled[8:]], axis=0)
        y = y + shifted * cw[FFN_CONV - 1 - j:FFN_CONV - j, :]
    o_ref[...] = (_silu(y) * u).astype(o_ref.dtype)


def ffn_up(h, w_gate, w_up, conv_w, layer, seq, tm=FFN_ROWS, tn=FFN_COLS):
    m = h.shape[0]
    tm = min(tm, seq)
    dff = w_gate.shape[-1]
    wspec = pl.BlockSpec((None, D_MODEL, tn), lambda i, j: (layer, 0, j))
    return pl.pallas_call(
        functools.partial(_ffn_up_body, tiles_per_seq=seq // tm),
        out_shape=jax.ShapeDtypeStruct((m, dff), BF16),
        grid=(m // tm, dff // tn),
        in_specs=[pl.BlockSpec((tm, D_MODEL), lambda i, j: (i, 0), pipeline_mode=pl.Buffered(1)),
                  pl.BlockSpec((8, D_MODEL), lambda i, j: (jnp.maximum(i * (tm // 8) - 1, 0), 0)),
                  wspec, wspec,
                  pl.BlockSpec((None, FFN_CONV, tn), lambda i, j: (layer, 0, j))],
        out_specs=pl.BlockSpec((tm, tn), lambda i, j: (i, j)),
        compiler_params=_cp(("parallel", "arbitrary")),
        name="ffn_up",
    )(h, h, w_gate, w_up, conv_w.astype(F32))


def _in_proj_weights(w, l):
    o = 4 * BRANCH
    secs = {}
    for name, width in (("cq", DS_Q_LORA), ("ckv", DS_KV_LORA), ("kidx", DS_IDX_DIM), ("widx", DS_IDX_HEADS),
                        ("z", BRANCH), ("xbc", MB_CONV_DIM), ("dt", MB_HEADS), ("fx", 3 * BRANCH),
                        ("ff", FX_HEADS)):
        secs[name] = w[l, :, o:o + width]
        o += width
    zeros = lambda n: jnp.zeros((w.shape[1], n), w.dtype)
    misc = jnp.concatenate([secs["ckv"], secs["kidx"], secs["widx"], zeros(LANES - DS_IDX_DIM - DS_IDX_HEADS),
                            secs["dt"], zeros(LANES - MB_HEADS), secs["ff"], zeros(LANES - FX_HEADS)], axis=1)
    return dict(cq=secs["cq"], misc=misc, z=secs["z"], xbc=secs["xbc"], fx=secs["fx"])


def _layer(x, l, batch, seq, prm):
    h = rmsnorm(x, prm["attn_norm"][l], BF16)
    w = _in_proj_weights(prm["w_in_bf"], l)
    p_hg = matmul(h, prm["w_in_bf"], F32, n=4 * BRANCH, b_lead=(l,))
    p_cq = matmul(h, w["cq"], F32, tn=DS_Q_LORA)
    p_misc = matmul(h, w["misc"], F32, tn=MISC_W)
    p_z = matmul(h, w["z"], F32)
    p_xbc = matmul(h, w["xbc"], F32, tn=MB_CONV_DIM // 2)
    fx_scale = jnp.concatenate([jnp.full((1, BRANCH), FX_D ** -0.5 * LOG2E, F32), jnp.ones((1, 2 * BRANCH), F32)], 1)
    p_fx = matmul(h, w["fx"], BF16, col_scale=fx_scale)

    y_a = hgrn2(p_hg, prm["hgrn_lb_logits"], prm["hgrn_norm"][l], l, batch, seq)

    c_q = rmsnorm(p_cq, prm["dsa_q_norm"][l], BF16)
    c_kv = rmsnorm(p_misc, prm["dsa_kv_norm"][l], BF16, width=DS_KV_LORA, col_block=0)
    k_idx = p_misc[:, MISC_IDX:MISC_IDX + DS_IDX_DIM].astype(BF16)
    w_q = jnp.concatenate([prm["dsa_w_uq"][l].astype(BF16), prm["dsa_w_iq"][l].astype(BF16)], axis=1)
    qcat = matmul(c_q, w_q, BF16)
    dsa_tq = min(ATTN_Q_ROWS, seq)
    q_lat = dsa_qlat(qcat, prm["dsa_w_uk"][l].astype(BF16), dsa_tq)
    y_b = dsa_attention(qcat, p_misc, q_lat, k_idx, c_kv, prm["dsa_w_uv"][l].astype(BF16), batch, seq,
                        tq=dsa_tq)

    y_c = mamba2(p_z, p_xbc, p_misc, prm["ssm_conv_w"][l], prm["ssm_conv_b"][l], prm["ssm_dt_bias"][l],
                 prm["ssm_a_log"][l], prm["ssm_d"][l], prm["ssm_norm"][l], batch, seq)

    y_d = fox_mixer(p_fx, p_misc, prm["fox_f_bias"][l], batch, seq)

    merged = gated_merge(h, prm["w_gate"], prm["w_branch"], l, (y_a, y_b, y_c, y_d))
    x = matmul(merged, prm["w_out_bf"], F32, residual=x, b_lead=(l,), vmem_mb=VMEM_BIG_MB)

    h2 = rmsnorm(x, prm["ffn_norm"][l], BF16)
    act = ffn_up(h2, prm["ffn_w_gate"], prm["ffn_w_up"], prm["ffn_conv"], l, seq)
    return matmul(act, prm["w_down_bf"], F32, DOWN_ROWS, DOWN_COLS, residual=x, b_lead=(l,), vmem_mb=VMEM_BIG_MB)


def kernel(x, attn_norm, ffn_norm, final_norm, w_in, hgrn_lb_logits, hgrn_norm, dsa_q_norm, dsa_kv_norm,
           dsa_w_uq, dsa_w_iq, dsa_w_uk, dsa_w_uv, ssm_conv_w, ssm_conv_b, ssm_dt_bias, ssm_a_log, ssm_d,
           ssm_norm, fox_f_bias, w_gate, w_branch, w_out, ffn_w_gate, ffn_w_up, ffn_conv, ffn_w_down):
    batch, seq, d = x.shape
    prm = dict(attn_norm=attn_norm, ffn_norm=ffn_norm, w_in=w_in, hgrn_lb_logits=hgrn_lb_logits,
               hgrn_norm=hgrn_norm, dsa_q_norm=dsa_q_norm, dsa_kv_norm=dsa_kv_norm, dsa_w_uq=dsa_w_uq,
               dsa_w_iq=dsa_w_iq, dsa_w_uk=dsa_w_uk, dsa_w_uv=dsa_w_uv, ssm_conv_w=ssm_conv_w,
               ssm_conv_b=ssm_conv_b, ssm_dt_bias=ssm_dt_bias, ssm_a_log=ssm_a_log, ssm_d=ssm_d,
               ssm_norm=ssm_norm, fox_f_bias=fox_f_bias, w_gate=w_gate, w_branch=w_branch, w_out=w_out,
               ffn_w_gate=ffn_w_gate, ffn_w_up=ffn_w_up, ffn_conv=ffn_conv, ffn_w_down=ffn_w_down)
    prm.update(w_in_bf=w_in.astype(BF16), w_out_bf=w_out.astype(BF16), w_down_bf=ffn_w_down.astype(BF16))
    xf = x.reshape(batch * seq, d)
    for l in range(DEPTH):
        xf = _layer(xf, l, batch, seq, prm)
    return rmsnorm(xf, final_norm, x.dtype).reshape(batch, seq, d)
```

```python
import functools

import jax
import jax.numpy as jnp
from jax import lax
from jax.experimental import pallas as pl
from jax.experimental.pallas import tpu as pltpu

F32, BF16, I32 = jnp.float32, jnp.bfloat16, jnp.int32
HIGHEST = lax.Precision.HIGHEST

D_MODEL = 4096
DEPTH = 2
BRANCH = 1024
HG_HEADS, HG_D, HG_CHUNK, HG_SUB = 8, 128, 64, 16
DS_HEADS, DS_HEAD_DIM, DS_Q_LORA, DS_KV_LORA = 8, 128, 768, 512
DS_IDX_HEADS, DS_IDX_DIM, DS_TOPK_MAX = 16, 64, 256
MB_HEADS, MB_P, MB_N, MB_GROUPS, MB_CONV, MB_CHUNK = 16, 64, 128, 2, 4, 128
MB_CONV_DIM = BRANCH + 2 * MB_GROUPS * MB_N
FX_HEADS, FX_D = 8, 128
D_FF = 11008
FFN_CONV = 3
EPS = 1e-6
LANES = 128
NEG_BIG = -1e30
LOG2E = 1.4426950408889634
FOX_SKIP_MARGIN = 160.0
FOX_NORM_SLACK = 1.02
INT_MIN = -(2 ** 31)

MISC_CKV = 0
MISC_IDX = 512
MISC_DT = 640
MISC_FF = 768
MISC_W = 896


VMEM_MB, VMEM_BIG_MB = 48, 56
PROJ_ROWS, PROJ_COLS = 1024, 1024
DOWN_ROWS, DOWN_COLS = 512, 512
MERGE_ROWS, MERGE_COLS = 1024, 512
FFN_ROWS, FFN_COLS = 2048, 256
ATTN_Q_ROWS, ATTN_KEYS = 256, 512
HG_ROWS, HG_HEADS_PER_STEP = 256, 8
NORM_ROWS, SCAN_ROWS = 256, 512


def _cp(sem, vmem_mb=VMEM_MB):
    return pltpu.CompilerParams(dimension_semantics=sem, vmem_limit_bytes=vmem_mb * 2 ** 20)


def _dot(a, b, precision=None):
    return jnp.dot(a, b, preferred_element_type=F32, precision=precision)


def _dot_nt(a, b):
    return lax.dot_general(a, b, (((1,), (1,)), ((), ())), preferred_element_type=F32)


def _dot_tn(a, b):
    return lax.dot_general(a, b, (((0,), (0,)), ((), ())), preferred_element_type=F32)


def _tril(n):
    r = lax.broadcasted_iota(I32, (n, n), 0)
    c = lax.broadcasted_iota(I32, (n, n), 1)
    return (r >= c).astype(F32)


def _silu(x):
    return x * jax.nn.sigmoid(x)


def _rmsnorm_body(x_ref, g_ref, o_ref):
    x = x_ref[...].astype(F32)
    ms = jnp.mean(x * x, axis=-1, keepdims=True)
    o_ref[...] = (x * lax.rsqrt(ms + EPS) * g_ref[...]).astype(o_ref.dtype)


def rmsnorm(x, gain, out_dtype, width=None, col_block=0, tm=NORM_ROWS):
    m = x.shape[0]
    width = x.shape[1] if width is None else width
    return pl.pallas_call(
        _rmsnorm_body,
        out_shape=jax.ShapeDtypeStruct((m, width), out_dtype),
        grid=(m // tm,),
        in_specs=[pl.BlockSpec((tm, width), lambda i: (i, col_block)),
                  pl.BlockSpec((1, width), lambda i: (0, 0))],
        out_specs=pl.BlockSpec((tm, width), lambda i: (i, 0)),
        compiler_params=_cp(("parallel",)),
        name="rmsnorm",
    )(x, gain.reshape(1, width).astype(F32))


def _mm_body(a_ref, b_ref, o_ref):
    o_ref[...] = _dot(a_ref[...], b_ref[...].astype(BF16)).astype(o_ref.dtype)


def _mm_add_body(a_ref, b_ref, r_ref, o_ref):
    o_ref[...] = (r_ref[...] + _dot(a_ref[...], b_ref[...].astype(BF16))).astype(o_ref.dtype)


def _mm_scale_body(a_ref, b_ref, s_ref, o_ref):
    o_ref[...] = (_dot(a_ref[...], b_ref[...].astype(BF16)) * s_ref[...]).astype(o_ref.dtype)


def matmul(a, b, out_dtype, tm=PROJ_ROWS, tn=PROJ_COLS, residual=None, col_scale=None, n=None, b_lead=(),
           vmem_mb=VMEM_MB):
    m, k = a.shape
    n = b.shape[-1] if n is None else n
    tm, tn = min(tm, m), min(tn, n)
    in_specs = [pl.BlockSpec((tm, k), lambda i, j: (i, 0)),
                pl.BlockSpec((None,) * len(b_lead) + (k, tn), lambda i, j: tuple(b_lead) + (0, j))]
    args, body = (a, b), _mm_body
    if residual is not None:
        in_specs.append(pl.BlockSpec((tm, tn), lambda i, j: (i, j)))
        args, body = (a, b, residual), _mm_add_body
    elif col_scale is not None:
        in_specs.append(pl.BlockSpec((1, tn), lambda i, j: (0, j)))
        args, body = (a, b, col_scale), _mm_scale_body
    return pl.pallas_call(
        body,
        out_shape=jax.ShapeDtypeStruct((m, n), out_dtype),
        grid=(m // tm, n // tn),
        in_specs=in_specs,
        out_specs=pl.BlockSpec((tm, tn), lambda i, j: (i, j)),
        compiler_params=_cp(("parallel", "arbitrary"), vmem_mb),
        name="matmul",
    )(*args)


def _hgrn_body(lbl_ref, gain_ref, q_ref, f_ref, i_ref, g_ref, o_ref, st_ref, *, layer, nchunks, hpb):
    c = pl.program_id(2)

    @pl.when(c == 0)
    def _():
        st_ref[...] = jnp.zeros_like(st_ref)

    logits = lbl_ref[...]
    e = jnp.exp(logits - jnp.max(logits, axis=0, keepdims=True))
    p = e / jnp.sum(e, axis=0, keepdims=True)
    lb_all = jnp.sum(p[0:layer + 1], axis=0, keepdims=True) - p[0:1]
    gain_all = gain_ref[...]
    tril = _tril(HG_CHUNK)
    row = lax.broadcasted_iota(I32, (HG_CHUNK, 1), 0)
    row_in_sub = row % HG_SUB
    nsub = HG_CHUNK // HG_SUB

    def chunk(ci, carry):
        sl = pl.ds(pl.multiple_of(ci * HG_CHUNK, HG_CHUNK), HG_CHUNK)
        for hh in range(hpb):
            hs = slice(hh * HG_D, (hh + 1) * HG_D)
            lb, gain = lb_all[:, hs], gain_all[:, hs]
            q = _silu(q_ref[sl, hs])
            v = _silu(i_ref[sl, hs])
            f = lb + (1.0 - lb) * jax.nn.sigmoid(f_ref[sl, hs])
            k = 1.0 - f
            b = _dot(tril, jnp.log(f) * LOG2E, HIGHEST)
            st = st_ref[hh]

            o = _dot_nt((q * jnp.exp2(b)).astype(BF16), st.astype(BF16))

            intra = jnp.zeros((HG_CHUNK, HG_D), F32)
            for d in range(HG_SUB):
                ks = k if d == 0 else pltpu.roll(k, d, 0)
                bs = b if d == 0 else pltpu.roll(b, d, 0)
                vs = v if d == 0 else pltpu.roll(v, d, 0)
                expo = jnp.where(row_in_sub >= d, b - bs, -jnp.inf)
                w = jnp.sum(q * ks * jnp.exp2(expo), axis=-1, keepdims=True)
                intra = intra + w * vs
            o = o + intra

            parts = [jnp.zeros((HG_SUB, HG_D), F32)]
            for si in range(1, nsub):
                lo = si * HG_SUB
                r = b[lo - 1:lo, :]
                qi = (q[lo:lo + HG_SUB] * jnp.exp2(b[lo:lo + HG_SUB] - r)).astype(BF16)
                kj = (k[0:lo] * jnp.exp2(r - b[0:lo])).astype(BF16)
                sc = _dot_nt(qi, kj)
                parts.append(_dot(sc.astype(BF16), v[0:lo].astype(BF16)))
            o = o + jnp.concatenate(parts, axis=0)

            b_last = b[HG_CHUNK - 1:HG_CHUNK, :]
            kd = (k * jnp.exp2(b_last - b)).astype(BF16)
            st_ref[hh] = st * jnp.exp2(b_last) + _dot_tn(v.astype(BF16), kd)

            og = o * jax.nn.sigmoid(g_ref[sl, hs])
            ms = jnp.mean(og * og, axis=-1, keepdims=True)
            o_ref[sl, hs] = (og * lax.rsqrt(ms + EPS) * gain).astype(o_ref.dtype)
        return carry

    lax.fori_loop(0, nchunks, chunk, 0, unroll=True)


def hgrn2(p_hg, lb_logits, norm_gain, layer, batch, seq, tt=HG_ROWS, hpb=HG_HEADS_PER_STEP):
    tt = min(tt, seq)
    nt = seq // tt
    hb = HG_HEADS // hpb
    w = hpb * HG_D

    def col(sec):
        return lambda b, h, c: (b * nt + c, sec * hb + h)

    return pl.pallas_call(
        functools.partial(_hgrn_body, layer=layer, nchunks=tt // HG_CHUNK, hpb=hpb),
        out_shape=jax.ShapeDtypeStruct((batch * seq, BRANCH), BF16),
        grid=(batch, hb, nt),
        in_specs=[pl.BlockSpec((DEPTH, w), lambda b, h, c: (0, h)),
                  pl.BlockSpec((1, w), lambda b, h, c: (0, h)),
                  pl.BlockSpec((tt, w), col(0)),
                  pl.BlockSpec((tt, w), col(1)),
                  pl.BlockSpec((tt, w), col(2)),
                  pl.BlockSpec((tt, w), col(3))],
        out_specs=pl.BlockSpec((tt, w), lambda b, h, c: (b * nt + c, h)),
        scratch_shapes=[pltpu.VMEM((hpb, HG_D, HG_D), F32)],
        compiler_params=_cp(("parallel", "parallel", "arbitrary")),
        name="hgrn2",
    )(lb_logits.astype(F32), norm_gain.reshape(1, BRANCH).astype(F32), p_hg, p_hg, p_hg, p_hg)


def _mamba_body(z_ref, xbc_ref, dt_ref, cw_ref, cb_ref, dtb_ref, alog_ref, dsk_ref, gain_ref,
                o_ref, prev_ref, st_ref):
    c = pl.program_id(1)
    L = MB_CHUNK

    @pl.when(c == 0)
    def _():
        prev_ref[...] = jnp.zeros_like(prev_ref)
        st_ref[...] = jnp.zeros_like(st_ref)

    x = xbc_ref[...]
    prev = prev_ref[...]
    row = lax.broadcasted_iota(I32, (L, 1), 0)
    cw = cw_ref[...]
    acc = x * cw[MB_CONV - 1:MB_CONV, :] + cb_ref[...]
    for j in range(1, MB_CONV):
        sh = jnp.where(row >= j, pltpu.roll(x, j, 0), pltpu.roll(prev, j, 0))
        acc = acc + sh * cw[MB_CONV - 1 - j:MB_CONV - j, :]
    prev_ref[...] = x
    xbc = _silu(acc)
    xs = xbc[:, 0:BRANCH]
    bm = xbc[:, BRANCH:BRANCH + MB_GROUPS * MB_N]
    cm = xbc[:, BRANCH + MB_GROUPS * MB_N:MB_CONV_DIM]

    raw = dt_ref[...] + dtb_ref[...]
    dt = jnp.maximum(raw, 0.0) + jnp.log1p(jnp.exp(-jnp.abs(raw)))
    a = -jnp.exp(alog_ref[...]) * dt
    tril = _tril(L)
    a_cs = _dot(tril, a, HIGHEST)
    a_cs_t = a_cs.T
    causal = tril > 0.5
    dsk = dsk_ref[...]

    hpg = MB_HEADS // MB_GROUPS
    ys = []
    for g in range(MB_GROUPS):
        bg = bm[:, g * MB_N:(g + 1) * MB_N]
        cg = cm[:, g * MB_N:(g + 1) * MB_N]
        cb = _dot_nt(cg.astype(BF16), bg.astype(BF16))
        for hh in range(hpg):
            h = g * hpg + hh
            acol = a_cs[:, h:h + 1]
            arow = a_cs_t[h:h + 1, :]
            lmat = jnp.exp(jnp.where(causal, acol - arow, -jnp.inf))
            xh = xs[:, h * MB_P:(h + 1) * MB_P]
            xdt = (xh * dt[:, h:h + 1]).astype(BF16)
            y = _dot((cb * lmat).astype(BF16), xdt)
            st = st_ref[h]
            y = y + _dot(cg.astype(BF16), st.astype(BF16)) * jnp.exp(acol)
            a_last = a_cs[L - 1:L, h:h + 1]
            bdec = (bg * jnp.exp(a_last - acol)).astype(BF16)
            st_ref[h] = st * jnp.exp(a_last) + _dot_tn(bdec, xdt)
            ys.append(y + dsk[:, h:h + 1] * xh)
    y = jnp.concatenate(ys, axis=1)
    y = y * _silu(z_ref[...])
    gw = BRANCH // MB_GROUPS
    outs = []
    for g in range(MB_GROUPS):
        yg = y[:, g * gw:(g + 1) * gw]
        ms = jnp.mean(yg * yg, axis=-1, keepdims=True)
        outs.append(yg * lax.rsqrt(ms + EPS))
    o_ref[...] = (jnp.concatenate(outs, axis=1) * gain_ref[...]).astype(o_ref.dtype)


def _pad_lanes(v, n=LANES):
    v = v.reshape(1, -1).astype(F32)
    return jnp.pad(v, ((0, 0), (0, n - v.shape[1])))


def mamba2(p_z, p_xbc, p_misc, conv_w, conv_b, dt_bias, a_log, d_skip, norm_gain, batch, seq):
    nc = seq // MB_CHUNK
    L = MB_CHUNK
    full = lambda shape: pl.BlockSpec(shape, lambda b, c: (0, 0))
    return pl.pallas_call(
        _mamba_body,
        out_shape=jax.ShapeDtypeStruct((batch * seq, BRANCH), BF16),
        grid=(batch, nc),
        in_specs=[pl.BlockSpec((L, BRANCH), lambda b, c: (b * nc + c, 0)),
                  pl.BlockSpec((L, MB_CONV_DIM), lambda b, c: (b * nc + c, 0)),
                  pl.BlockSpec((L, LANES), lambda b, c: (b * nc + c, MISC_DT // LANES)),
                  full((MB_CONV, MB_CONV_DIM)), full((1, MB_CONV_DIM)),
                  full((1, LANES)), full((1, LANES)), full((1, LANES)), full((1, BRANCH))],
        out_specs=pl.BlockSpec((L, BRANCH), lambda b, c: (b * nc + c, 0)),
        scratch_shapes=[pltpu.VMEM((L, MB_CONV_DIM), F32),
                        pltpu.VMEM((MB_HEADS, MB_N, MB_P), F32)],
        compiler_params=_cp(("parallel", "arbitrary")),
        name="mamba2",
    )(p_z, p_xbc, p_misc, conv_w.astype(F32), conv_b.reshape(1, -1).astype(F32),
      _pad_lanes(dt_bias), _pad_lanes(a_log), _pad_lanes(d_skip), norm_gain.reshape(1, BRANCH).astype(F32))


def _fox_cum_body(f_ref, bias_ref, qb_ref, kb_ref, cum_ref, carry_ref, *, tt):
    c = pl.program_id(1)

    @pl.when(c == 0)
    def _():
        carry_ref[...] = jnp.zeros_like(carry_ref)

    logf = jax.nn.log_sigmoid(f_ref[...] + bias_ref[...])
    cum = _dot(_tril(tt), logf, HIGHEST) + carry_ref[...]
    carry_ref[...] = cum[tt - 1:tt, :]

    c2 = cum * LOG2E
    cum_ref[...] = c2
    lane = lax.broadcasted_iota(I32, (1, LANES), 1)
    ones = jnp.where(lane < 6, 1.0, 0.0)
    for h in range(FX_HEADS):
        col = c2[:, h:h + 1]
        hi = col.astype(BF16).astype(F32)
        r1 = col - hi
        mid = r1.astype(BF16).astype(F32)
        lo = r1 - mid
        qb = jnp.where(lane == 0, hi, jnp.where(lane == 1, mid, jnp.where(lane == 2, lo, ones)))
        kb = jnp.where(lane == 3, -hi, jnp.where(lane == 4, -mid, jnp.where(lane == 5, -lo, ones)))
        qb_ref[:, h * LANES:(h + 1) * LANES] = qb.astype(BF16)
        kb_ref[:, h * LANES:(h + 1) * LANES] = kb.astype(BF16)


def fox_bias_columns(p_misc, f_bias, batch, seq, tt=SCAN_ROWS):
    tt = min(tt, seq)
    nt = seq // tt
    m = batch * seq
    out = jax.ShapeDtypeStruct((m, FX_HEADS * LANES), BF16)
    ospec = pl.BlockSpec((tt, FX_HEADS * LANES), lambda b, c: (b * nt + c, 0))
    return pl.pallas_call(
        functools.partial(_fox_cum_body, tt=tt),
        out_shape=(out, out, jax.ShapeDtypeStruct((m, LANES), F32)),
        grid=(batch, nt),
        in_specs=[pl.BlockSpec((tt, LANES), lambda b, c: (b * nt + c, MISC_FF // LANES)),
                  pl.BlockSpec((1, LANES), lambda b, c: (0, 0))],
        out_specs=(ospec, ospec, pl.BlockSpec((tt, LANES), lambda b, c: (b * nt + c, 0))),
        scratch_shapes=[pltpu.VMEM((1, LANES), F32)],
        compiler_params=_cp(("parallel", "arbitrary")),
        name="fox_bias_columns",
    )(p_misc, _pad_lanes(f_bias))


def _fox_norm_body(q_ref, k_ref, o_ref):
    c = lax.broadcasted_iota(I32, (BRANCH, LANES), 0) // FX_D
    l = lax.broadcasted_iota(I32, (BRANCH, LANES), 1)
    q = q_ref[...].astype(F32)
    k = k_ref[...].astype(F32)
    o_ref[...] = (_dot((q * q).astype(BF16), jnp.where(c == l, 1.0, 0.0).astype(BF16))
                  + _dot((k * k).astype(BF16), jnp.where(c + FX_HEADS == l, 1.0, 0.0).astype(BF16)))


def fox_row_norms(p_fx, tm=SCAN_ROWS):
    m = p_fx.shape[0]
    tm = min(tm, m)
    return pl.pallas_call(
        _fox_norm_body,
        out_shape=jax.ShapeDtypeStruct((m, LANES), F32),
        grid=(m // tm,),
        in_specs=[pl.BlockSpec((tm, BRANCH), lambda i: (i, 0)), pl.BlockSpec((tm, BRANCH), lambda i: (i, 1))],
        out_specs=pl.BlockSpec((tm, LANES), lambda i: (i, 0)),
        compiler_params=_cp(("parallel",)),
        name="fox_row_norms",
    )(p_fx, p_fx)


def _fox_bounds_body(nsq_ref, cum_ref, o_ref, *, tq, tk):
    seq = nsq_ref.shape[0]
    nq, nk = seq // tq, seq // tk
    nsq = nsq_ref[...]
    qn = jnp.sqrt(jnp.max(nsq.reshape(nq, tq, LANES), axis=1))
    kn = jnp.sqrt(jnp.max(nsq, axis=0, keepdims=True))
    kn = pltpu.roll(kn, LANES - FX_HEADS, 1)
    cum = cum_ref[...]
    cq_first = cum.reshape(nq, tq, LANES)[:, 0, :]
    ck_last = cum.reshape(nk, tk, LANES)[:, tk - 1, :]
    thr = cq_first + 2.0 * FOX_NORM_SLACK * qn * kn + FOX_SKIP_MARGIN
    start = jnp.zeros((nq, LANES), I32)
    for j in range(nk):
        start = start + jnp.where(ck_last[j:j + 1, :] > thr, 1, 0)
    o_ref[0] = start


def fox_first_blocks(nsq, cum, batch, seq, tq, tk):
    return pl.pallas_call(
        functools.partial(_fox_bounds_body, tq=tq, tk=tk),
        out_shape=jax.ShapeDtypeStruct((batch, seq // tq, LANES), I32),
        grid=(batch,),
        in_specs=[pl.BlockSpec((seq, LANES), lambda b: (b, 0)), pl.BlockSpec((seq, LANES), lambda b: (b, 0))],
        out_specs=pl.BlockSpec((1, seq // tq, LANES), lambda b: (b, 0, 0)),
        compiler_params=_cp(("parallel",)),
        name="fox_first_blocks",
    )(nsq, cum)


def _fox_body(first_ref, q_ref, qb_ref, k_ref, kb_ref, v_ref, o_ref, m_ref, l_ref, acc_ref, *, tq, tk, nq):
    b = pl.program_id(0)
    i = pl.program_id(1)
    j_diag = (i * tq + tq - 1) // tk
    m_ref[...] = jnp.full_like(m_ref, NEG_BIG)
    l_ref[...] = jnp.zeros_like(l_ref)
    acc_ref[...] = jnp.zeros_like(acc_ref)

    def head_block(h, j, masked):
        hs = slice(h * FX_D, (h + 1) * FX_D)
        ksl = pl.ds(pl.multiple_of(j * tk, tk), tk)
        qa = jnp.concatenate([q_ref[:, hs], qb_ref[:, hs]], axis=1)
        ka = jnp.concatenate([k_ref[ksl, hs], kb_ref[ksl, hs]], axis=1)
        s = _dot_nt(qa, ka)
        if masked:
            t_glob = i * tq + lax.broadcasted_iota(I32, (tq, 1), 0)
            s_glob = j * tk + lax.broadcasted_iota(I32, (1, tk), 1)
            s = jnp.where(s_glob <= t_glob, s, -jnp.inf)
        m_prev = m_ref[h][:, 0:1]
        m_new = jnp.maximum(m_prev, jnp.max(s, axis=1, keepdims=True))
        alpha = jnp.exp2(m_prev - m_new)
        p = jnp.exp2(s - m_new)
        l_new = alpha * l_ref[h][:, 0:1] + jnp.sum(p, axis=1, keepdims=True)
        acc_ref[h] = alpha * acc_ref[h] + _dot(p.astype(BF16), v_ref[ksl, hs])
        m_ref[h] = jnp.broadcast_to(m_new, (tq, LANES))
        l_ref[h] = jnp.broadcast_to(l_new, (tq, LANES))

    base = (b * nq + i) * FX_HEADS
    for h0 in range(0, FX_HEADS, 2):
        j0 = jnp.minimum(first_ref[base + h0], first_ref[base + h0 + 1])

        def body(j, carry, h0=h0):
            head_block(h0, j, False)
            head_block(h0 + 1, j, False)
            return carry

        lax.fori_loop(j0, j_diag, body, 0)
        head_block(h0, j_diag, True)
        head_block(h0 + 1, j_diag, True)

    for h in range(FX_HEADS):
        o_ref[:, h * FX_D:(h + 1) * FX_D] = (acc_ref[h] / l_ref[h][:, 0:1]).astype(o_ref.dtype)


def fox_attention(p_fx, qb, kb, first_blocks, batch, seq, tq=ATTN_Q_ROWS, tk=ATTN_KEYS):
    tq, tk = min(tq, seq), min(tk, seq)
    assert tk % tq == 0
    nq = seq // tq
    once = pl.Buffered(1)
    qspec = lambda colblk: pl.BlockSpec((tq, BRANCH), lambda b, i, f: (b * nq + i, colblk))
    kspec = lambda colblk: pl.BlockSpec((seq, BRANCH), lambda b, i, f: (b, colblk), pipeline_mode=once)
    return pl.pallas_call(
        functools.partial(_fox_body, tq=tq, tk=tk, nq=nq),
        out_shape=jax.ShapeDtypeStruct((batch * seq, BRANCH), BF16),
        grid_spec=pltpu.PrefetchScalarGridSpec(
            num_scalar_prefetch=1,
            grid=(batch, nq),
            in_specs=[qspec(0), qspec(0), kspec(1), kspec(0), kspec(2)],
            out_specs=pl.BlockSpec((tq, BRANCH), lambda b, i, f: (b * nq + i, 0)),
            scratch_shapes=[pltpu.VMEM((FX_HEADS, tq, LANES), F32),
                            pltpu.VMEM((FX_HEADS, tq, LANES), F32),
                            pltpu.VMEM((FX_HEADS, tq, FX_D), F32)]),
        compiler_params=_cp(("parallel", "arbitrary")),
        name="fox_attention",
    )(first_blocks, p_fx, qb, p_fx, kb, p_fx)


def fox_mixer(p_fx, p_misc, f_bias, batch, seq, tq=ATTN_Q_ROWS, tk=ATTN_KEYS):
    tq, tk = min(tq, seq), min(tk, seq)
    qb, kb, cum = fox_bias_columns(p_misc, f_bias, batch, seq)
    first = fox_first_blocks(fox_row_norms(p_fx), cum, batch, seq, tq, tk)
    first = first[:, :, :FX_HEADS].reshape(-1)
    return fox_attention(p_fx, qb, kb, first, batch, seq, tq, tk)


def _qlat_body(q_ref, wuk_ref, o_ref, *, tq):
    r = (_dot_nt(wuk_ref[...], q_ref[...]) * (DS_HEAD_DIM ** -0.5 * LOG2E)).astype(o_ref.dtype)
    for g in range(o_ref.shape[0]):
        o_ref[g] = r[:, g * tq:(g + 1) * tq]


def dsa_qlat(qcat, w_uk, tq, tm=PROJ_ROWS):
    m = qcat.shape[0]
    tm = min(tm, m)
    return pl.pallas_call(
        functools.partial(_qlat_body, tq=tq),
        out_shape=jax.ShapeDtypeStruct((m // tq, DS_KV_LORA, DS_HEADS * tq), BF16),
        grid=(m // tm, DS_HEADS),
        in_specs=[pl.BlockSpec((tm, DS_HEAD_DIM), lambda i, h: (i, h)),
                  pl.BlockSpec((DS_KV_LORA, DS_HEAD_DIM), lambda i, h: (0, h))],
        out_specs=pl.BlockSpec((tm // tq, DS_KV_LORA, tq), lambda i, h: (i, 0, h)),
        compiler_params=_cp(("parallel", "arbitrary")),
        name="dsa_qlat",
    )(qcat, w_uk)


def _dsa_body(qi_ref, misc_ref, ql_ref, kidx_ref, ckv_ref, ckvt_ref, wuv_ref, o_ref,
              keys_ref, thr_ref, s_ref, p_ref, m_ref, l_ref, alpha_ref, acc_ref, *, tq, tk, topk):
    seq_len = kidx_ref.shape[0]
    i = pl.program_id(1)
    nh = DS_HEADS
    cols = nh * tq
    nkb = (i * tq + tq - 1) // tk + 1
    t_glob = i * tq + lax.broadcasted_iota(I32, (1, tq), 1)
    krow = lax.broadcasted_iota(I32, (tk, 1), 0)

    hsel = (lax.broadcasted_iota(I32, (DS_IDX_HEADS, LANES), 1)
            == lax.broadcasted_iota(I32, (DS_IDX_HEADS, LANES), 0) + DS_IDX_DIM).astype(F32)
    w_t = lax.dot_general(hsel, misc_ref[...], (((1,), (1,)), ((), ())), preferred_element_type=F32,
                          precision=HIGHEST) * (DS_IDX_HEADS ** -0.5 * DS_IDX_DIM ** -0.5)
    qh = [qi_ref[:, h * DS_IDX_DIM:(h + 1) * DS_IDX_DIM] for h in range(DS_IDX_HEADS)]

    def score_body(j, carry):
        kb = kidx_ref[pl.ds(pl.multiple_of(j * tk, tk), tk), :]
        sc = jnp.zeros((tk, tq), F32)
        for h in range(DS_IDX_HEADS):
            sc = sc + jnp.maximum(_dot_nt(kb, qh[h]), 0.0) * w_t[h:h + 1, :]
        sc = jnp.where(sc == 0.0, 0.0, sc)
        sc = jnp.where(j * tk + krow <= t_glob, sc, -jnp.inf)
        bits = pltpu.bitcast(sc, I32)
        keys_ref[j] = jnp.where(bits < 0, bits ^ 0x7FFFFFFF, bits)
        return carry

    lax.fori_loop(0, nkb, score_body, 0)

    def count_rows(hit):
        return jnp.sum(jnp.where(hit, 1.0, 0.0).reshape(tk // 8, 8, tq), axis=0)

    def count_ge(cand):
        def body(j, acc):
            return acc + count_rows(keys_ref[j] >= cand)
        acc = lax.fori_loop(0, nkb, body, jnp.zeros((8, tq), F32))
        return jnp.sum(acc, axis=0, keepdims=True)

    def top2_body(j, carry):
        m1, m2 = carry
        for g in range(tk // LANES):
            x = keys_ref[j, g * LANES:(g + 1) * LANES, :]
            m2 = jnp.maximum(m2, jnp.minimum(m1, x))
            m1 = jnp.maximum(m1, x)
        return m1, m2

    init = jnp.full((LANES, tq), INT_MIN, I32)
    m1, m2 = lax.fori_loop(0, nkb, top2_body, (init, init))
    lo = jnp.min(m2, axis=0, keepdims=True)
    hi = jnp.max(m1, axis=0, keepdims=True)

    kf = float(topk)

    def bisect(lo, hi, cnt):
        mid = (lo | hi) - ((lo ^ hi) >> 1)
        c = count_ge(mid)
        take = c >= kf
        return jnp.where(take, mid, lo), jnp.where(take, hi, mid - 1), jnp.where(take, c, cnt)

    passes_per_round = 3

    def search_cond(carry):
        lo, hi, cnt = carry
        return jnp.max(jnp.where(lo < hi, cnt, kf)) > kf

    def search_round(carry):
        lo, hi, cnt = carry
        for _ in range(passes_per_round):
            lo, hi, cnt = bisect(lo, hi, cnt)
        return lo, hi, cnt

    unknown = jnp.full((1, tq), 2.0 * kf + float(seq_len), F32)
    thr, _, cnt = lax.while_loop(search_cond, search_round, (lo, hi, unknown))
    thr_ref[...] = thr

    @pl.when(jnp.max(cnt) > kf)
    def _():
        def count_tied(x):
            def body(j, acc):
                return acc + count_rows((keys_ref[j] == thr) & (j * tk + krow <= x))
            acc = lax.fori_loop(0, nkb, body, jnp.zeros((8, tq), F32))
            return jnp.sum(acc, axis=0, keepdims=True)

        need = kf - count_ge(thr + 1)

        def cut_bit(bi, t):
            cand = t + lax.shift_left(jnp.int32(1), bi)
            return jnp.where(count_tied(cand - 1) < need, cand, t)

        nbits = max(1, (seq_len - 1).bit_length())
        cut = lax.fori_loop(0, nbits, lambda r, t: cut_bit(nbits - 1 - r, t), jnp.zeros((1, tq), I32))

        def demote(j, carry):
            kk = keys_ref[j]
            keys_ref[j] = jnp.where((kk == thr) & (j * tk + krow > cut), thr - 1, kk)
            return carry

        lax.fori_loop(0, nkb, demote, 0)

    m_ref[...] = jnp.full_like(m_ref, NEG_BIG)
    l_ref[...] = jnp.zeros_like(l_ref)
    acc_ref[...] = jnp.zeros_like(acc_ref)

    def attn_body(j, carry):
        ksl = pl.ds(pl.multiple_of(j * tk, tk), tk)
        s_ref[...] = _dot(ckv_ref[ksl, :], ql_ref[0])
        for c in range(cols // LANES):
            lsl = slice(c * LANES, (c + 1) * LANES)
            tsl = slice((c * LANES) % tq, (c * LANES) % tq + LANES)
            sel = (keys_ref[j, :, tsl] >= thr_ref[:, tsl]) & (j * tk + krow <= t_glob[:, tsl])
            s = jnp.where(sel, s_ref[:, lsl], -jnp.inf)
            m_prev = m_ref[:, lsl]
            m_new = jnp.maximum(m_prev, jnp.max(s, axis=0, keepdims=True))
            alpha = jnp.exp2(m_prev - m_new)
            p = jnp.exp2(s - m_new)
            l_ref[:, lsl] = alpha * l_ref[:, lsl] + jnp.sum(p, axis=0, keepdims=True)
            m_ref[:, lsl] = m_new
            alpha_ref[:, lsl] = alpha
            p_ref[:, lsl] = p.astype(BF16)
        pv = _dot(ckvt_ref[j], p_ref[...])
        acc_ref[...] = alpha_ref[...] * acc_ref[...] + pv
        return carry

    lax.fori_loop(0, nkb, attn_body, 0)

    for h in range(nh):
        hsl = slice(h * tq, (h + 1) * tq)
        o_lat = (acc_ref[:, hsl] / l_ref[:, hsl]).astype(BF16)
        o_ref[:, h * DS_HEAD_DIM:(h + 1) * DS_HEAD_DIM] = _dot_tn(
            o_lat, wuv_ref[:, h * DS_HEAD_DIM:(h + 1) * DS_HEAD_DIM]).astype(o_ref.dtype)


def dsa_attention(qcat, p_misc, q_lat, k_idx, c_kv, w_uv, batch, seq, tq=ATTN_Q_ROWS, tk=ATTN_KEYS):
    tq, tk = min(tq, seq), min(tk, seq)
    nq, nk = seq // tq, seq // tk
    cols = DS_HEADS * tq
    topk = min(DS_TOPK_MAX, seq // 4)
    assert tk >= topk and topk <= 2 * LANES and tq % LANES == 0 and tk % LANES == 0
    c_kv_t = c_kv.reshape(batch, nk, tk, DS_KV_LORA).transpose(0, 1, 3, 2).reshape(batch * nk, DS_KV_LORA, tk)
    return pl.pallas_call(
        functools.partial(_dsa_body, tq=tq, tk=tk, topk=topk),
        out_shape=jax.ShapeDtypeStruct((batch * seq, BRANCH), BF16),
        grid=(batch, nq),
        in_specs=[pl.BlockSpec((tq, DS_IDX_HEADS * DS_IDX_DIM), lambda b, i: (b * nq + i, 1)),
                  pl.BlockSpec((tq, LANES), lambda b, i: (b * nq + i, MISC_IDX // LANES)),
                  pl.BlockSpec((1, DS_KV_LORA, cols), lambda b, i: (b * nq + i, 0, 0)),
                  pl.BlockSpec((seq, DS_IDX_DIM), lambda b, i: (b, 0)),
                  pl.BlockSpec((seq, DS_KV_LORA), lambda b, i: (b, 0)),
                  pl.BlockSpec((nk, DS_KV_LORA, tk), lambda b, i: (b, 0, 0)),
                  pl.BlockSpec((DS_KV_LORA, BRANCH), lambda b, i: (0, 0))],
        out_specs=pl.BlockSpec((tq, BRANCH), lambda b, i: (b * nq + i, 0)),
        scratch_shapes=[pltpu.VMEM((nk, tk, tq), I32),
                        pltpu.VMEM((1, tq), I32),
                        pltpu.VMEM((tk, cols), F32),
                        pltpu.VMEM((tk, cols), BF16),
                        pltpu.VMEM((1, cols), F32),
                        pltpu.VMEM((1, cols), F32),
                        pltpu.VMEM((1, cols), F32),
                        pltpu.VMEM((DS_KV_LORA, cols), F32)],
        compiler_params=_cp(("parallel", "arbitrary"), VMEM_BIG_MB),
        name="dsa_attention",
    )(qcat, p_misc, q_lat, k_idx, c_kv, c_kv_t, w_uv)


def _merge_body(h_ref, wg_ref, wb_ref, ya_ref, yb_ref, yc_ref, yd_ref, o_ref, acc_ref):
    n = pl.program_id(2)
    gate = jax.nn.sigmoid(_dot(h_ref[...], wg_ref[0].astype(BF16)))
    wb = wb_ref[0].astype(BF16)
    for idx, y_ref in enumerate((ya_ref, yb_ref, yc_ref, yd_ref)):
        @pl.when(n == idx)
        def _(y_ref=y_ref, idx=idx):
            contrib = gate * _dot(y_ref[...], wb)
            if idx == 0:
                acc_ref[...] = contrib
            else:
                acc_ref[...] += contrib

    @pl.when(n == pl.num_programs(2) - 1)
    def _():
        o_ref[...] = acc_ref[...].astype(o_ref.dtype)


def gated_merge(h, w_gate, w_branch, layer, ys, tm=MERGE_ROWS, tn=MERGE_COLS):
    m = h.shape[0]
    tm = min(tm, m)
    ymap = lambda i, j, n: (i, 0)
    once = pl.Buffered(1)
    return pl.pallas_call(
        _merge_body,
        out_shape=jax.ShapeDtypeStruct((m, D_MODEL), BF16),
        grid=(m // tm, D_MODEL // tn, 4),
        in_specs=[pl.BlockSpec((tm, D_MODEL), lambda i, j, n: (i, 0), pipeline_mode=once),
                  pl.BlockSpec((None, 1, D_MODEL, tn), lambda i, j, n: (layer, n, 0, j)),
                  pl.BlockSpec((None, 1, BRANCH, tn), lambda i, j, n: (layer, n, 0, j))]
                 + [pl.BlockSpec((tm, BRANCH), ymap, pipeline_mode=once) for _ in range(4)],
        out_specs=pl.BlockSpec((tm, tn), lambda i, j, n: (i, j)),
        scratch_shapes=[pltpu.VMEM((tm, tn), F32)],
        compiler_params=_cp(("parallel", "parallel", "arbitrary"), VMEM_BIG_MB),
        name="gated_merge",
    )(h, w_gate, w_branch, *ys)


def _ffn_up_body(h_ref, halo_ref, wg_ref, wu_ref, cw_ref, o_ref, *, tiles_per_seq):
    i = pl.program_id(0)
    wg = wg_ref[...].astype(BF16)
    g = _dot(h_ref[...], wg)
    u = _dot(h_ref[...], wu_ref[...].astype(BF16))
    gh = _dot(halo_ref[...], wg)
    gh = gh * jnp.where(i % tiles_per_seq == 0, 0.0, 1.0)
    cw = cw_ref[...]
    row8 = lax.broadcasted_iota(I32, (8, 1), 0)
    y = g * cw[FFN_CONV - 1:FFN_CONV, :]
    for j in range(1, FFN_CONV):
        rolled = pltpu.roll(g, j, 0)
        head = jnp.where(row8 < j, pltpu.roll(gh, j, 0), rolled[0:8])
        shifted = jnp.concatenate([head, rolled[8:]], axis=0)
        y = y + shifted * cw[FFN_CONV - 1 - j:FFN_CONV - j, :]
    o_ref[...] = (_silu(y) * u).astype(o_ref.dtype)


def ffn_up(h, w_gate, w_up, conv_w, layer, seq, tm=FFN_ROWS, tn=FFN_COLS):
    m = h.shape[0]
    tm = min(tm, seq)
    dff = w_gate.shape[-1]
    wspec = pl.BlockSpec((None, D_MODEL, tn), lambda i, j: (layer, 0, j))
    return pl.pallas_call(
        functools.partial(_ffn_up_body, tiles_per_seq=seq // tm),
        out_shape=jax.ShapeDtypeStruct((m, dff), BF16),
        grid=(m // tm, dff // tn),
        in_specs=[pl.BlockSpec((tm, D_MODEL), lambda i, j: (i, 0), pipeline_mode=pl.Buffered(1)),
                  pl.BlockSpec((8, D_MODEL), lambda i, j: (jnp.maximum(i * (tm // 8) - 1, 0), 0)),
                  wspec, wspec,
                  pl.BlockSpec((None, FFN_CONV, tn), lambda i, j: (layer, 0, j))],
        out_specs=pl.BlockSpec((tm, tn), lambda i, j: (i, j)),
        compiler_params=_cp(("parallel", "arbitrary")),
        name="ffn_up",
    )(h, h, w_gate, w_up, conv_w.astype(F32))


def _in_proj_weights(w, l):
    o = 4 * BRANCH
    secs = {}
    for name, width in (("cq", DS_Q_LORA), ("ckv", DS_KV_LORA), ("kidx", DS_IDX_DIM), ("widx", DS_IDX_HEADS),
                        ("z", BRANCH), ("xbc", MB_CONV_DIM), ("dt", MB_HEADS), ("fx", 3 * BRANCH),
                        ("ff", FX_HEADS)):
        secs[name] = w[l, :, o:o + width]
        o += width
    zeros = lambda n: jnp.zeros((w.shape[1], n), w.dtype)
    misc = jnp.concatenate([secs["ckv"], secs["kidx"], secs["widx"], zeros(LANES - DS_IDX_DIM - DS_IDX_HEADS),
                            secs["dt"], zeros(LANES - MB_HEADS), secs["ff"], zeros(LANES - FX_HEADS)], axis=1)
    return dict(cq=secs["cq"], misc=misc, z=secs["z"], xbc=secs["xbc"], fx=secs["fx"])


def _layer(x, l, batch, seq, prm):
    h = rmsnorm(x, prm["attn_norm"][l], BF16)
    w = _in_proj_weights(prm["w_in_bf"], l)
    p_hg = matmul(h, prm["w_in_bf"], F32, n=4 * BRANCH, b_lead=(l,))
    p_cq = matmul(h, w["cq"], F32, tn=DS_Q_LORA)
    p_misc = matmul(h, w["misc"], F32, tn=MISC_W)
    p_z = matmul(h, w["z"], F32)
    p_xbc = matmul(h, w["xbc"], F32, tn=MB_CONV_DIM // 2)
    fx_scale = jnp.concatenate([jnp.full((1, BRANCH), FX_D ** -0.5 * LOG2E, F32), jnp.ones((1, 2 * BRANCH), F32)], 1)
    p_fx = matmul(h, w["fx"], BF16, col_scale=fx_scale)

    y_a = hgrn2(p_hg, prm["hgrn_lb_logits"], prm["hgrn_norm"][l], l, batch, seq)

    c_q = rmsnorm(p_cq, prm["dsa_q_norm"][l], BF16)
    c_kv = rmsnorm(p_misc, prm["dsa_kv_norm"][l], BF16, width=DS_KV_LORA, col_block=0)
    k_idx = p_misc[:, MISC_IDX:MISC_IDX + DS_IDX_DIM].astype(BF16)
    w_q = jnp.concatenate([prm["dsa_w_uq"][l].astype(BF16), prm["dsa_w_iq"][l].astype(BF16)], axis=1)
    qcat = matmul(c_q, w_q, BF16)
    dsa_tq = min(ATTN_Q_ROWS, seq)
    q_lat = dsa_qlat(qcat, prm["dsa_w_uk"][l].astype(BF16), dsa_tq)
    y_b = dsa_attention(qcat, p_misc, q_lat, k_idx, c_kv, prm["dsa_w_uv"][l].astype(BF16), batch, seq,
                        tq=dsa_tq)

    y_c = mamba2(p_z, p_xbc, p_misc, prm["ssm_conv_w"][l], prm["ssm_conv_b"][l], prm["ssm_dt_bias"][l],
                 prm["ssm_a_log"][l], prm["ssm_d"][l], prm["ssm_norm"][l], batch, seq)

    y_d = fox_mixer(p_fx, p_misc, prm["fox_f_bias"][l], batch, seq)

    merged = gated_merge(h, prm["w_gate"], prm["w_branch"], l, (y_a, y_b, y_c, y_d))
    x = matmul(merged, prm["w_out_bf"], F32, residual=x, b_lead=(l,), vmem_mb=VMEM_BIG_MB)

    h2 = rmsnorm(x, prm["ffn_norm"][l], BF16)
    act = ffn_up(h2, prm["ffn_w_gate"], prm["ffn_w_up"], prm["ffn_conv"], l, seq)
    return matmul(act, prm["w_down_bf"], F32, DOWN_ROWS, DOWN_COLS, residual=x, b_lead=(l,), vmem_mb=VMEM_BIG_MB)


def kernel(x, attn_norm, ffn_norm, final_norm, w_in, hgrn_lb_logits, hgrn_norm, dsa_q_norm, dsa_kv_norm,
           dsa_w_uq, dsa_w_iq, dsa_w_uk, dsa_w_uv, ssm_conv_w, ssm_conv_b, ssm_dt_bias, ssm_a_log, ssm_d,
           ssm_norm, fox_f_bias, w_gate, w_branch, w_out, ffn_w_gate, ffn_w_up, ffn_conv, ffn_w_down):
    batch, seq, d = x.shape
    prm = dict(attn_norm=attn_norm, ffn_norm=ffn_norm, w_in=w_in, hgrn_lb_logits=hgrn_lb_logits,
               hgrn_norm=hgrn_norm, dsa_q_norm=dsa_q_norm, dsa_kv_norm=dsa_kv_norm, dsa_w_uq=dsa_w_uq,
               dsa_w_iq=dsa_w_iq, dsa_w_uk=dsa_w_uk, dsa_w_uv=dsa_w_uv, ssm_conv_w=ssm_conv_w,
               ssm_conv_b=ssm_conv_b, ssm_dt_bias=ssm_dt_bias, ssm_a_log=ssm_a_log, ssm_d=ssm_d,
               ssm_norm=ssm_norm, fox_f_bias=fox_f_bias, w_gate=w_gate, w_branch=w_branch, w_out=w_out,
               ffn_w_gate=ffn_w_gate, ffn_w_up=ffn_w_up, ffn_conv=ffn_conv, ffn_w_down=ffn_w_down)
    prm.update(w_in_bf=w_in.astype(BF16), w_out_bf=w_out.astype(BF16), w_down_bf=ffn_w_down.astype(BF16))
    xf = x.reshape(batch * seq, d)
    for l in range(DEPTH):
        xf = _layer(xf, l, batch, seq, prm)
    return rmsnorm(xf, final_norm, x.dtype).reshape(batch, seq, d)
```

```python
import functools

import jax
import jax.numpy as jnp
from jax import lax
from jax.experimental import pallas as pl
from jax.experimental.pallas import tpu as pltpu

F32, BF16, I32 = jnp.float32, jnp.bfloat16, jnp.int32
HIGHEST = lax.Precision.HIGHEST

D_MODEL = 4096
DEPTH = 2
BRANCH = 1024
HG_HEADS, HG_D, HG_CHUNK, HG_SUB = 8, 128, 64, 16
DS_HEADS, DS_HEAD_DIM, DS_Q_LORA, DS_KV_LORA = 8, 128, 768, 512
DS_IDX_HEADS, DS_IDX_DIM, DS_TOPK_MAX = 16, 64, 256
MB_HEADS, MB_P, MB_N, MB_GROUPS, MB_CONV, MB_CHUNK = 16, 64, 128, 2, 4, 128
MB_CONV_DIM = BRANCH + 2 * MB_GROUPS * MB_N
FX_HEADS, FX_D = 8, 128
D_FF = 11008
FFN_CONV = 3
EPS = 1e-6
LANES = 128
NEG_BIG = -1e30
LOG2E = 1.4426950408889634
FOX_SKIP_MARGIN = 160.0
FOX_NORM_SLACK = 1.02
INT_MIN = -(2 ** 31)

MISC_CKV = 0
MISC_IDX = 512
MISC_DT = 640
MISC_FF = 768
MISC_W = 896


VMEM_MB, VMEM_BIG_MB = 48, 56
PROJ_ROWS, PROJ_COLS = 1024, 1024
DOWN_ROWS, DOWN_COLS = 512, 512
MERGE_ROWS, MERGE_COLS = 1024, 512
FFN_ROWS, FFN_COLS = 2048, 256
ATTN_Q_ROWS, ATTN_KEYS = 256, 512
HG_ROWS, HG_HEADS_PER_STEP = 256, 8
NORM_ROWS, SCAN_ROWS = 256, 512


def _cp(sem, vmem_mb=VMEM_MB):
    return pltpu.CompilerParams(dimension_semantics=sem, vmem_limit_bytes=vmem_mb * 2 ** 20)


def _dot(a, b, precision=None):
    return jnp.dot(a, b, preferred_element_type=F32, precision=precision)


def _dot_nt(a, b):
    return lax.dot_general(a, b, (((1,), (1,)), ((), ())), preferred_element_type=F32)


def _dot_tn(a, b):
    return lax.dot_general(a, b, (((0,), (0,)), ((), ())), preferred_element_type=F32)


def _tril(n):
    r = lax.broadcasted_iota(I32, (n, n), 0)
    c = lax.broadcasted_iota(I32, (n, n), 1)
    return (r >= c).astype(F32)


def _silu(x):
    return x * jax.nn.sigmoid(x)


def _rmsnorm_body(x_ref, g_ref, o_ref):
    x = x_ref[...].astype(F32)
    ms = jnp.mean(x * x, axis=-1, keepdims=True)
    o_ref[...] = (x * lax.rsqrt(ms + EPS) * g_ref[...]).astype(o_ref.dtype)


def rmsnorm(x, gain, out_dtype, width=None, col_block=0, tm=NORM_ROWS):
    m = x.shape[0]
    width = x.shape[1] if width is None else width
    return pl.pallas_call(
        _rmsnorm_body,
        out_shape=jax.ShapeDtypeStruct((m, width), out_dtype),
        grid=(m // tm,),
        in_specs=[pl.BlockSpec((tm, width), lambda i: (i, col_block)),
                  pl.BlockSpec((1, width), lambda i: (0, 0))],
        out_specs=pl.BlockSpec((tm, width), lambda i: (i, 0)),
        compiler_params=_cp(("parallel",)),
        name="rmsnorm",
    )(x, gain.reshape(1, width).astype(F32))


def _mm_body(a_ref, b_ref, o_ref):
    o_ref[...] = _dot(a_ref[...], b_ref[...].astype(BF16)).astype(o_ref.dtype)


def _mm_add_body(a_ref, b_ref, r_ref, o_ref):
    o_ref[...] = (r_ref[...] + _dot(a_ref[...], b_ref[...].astype(BF16))).astype(o_ref.dtype)


def _mm_scale_body(a_ref, b_ref, s_ref, o_ref):
    o_ref[...] = (_dot(a_ref[...], b_ref[...].astype(BF16)) * s_ref[...]).astype(o_ref.dtype)


def matmul(a, b, out_dtype, tm=PROJ_ROWS, tn=PROJ_COLS, residual=None, col_scale=None, n=None, b_lead=(),
           vmem_mb=VMEM_MB):
    m, k = a.shape
    n = b.shape[-1] if n is None else n
    tm, tn = min(tm, m), min(tn, n)
    in_specs = [pl.BlockSpec((tm, k), lambda i, j: (i, 0)),
                pl.BlockSpec((None,) * len(b_lead) + (k, tn), lambda i, j: tuple(b_lead) + (0, j))]
    args, body = (a, b), _mm_body
    if residual is not None:
        in_specs.append(pl.BlockSpec((tm, tn), lambda i, j: (i, j)))
        args, body = (a, b, residual), _mm_add_body
    elif col_scale is not None:
        in_specs.append(pl.BlockSpec((1, tn), lambda i, j: (0, j)))
        args, body = (a, b, col_scale), _mm_scale_body
    return pl.pallas_call(
        body,
        out_shape=jax.ShapeDtypeStruct((m, n), out_dtype),
        grid=(m // tm, n // tn),
        in_specs=in_specs,
        out_specs=pl.BlockSpec((tm, tn), lambda i, j: (i, j)),
        compiler_params=_cp(("parallel", "arbitrary"), vmem_mb),
        name="matmul",
    )(*args)


def _hgrn_body(lbl_ref, gain_ref, q_ref, f_ref, i_ref, g_ref, o_ref, st_ref, *, layer, nchunks, hpb):
    c = pl.program_id(2)

    @pl.when(c == 0)
    def _():
        st_ref[...] = jnp.zeros_like(st_ref)

    logits = lbl_ref[...]
    e = jnp.exp(logits - jnp.max(logits, axis=0, keepdims=True))
    p = e / jnp.sum(e, axis=0, keepdims=True)
    lb_all = jnp.sum(p[0:layer + 1], axis=0, keepdims=True) - p[0:1]
    gain_all = gain_ref[...]
    tril = _tril(HG_CHUNK)
    row = lax.broadcasted_iota(I32, (HG_CHUNK, 1), 0)
    row_in_sub = row % HG_SUB
    nsub = HG_CHUNK // HG_SUB

    def chunk(ci, carry):
        sl = pl.ds(pl.multiple_of(ci * HG_CHUNK, HG_CHUNK), HG_CHUNK)
        for hh in range(hpb):
            hs = slice(hh * HG_D, (hh + 1) * HG_D)
            lb, gain = lb_all[:, hs], gain_all[:, hs]
            q = _silu(q_ref[sl, hs])
            v = _silu(i_ref[sl, hs])
            f = lb + (1.0 - lb) * jax.nn.sigmoid(f_ref[sl, hs])
            k = 1.0 - f
            b = _dot(tril, jnp.log(f) * LOG2E, HIGHEST)
            st = st_ref[hh]

            o = _dot_nt((q * jnp.exp2(b)).astype(BF16), st.astype(BF16))

            intra = jnp.zeros((HG_CHUNK, HG_D), F32)
            for d in range(HG_SUB):
                ks = k if d == 0 else pltpu.roll(k, d, 0)
                bs = b if d == 0 else pltpu.roll(b, d, 0)
                vs = v if d == 0 else pltpu.roll(v, d, 0)
                expo = jnp.where(row_in_sub >= d, b - bs, -jnp.inf)
                w = jnp.sum(q * ks * jnp.exp2(expo), axis=-1, keepdims=True)
                intra = intra + w * vs
            o = o + intra

            parts = [jnp.zeros((HG_SUB, HG_D), F32)]
            for si in range(1, nsub):
                lo = si * HG_SUB
                r = b[lo - 1:lo, :]
                qi = (q[lo:lo + HG_SUB] * jnp.exp2(b[lo:lo + HG_SUB] - r)).astype(BF16)
                kj = (k[0:lo] * jnp.exp2(r - b[0:lo])).astype(BF16)
                sc = _dot_nt(qi, kj)
                parts.append(_dot(sc.astype(BF16), v[0:lo].astype(BF16)))
            o = o + jnp.concatenate(parts, axis=0)

            b_last = b[HG_CHUNK - 1:HG_CHUNK, :]
            kd = (k * jnp.exp2(b_last - b)).astype(BF16)
            st_ref[hh] = st * jnp.exp2(b_last) + _dot_tn(v.astype(BF16), kd)

            og = o * jax.nn.sigmoid(g_ref[sl, hs])
            ms = jnp.mean(og * og, axis=-1, keepdims=True)
            o_ref[sl, hs] = (og * lax.rsqrt(ms + EPS) * gain).astype(o_ref.dtype)
        return carry

    lax.fori_loop(0, nchunks, chunk, 0, unroll=True)


def hgrn2(p_hg, lb_logits, norm_gain, layer, batch, seq, tt=HG_ROWS, hpb=HG_HEADS_PER_STEP):
    tt = min(tt, seq)
    nt = seq // tt
    hb = HG_HEADS // hpb
    w = hpb * HG_D

    def col(sec):
        return lambda b, h, c: (b * nt + c, sec * hb + h)

    return pl.pallas_call(
        functools.partial(_hgrn_body, layer=layer, nchunks=tt // HG_CHUNK, hpb=hpb),
        out_shape=jax.ShapeDtypeStruct((batch * seq, BRANCH), BF16),
        grid=(batch, hb, nt),
        in_specs=[pl.BlockSpec((DEPTH, w), lambda b, h, c: (0, h)),
                  pl.BlockSpec((1, w), lambda b, h, c: (0, h)),
                  pl.BlockSpec((tt, w), col(0)),
                  pl.BlockSpec((tt, w), col(1)),
                  pl.BlockSpec((tt, w), col(2)),
                  pl.BlockSpec((tt, w), col(3))],
        out_specs=pl.BlockSpec((tt, w), lambda b, h, c: (b * nt + c, h)),
        scratch_shapes=[pltpu.VMEM((hpb, HG_D, HG_D), F32)],
        compiler_params=_cp(("parallel", "parallel", "arbitrary")),
        name="hgrn2",
    )(lb_logits.astype(F32), norm_gain.reshape(1, BRANCH).astype(F32), p_hg, p_hg, p_hg, p_hg)


def _mamba_body(z_ref, xbc_ref, dt_ref, cw_ref, cb_ref, dtb_ref, alog_ref, dsk_ref, gain_ref,
                o_ref, prev_ref, st_ref):
    c = pl.program_id(1)
    L = MB_CHUNK

    @pl.when(c == 0)
    def _():
        prev_ref[...] = jnp.zeros_like(prev_ref)
        st_ref[...] = jnp.zeros_like(st_ref)

    x = xbc_ref[...]
    prev = prev_ref[...]
    row = lax.broadcasted_iota(I32, (L, 1), 0)
    cw = cw_ref[...]
    acc = x * cw[MB_CONV - 1:MB_CONV, :] + cb_ref[...]
    for j in range(1, MB_CONV):
        sh = jnp.where(row >= j, pltpu.roll(x, j, 0), pltpu.roll(prev, j, 0))
        acc = acc + sh * cw[MB_CONV - 1 - j:MB_CONV - j, :]
    prev_ref[...] = x
    xbc = _silu(acc)
    xs = xbc[:, 0:BRANCH]
    bm = xbc[:, BRANCH:BRANCH + MB_GROUPS * MB_N]
    cm = xbc[:, BRANCH + MB_GROUPS * MB_N:MB_CONV_DIM]

    raw = dt_ref[...] + dtb_ref[...]
    dt = jnp.maximum(raw, 0.0) + jnp.log1p(jnp.exp(-jnp.abs(raw)))
    a = -jnp.exp(alog_ref[...]) * dt
    tril = _tril(L)
    a_cs = _dot(tril, a, HIGHEST)
    a_cs_t = a_cs.T
    causal = tril > 0.5
    dsk = dsk_ref[...]

    hpg = MB_HEADS // MB_GROUPS
    ys = []
    for g in range(MB_GROUPS):
        bg = bm[:, g * MB_N:(g + 1) * MB_N]
        cg = cm[:, g * MB_N:(g + 1) * MB_N]
        cb = _dot_nt(cg.astype(BF16), bg.astype(BF16))
        for hh in range(hpg):
            h = g * hpg + hh
            acol = a_cs[:, h:h + 1]
            arow = a_cs_t[h:h + 1, :]
            lmat = jnp.exp(jnp.where(causal, acol - arow, -jnp.inf))
            xh = xs[:, h * MB_P:(h + 1) * MB_P]
            xdt = (xh * dt[:, h:h + 1]).astype(BF16)
            y = _dot((cb * lmat).astype(BF16), xdt)
            st = st_ref[h]
            y = y + _dot(cg.astype(BF16), st.astype(BF16)) * jnp.exp(acol)
            a_last = a_cs[L - 1:L, h:h + 1]
            bdec = (bg * jnp.exp(a_last - acol)).astype(BF16)
            st_ref[h] = st * jnp.exp(a_last) + _dot_tn(bdec, xdt)
            ys.append(y + dsk[:, h:h + 1] * xh)
    y = jnp.concatenate(ys, axis=1)
    y = y * _silu(z_ref[...])
    gw = BRANCH // MB_GROUPS
    outs = []
    for g in range(MB_GROUPS):
        yg = y[:, g * gw:(g + 1) * gw]
        ms = jnp.mean(yg * yg, axis=-1, keepdims=True)
        outs.append(yg * lax.rsqrt(ms + EPS))
    o_ref[...] = (jnp.concatenate(outs, axis=1) * gain_ref[...]).astype(o_ref.dtype)


def _pad_lanes(v, n=LANES):
    v = v.reshape(1, -1).astype(F32)
    return jnp.pad(v, ((0, 0), (0, n - v.shape[1])))


def mamba2(p_z, p_xbc, p_misc, conv_w, conv_b, dt_bias, a_log, d_skip, norm_gain, batch, seq):
    nc = seq // MB_CHUNK
    L = MB_CHUNK
    full = lambda shape: pl.BlockSpec(shape, lambda b, c: (0, 0))
    return pl.pallas_call(
        _mamba_body,
        out_shape=jax.ShapeDtypeStruct((batch * seq, BRANCH), BF16),
        grid=(batch, nc),
        in_specs=[pl.BlockSpec((L, BRANCH), lambda b, c: (b * nc + c, 0)),
                  pl.BlockSpec((L, MB_CONV_DIM), lambda b, c: (b * nc + c, 0)),
                  pl.BlockSpec((L, LANES), lambda b, c: (b * nc + c, MISC_DT // LANES)),
                  full((MB_CONV, MB_CONV_DIM)), full((1, MB_CONV_DIM)),
                  full((1, LANES)), full((1, LANES)), full((1, LANES)), full((1, BRANCH))],
        out_specs=pl.BlockSpec((L, BRANCH), lambda b, c: (b * nc + c, 0)),
        scratch_shapes=[pltpu.VMEM((L, MB_CONV_DIM), F32),
                        pltpu.VMEM((MB_HEADS, MB_N, MB_P), F32)],
        compiler_params=_cp(("parallel", "arbitrary")),
        name="mamba2",
    )(p_z, p_xbc, p_misc, conv_w.astype(F32), conv_b.reshape(1, -1).astype(F32),
      _pad_lanes(dt_bias), _pad_lanes(a_log), _pad_lanes(d_skip), norm_gain.reshape(1, BRANCH).astype(F32))


def _fox_cum_body(f_ref, bias_ref, qb_ref, kb_ref, cum_ref, carry_ref, *, tt):
    c = pl.program_id(1)

    @pl.when(c == 0)
    def _():
        carry_ref[...] = jnp.zeros_like(carry_ref)

    logf = jax.nn.log_sigmoid(f_ref[...] + bias_ref[...])
    cum = _dot(_tril(tt), logf, HIGHEST) + carry_ref[...]
    carry_ref[...] = cum[tt - 1:tt, :]

    c2 = cum * LOG2E
    cum_ref[...] = c2
    lane = lax.broadcasted_iota(I32, (1, LANES), 1)
    ones = jnp.where(lane < 6, 1.0, 0.0)
    for h in range(FX_HEADS):
        col = c2[:, h:h + 1]
        hi = col.astype(BF16).astype(F32)
        r1 = col - hi
        mid = r1.astype(BF16).astype(F32)
        lo = r1 - mid
        qb = jnp.where(lane == 0, hi, jnp.where(lane == 1, mid, jnp.where(lane == 2, lo, ones)))
        kb = jnp.where(lane == 3, -hi, jnp.where(lane == 4, -mid, jnp.where(lane == 5, -lo, ones)))
        qb_ref[:, h * LANES:(h + 1) * LANES] = qb.astype(BF16)
        kb_ref[:, h * LANES:(h + 1) * LANES] = kb.astype(BF16)


def fox_bias_columns(p_misc, f_bias, batch, seq, tt=SCAN_ROWS):
    tt = min(tt, seq)
    nt = seq // tt
    m = batch * seq
    out = jax.ShapeDtypeStruct((m, FX_HEADS * LANES), BF16)
    ospec = pl.BlockSpec((tt, FX_HEADS * LANES), lambda b, c: (b * nt + c, 0))
    return pl.pallas_call(
        functools.partial(_fox_cum_body, tt=tt),
        out_shape=(out, out, jax.ShapeDtypeStruct((m, LANES), F32)),
        grid=(batch, nt),
        in_specs=[pl.BlockSpec((tt, LANES), lambda b, c: (b * nt + c, MISC_FF // LANES)),
                  pl.BlockSpec((1, LANES), lambda b, c: (0, 0))],
        out_specs=(ospec, ospec, pl.BlockSpec((tt, LANES), lambda b, c: (b * nt + c, 0))),
        scratch_shapes=[pltpu.VMEM((1, LANES), F32)],
        compiler_params=_cp(("parallel", "arbitrary")),
        name="fox_bias_columns",
    )(p_misc, _pad_lanes(f_bias))


def _fox_norm_body(q_ref, k_ref, o_ref):
    c = lax.broadcasted_iota(I32, (BRANCH, LANES), 0) // FX_D
    l = lax.broadcasted_iota(I32, (BRANCH, LANES), 1)
    q = q_ref[...].astype(F32)
    k = k_ref[...].astype(F32)
    o_ref[...] = (_dot((q * q).astype(BF16), jnp.where(c == l, 1.0, 0.0).astype(BF16))
                  + _dot((k * k).astype(BF16), jnp.where(c + FX_HEADS == l, 1.0, 0.0).astype(BF16)))


def fox_row_norms(p_fx, tm=SCAN_ROWS):
    m = p_fx.shape[0]
    tm = min(tm, m)
    return pl.pallas_call(
        _fox_norm_body,
        out_shape=jax.ShapeDtypeStruct((m, LANES), F32),
        grid=(m // tm,),
        in_specs=[pl.BlockSpec((tm, BRANCH), lambda i: (i, 0)), pl.BlockSpec((tm, BRANCH), lambda i: (i, 1))],
        out_specs=pl.BlockSpec((tm, LANES), lambda i: (i, 0)),
        compiler_params=_cp(("parallel",)),
        name="fox_row_norms",
    )(p_fx, p_fx)


def _fox_bounds_body(nsq_ref, cum_ref, o_ref, *, tq, tk):
    seq = nsq_ref.shape[0]
    nq, nk = seq // tq, seq // tk
    nsq = nsq_ref[...]
    qn = jnp.sqrt(jnp.max(nsq.reshape(nq, tq, LANES), axis=1))
    kn = jnp.sqrt(jnp.max(nsq, axis=0, keepdims=True))
    kn = pltpu.roll(kn, LANES - FX_HEADS, 1)
    cum = cum_ref[...]
    cq_first = cum.reshape(nq, tq, LANES)[:, 0, :]
    ck_last = cum.reshape(nk, tk, LANES)[:, tk - 1, :]
    thr = cq_first + 2.0 * FOX_NORM_SLACK * qn * kn + FOX_SKIP_MARGIN
    start = jnp.zeros((nq, LANES), I32)
    for j in range(nk):
        start = start + jnp.where(ck_last[j:j + 1, :] > thr, 1, 0)
    o_ref[0] = start


def fox_first_blocks(nsq, cum, batch, seq, tq, tk):
    return pl.pallas_call(
        functools.partial(_fox_bounds_body, tq=tq, tk=tk),
        out_shape=jax.ShapeDtypeStruct((batch, seq // tq, LANES), I32),
        grid=(batch,),
        in_specs=[pl.BlockSpec((seq, LANES), lambda b: (b, 0)), pl.BlockSpec((seq, LANES), lambda b: (b, 0))],
        out_specs=pl.BlockSpec((1, seq // tq, LANES), lambda b: (b, 0, 0)),
        compiler_params=_cp(("parallel",)),
        name="fox_first_blocks",
    )(nsq, cum)


def _fox_body(first_ref, q_ref, qb_ref, k_ref, kb_ref, v_ref, o_ref, m_ref, l_ref, acc_ref, *, tq, tk, nq):
    b = pl.program_id(0)
    i = pl.program_id(1)
    j_diag = (i * tq + tq - 1) // tk
    m_ref[...] = jnp.full_like(m_ref, NEG_BIG)
    l_ref[...] = jnp.zeros_like(l_ref)
    acc_ref[...] = jnp.zeros_like(acc_ref)

    def head_block(h, j, masked):
        hs = slice(h * FX_D, (h + 1) * FX_D)
        ksl = pl.ds(pl.multiple_of(j * tk, tk), tk)
        qa = jnp.concatenate([q_ref[:, hs], qb_ref[:, hs]], axis=1)
        ka = jnp.concatenate([k_ref[ksl, hs], kb_ref[ksl, hs]], axis=1)
        s = _dot_nt(qa, ka)
        if masked:
            t_glob = i * tq + lax.broadcasted_iota(I32, (tq, 1), 0)
            s_glob = j * tk + lax.broadcasted_iota(I32, (1, tk), 1)
            s = jnp.where(s_glob <= t_glob, s, -jnp.inf)
        m_prev = m_ref[h][:, 0:1]
        m_new = jnp.maximum(m_prev, jnp.max(s, axis=1, keepdims=True))
        alpha = jnp.exp2(m_prev - m_new)
        p = jnp.exp2(s - m_new)
        l_new = alpha * l_ref[h][:, 0:1] + jnp.sum(p, axis=1, keepdims=True)
        acc_ref[h] = alpha * acc_ref[h] + _dot(p.astype(BF16), v_ref[ksl, hs])
        m_ref[h] = jnp.broadcast_to(m_new, (tq, LANES))
        l_ref[h] = jnp.broadcast_to(l_new, (tq, LANES))

    base = (b * nq + i) * FX_HEADS
    for h0 in range(0, FX_HEADS, 2):
        j0 = jnp.minimum(first_ref[base + h0], first_ref[base + h0 + 1])

        def body(j, carry, h0=h0):
            head_block(h0, j, False)
            head_block(h0 + 1, j, False)
            return carry

        lax.fori_loop(j0, j_diag, body, 0)
        head_block(h0, j_diag, True)
        head_block(h0 + 1, j_diag, True)

    for h in range(FX_HEADS):
        o_ref[:, h * FX_D:(h + 1) * FX_D] = (acc_ref[h] / l_ref[h][:, 0:1]).astype(o_ref.dtype)


def fox_attention(p_fx, qb, kb, first_blocks, batch, seq, tq=ATTN_Q_ROWS, tk=ATTN_KEYS):
    tq, tk = min(tq, seq), min(tk, seq)
    assert tk % tq == 0
    nq = seq // tq
    once = pl.Buffered(1)
    qspec = lambda colblk: pl.BlockSpec((tq, BRANCH), lambda b, i, f: (b * nq + i, colblk))
    kspec = lambda colblk: pl.BlockSpec((seq, BRANCH), lambda b, i, f: (b, colblk), pipeline_mode=once)
    return pl.pallas_call(
        functools.partial(_fox_body, tq=tq, tk=tk, nq=nq),
        out_shape=jax.ShapeDtypeStruct((batch * seq, BRANCH), BF16),
        grid_spec=pltpu.PrefetchScalarGridSpec(
            num_scalar_prefetch=1,
            grid=(batch, nq),
            in_specs=[qspec(0), qspec(0), kspec(1), kspec(0), kspec(2)],
            out_specs=pl.BlockSpec((tq, BRANCH), lambda b, i, f: (b * nq + i, 0)),
            scratch_shapes=[pltpu.VMEM((FX_HEADS, tq, LANES), F32),
                            pltpu.VMEM((FX_HEADS, tq, LANES), F32),
                            pltpu.VMEM((FX_HEADS, tq, FX_D), F32)]),
        compiler_params=_cp(("parallel", "arbitrary")),
        name="fox_attention",
    )(first_blocks, p_fx, qb, p_fx, kb, p_fx)


def fox_mixer(p_fx, p_misc, f_bias, batch, seq, tq=ATTN_Q_ROWS, tk=ATTN_KEYS):
    tq, tk = min(tq, seq), min(tk, seq)
    qb, kb, cum = fox_bias_columns(p_misc, f_bias, batch, seq)
    first = fox_first_blocks(fox_row_norms(p_fx), cum, batch, seq, tq, tk)
    first = first[:, :, :FX_HEADS].reshape(-1)
    return fox_attention(p_fx, qb, kb, first, batch, seq, tq, tk)


def _qlat_body(q_ref, wuk_ref, o_ref, *, tq):
    r = (_dot_nt(wuk_ref[...], q_ref[...]) * (DS_HEAD_DIM ** -0.5 * LOG2E)).astype(o_ref.dtype)
    for g in range(o_ref.shape[0]):
        o_ref[g] = r[:, g * tq:(g + 1) * tq]


def dsa_qlat(qcat, w_uk, tq, tm=PROJ_ROWS):
    m = qcat.shape[0]
    tm = min(tm, m)
    return pl.pallas_call(
        functools.partial(_qlat_body, tq=tq),
        out_shape=jax.ShapeDtypeStruct((m // tq, DS_KV_LORA, DS_HEADS * tq), BF16),
        grid=(m // tm, DS_HEADS),
        in_specs=[pl.BlockSpec((tm, DS_HEAD_DIM), lambda i, h: (i, h)),
                  pl.BlockSpec((DS_KV_LORA, DS_HEAD_DIM), lambda i, h: (0, h))],
        out_specs=pl.BlockSpec((tm // tq, DS_KV_LORA, tq), lambda i, h: (i, 0, h)),
        compiler_params=_cp(("parallel", "arbitrary")),
        name="dsa_qlat",
    )(qcat, w_uk)


def _dsa_body(qi_ref, misc_ref, ql_ref, kidx_ref, ckv_ref, ckvt_ref, wuv_ref, o_ref,
              keys_ref, thr_ref, s_ref, p_ref, m_ref, l_ref, alpha_ref, acc_ref, *, tq, tk, topk):
    seq_len = kidx_ref.shape[0]
    i = pl.program_id(1)
    nh = DS_HEADS
    cols = nh * tq
    nkb = (i * tq + tq - 1) // tk + 1
    t_glob = i * tq + lax.broadcasted_iota(I32, (1, tq), 1)
    krow = lax.broadcasted_iota(I32, (tk, 1), 0)

    hsel = (lax.broadcasted_iota(I32, (DS_IDX_HEADS, LANES), 1)
            == lax.broadcasted_iota(I32, (DS_IDX_HEADS, LANES), 0) + DS_IDX_DIM).astype(F32)
    w_t = lax.dot_general(hsel, misc_ref[...], (((1,), (1,)), ((), ())), preferred_element_type=F32,
                          precision=HIGHEST) * (DS_IDX_HEADS ** -0.5 * DS_IDX_DIM ** -0.5)
    qh = [qi_ref[:, h * DS_IDX_DIM:(h + 1) * DS_IDX_DIM] for h in range(DS_IDX_HEADS)]

    def score_body(j, carry):
        kb = kidx_ref[pl.ds(pl.multiple_of(j * tk, tk), tk), :]
        sc = jnp.zeros((tk, tq), F32)
        for h in range(DS_IDX_HEADS):
            sc = sc + jnp.maximum(_dot_nt(kb, qh[h]), 0.0) * w_t[h:h + 1, :]
        sc = jnp.where(sc == 0.0, 0.0, sc)
        sc = jnp.where(j * tk + krow <= t_glob, sc, -jnp.inf)
        bits = pltpu.bitcast(sc, I32)
        keys_ref[j] = jnp.where(bits < 0, bits ^ 0x7FFFFFFF, bits)
        return carry

    lax.fori_loop(0, nkb, score_body, 0)

    def count_rows(hit):
        return jnp.sum(jnp.where(hit, 1.0, 0.0).reshape(tk // 8, 8, tq), axis=0)

    def count_ge(cand):
        def body(jj, acc):
            j2 = jnp.minimum(2 * jj + 1, nkb - 1)
            both = jnp.where(2 * jj + 1 < nkb, 1.0, 0.0)
            return acc + count_rows(keys_ref[2 * jj] >= cand) + both * count_rows(keys_ref[j2] >= cand)
        acc = lax.fori_loop(0, (nkb + 1) // 2, body, jnp.zeros((8, tq), F32))
        return jnp.sum(acc, axis=0, keepdims=True)

    def top2_body(j, carry):
        m1, m2 = carry
        for g in range(tk // LANES):
            x = keys_ref[j, g * LANES:(g + 1) * LANES, :]
            m2 = jnp.maximum(m2, jnp.minimum(m1, x))
            m1 = jnp.maximum(m1, x)
        return m1, m2

    init = jnp.full((LANES, tq), INT_MIN, I32)
    m1, m2 = lax.fori_loop(0, nkb, top2_body, (init, init))
    lo = jnp.min(m2, axis=0, keepdims=True)
    hi = jnp.max(m1, axis=0, keepdims=True)

    kf = float(topk)

    def bisect(lo, hi, cnt):
        mid = (lo | hi) - ((lo ^ hi) >> 1)
        c = count_ge(mid)
        take = c >= kf
        return jnp.where(take, mid, lo), jnp.where(take, hi, mid - 1), jnp.where(take, c, cnt)

    passes_per_round = 3

    def search_cond(carry):
        lo, hi, cnt = carry
        return jnp.max(jnp.where(lo < hi, cnt, kf)) > kf

    def search_round(carry):
        lo, hi, cnt = carry
        for _ in range(passes_per_round):
            lo, hi, cnt = bisect(lo, hi, cnt)
        return lo, hi, cnt

    unknown = jnp.full((1, tq), 2.0 * kf + float(seq_len), F32)
    thr, _, cnt = lax.while_loop(search_cond, search_round, (lo, hi, unknown))
    thr_ref[...] = thr

    @pl.when(jnp.max(cnt) > kf)
    def _():
        def count_tied(x):
            def body(j, acc):
                return acc + count_rows((keys_ref[j] == thr) & (j * tk + krow <= x))
            acc = lax.fori_loop(0, nkb, body, jnp.zeros((8, tq), F32))
            return jnp.sum(acc, axis=0, keepdims=True)

        need = kf - count_ge(thr + 1)

        def cut_bit(bi, t):
            cand = t + lax.shift_left(jnp.int32(1), bi)
            return jnp.where(count_tied(cand - 1) < need, cand, t)

        nbits = max(1, (seq_len - 1).bit_length())
        cut = lax.fori_loop(0, nbits, lambda r, t: cut_bit(nbits - 1 - r, t), jnp.zeros((1, tq), I32))

        def demote(j, carry):
            kk = keys_ref[j]
            keys_ref[j] = jnp.where((kk == thr) & (j * tk + krow > cut), thr - 1, kk)
            return carry

        lax.fori_loop(0, nkb, demote, 0)

    m_ref[...] = jnp.full_like(m_ref, NEG_BIG)
    l_ref[...] = jnp.zeros_like(l_ref)
    acc_ref[...] = jnp.zeros_like(acc_ref)

    def attn_body(j, carry):
        ksl = pl.ds(pl.multiple_of(j * tk, tk), tk)
        s_ref[...] = _dot(ckv_ref[ksl, :], ql_ref[0])
        for c in range(cols // LANES):
            lsl = slice(c * LANES, (c + 1) * LANES)
            tsl = slice((c * LANES) % tq, (c * LANES) % tq + LANES)
            sel = (keys_ref[j, :, tsl] >= thr_ref[:, tsl]) & (j * tk + krow <= t_glob[:, tsl])
            s = jnp.where(sel, s_ref[:, lsl], -jnp.inf)
            m_prev = m_ref[:, lsl]
            m_new = jnp.maximum(m_prev, jnp.max(s, axis=0, keepdims=True))
            alpha = jnp.exp2(m_prev - m_new)
            p = jnp.exp2(s - m_new)
            l_ref[:, lsl] = alpha * l_ref[:, lsl] + jnp.sum(p, axis=0, keepdims=True)
            m_ref[:, lsl] = m_new
            alpha_ref[:, lsl] = alpha
            p_ref[:, lsl] = p.astype(BF16)
        pv = _dot(ckvt_ref[j], p_ref[...])
        acc_ref[...] = alpha_ref[...] * acc_ref[...] + pv
        return carry

    lax.fori_loop(0, nkb, attn_body, 0)

    for h in range(nh):
        hsl = slice(h * tq, (h + 1) * tq)
        o_lat = (acc_ref[:, hsl] / l_ref[:, hsl]).astype(BF16)
        o_ref[:, h * DS_HEAD_DIM:(h + 1) * DS_HEAD_DIM] = _dot_tn(
            o_lat, wuv_ref[:, h * DS_HEAD_DIM:(h + 1) * DS_HEAD_DIM]).astype(o_ref.dtype)


def dsa_attention(qcat, p_misc, q_lat, k_idx, c_kv, w_uv, batch, seq, tq=ATTN_Q_ROWS, tk=ATTN_KEYS):
    tq, tk = min(tq, seq), min(tk, seq)
    nq, nk = seq // tq, seq // tk
    cols = DS_HEADS * tq
    topk = min(DS_TOPK_MAX, seq // 4)
    assert tk >= topk and topk <= 2 * LANES and tq % LANES == 0 and tk % LANES == 0
    c_kv_t = c_kv.reshape(batch, nk, tk, DS_KV_LORA).transpose(0, 1, 3, 2).reshape(batch * nk, DS_KV_LORA, tk)
    return pl.pallas_call(
        functools.partial(_dsa_body, tq=tq, tk=tk, topk=topk),
        out_shape=jax.ShapeDtypeStruct((batch * seq, BRANCH), BF16),
        grid=(batch, nq),
        in_specs=[pl.BlockSpec((tq, DS_IDX_HEADS * DS_IDX_DIM), lambda b, i: (b * nq + i, 1)),
                  pl.BlockSpec((tq, LANES), lambda b, i: (b * nq + i, MISC_IDX // LANES)),
                  pl.BlockSpec((1, DS_KV_LORA, cols), lambda b, i: (b * nq + i, 0, 0)),
                  pl.BlockSpec((seq, DS_IDX_DIM), lambda b, i: (b, 0)),
                  pl.BlockSpec((seq, DS_KV_LORA), lambda b, i: (b, 0)),
                  pl.BlockSpec((nk, DS_KV_LORA, tk), lambda b, i: (b, 0, 0)),
                  pl.BlockSpec((DS_KV_LORA, BRANCH), lambda b, i: (0, 0))],
        out_specs=pl.BlockSpec((tq, BRANCH), lambda b, i: (b * nq + i, 0)),
        scratch_shapes=[pltpu.VMEM((nk, tk, tq), I32),
                        pltpu.VMEM((1, tq), I32),
                        pltpu.VMEM((tk, cols), F32),
                        pltpu.VMEM((tk, cols), BF16),
                        pltpu.VMEM((1, cols), F32),
                        pltpu.VMEM((1, cols), F32),
                        pltpu.VMEM((1, cols), F32),
                        pltpu.VMEM((DS_KV_LORA, cols), F32)],
        compiler_params=_cp(("parallel", "arbitrary"), VMEM_BIG_MB),
        name="dsa_attention",
    )(qcat, p_misc, q_lat, k_idx, c_kv, c_kv_t, w_uv)


def _merge_body(h_ref, wg_ref, wb_ref, ya_ref, yb_ref, yc_ref, yd_ref, o_ref, acc_ref):
    n = pl.program_id(2)
    gate = jax.nn.sigmoid(_dot(h_ref[...], wg_ref[0].astype(BF16)))
    wb = wb_ref[0].astype(BF16)
    for idx, y_ref in enumerate((ya_ref, yb_ref, yc_ref, yd_ref)):
        @pl.when(n == idx)
        def _(y_ref=y_ref, idx=idx):
            contrib = gate * _dot(y_ref[...], wb)
            if idx == 0:
                acc_ref[...] = contrib
            else:
                acc_ref[...] += contrib

    @pl.when(n == pl.num_programs(2) - 1)
    def _():
        o_ref[...] = acc_ref[...].astype(o_ref.dtype)


def gated_merge(h, w_gate, w_branch, layer, ys, tm=MERGE_ROWS, tn=MERGE_COLS):
    m = h.shape[0]
    tm = min(tm, m)
    ymap = lambda i, j, n: (i, 0)
    once = pl.Buffered(1)
    return pl.pallas_call(
        _merge_body,
        out_shape=jax.ShapeDtypeStruct((m, D_MODEL), BF16),
        grid=(m // tm, D_MODEL // tn, 4),
        in_specs=[pl.BlockSpec((tm, D_MODEL), lambda i, j, n: (i, 0), pipeline_mode=once),
                  pl.BlockSpec((None, 1, D_MODEL, tn), lambda i, j, n: (layer, n, 0, j)),
                  pl.BlockSpec((None, 1, BRANCH, tn), lambda i, j, n: (layer, n, 0, j))]
                 + [pl.BlockSpec((tm, BRANCH), ymap, pipeline_mode=once) for _ in range(4)],
        out_specs=pl.BlockSpec((tm, tn), lambda i, j, n: (i, j)),
        scratch_shapes=[pltpu.VMEM((tm, tn), F32)],
        compiler_params=_cp(("parallel", "parallel", "arbitrary"), VMEM_BIG_MB),
        name="gated_merge",
    )(h, w_gate, w_branch, *ys)


def _ffn_up_body(h_ref, halo_ref, wg_ref, wu_ref, cw_ref, o_ref, *, tiles_per_seq):
    i = pl.program_id(0)
    wg = wg_ref[...].astype(BF16)
    g = _dot(h_ref[...], wg)
    u = _dot(h_ref[...], wu_ref[...].astype(BF16))
    gh = _dot(halo_ref[...], wg)
    gh = gh * jnp.where(i % tiles_per_seq == 0, 0.0, 1.0)
    cw = cw_ref[...]
    row8 = lax.broadcasted_iota(I32, (8, 1), 0)
    y = g * cw[FFN_CONV - 1:FFN_CONV, :]
    for j in range(1, FFN_CONV):
        rolled = pltpu.roll(g, j, 0)
        head = jnp.where(row8 < j, pltpu.roll(gh, j, 0), rolled[0:8])
        shifted = jnp.concatenate([head, rolled[8:]], axis=0)
        y = y + shifted * cw[FFN_CONV - 1 - j:FFN_CONV - j, :]
    o_ref[...] = (_silu(y) * u).astype(o_ref.dtype)


def ffn_up(h, w_gate, w_up, conv_w, layer, seq, tm=FFN_ROWS, tn=FFN_COLS):
    m = h.shape[0]
    tm = min(tm, seq)
    dff = w_gate.shape[-1]
    wspec = pl.BlockSpec((None, D_MODEL, tn), lambda i, j: (layer, 0, j))
    return pl.pallas_call(
        functools.partial(_ffn_up_body, tiles_per_seq=seq // tm),
        out_shape=jax.ShapeDtypeStruct((m, dff), BF16),
        grid=(m // tm, dff // tn),
        in_specs=[pl.BlockSpec((tm, D_MODEL), lambda i, j: (i, 0), pipeline_mode=pl.Buffered(1)),
                  pl.BlockSpec((8, D_MODEL), lambda i, j: (jnp.maximum(i * (tm // 8) - 1, 0), 0)),
                  wspec, wspec,
                  pl.BlockSpec((None, FFN_CONV, tn), lambda i, j: (layer, 0, j))],
        out_specs=pl.BlockSpec((tm, tn), lambda i, j: (i, j)),
        compiler_params=_cp(("parallel", "arbitrary")),
        name="ffn_up",
    )(h, h, w_gate, w_up, conv_w.astype(F32))


def _in_proj_weights(w, l):
    o = 4 * BRANCH
    secs = {}
    for name, width in (("cq", DS_Q_LORA), ("ckv", DS_KV_LORA), ("kidx", DS_IDX_DIM), ("widx", DS_IDX_HEADS),
                        ("z", BRANCH), ("xbc", MB_CONV_DIM), ("dt", MB_HEADS), ("fx", 3 * BRANCH),
                        ("ff", FX_HEADS)):
        secs[name] = w[l, :, o:o + width]
        o += width
    zeros = lambda n: jnp.zeros((w.shape[1], n), w.dtype)
    misc = jnp.concatenate([secs["ckv"], secs["kidx"], secs["widx"], zeros(LANES - DS_IDX_DIM - DS_IDX_HEADS),
                            secs["dt"], zeros(LANES - MB_HEADS), secs["ff"], zeros(LANES - FX_HEADS)], axis=1)
    return dict(cq=secs["cq"], misc=misc, z=secs["z"], xbc=secs["xbc"], fx=secs["fx"])


def _layer(x, l, batch, seq, prm):
    h = rmsnorm(x, prm["attn_norm"][l], BF16)
    w = _in_proj_weights(prm["w_in_bf"], l)
    p_hg = matmul(h, prm["w_in_bf"], F32, n=4 * BRANCH, b_lead=(l,))
    p_cq = matmul(h, w["cq"], F32, tn=DS_Q_LORA)
    p_misc = matmul(h, w["misc"], F32, tn=MISC_W)
    p_z = matmul(h, w["z"], F32)
    p_xbc = matmul(h, w["xbc"], F32, tn=MB_CONV_DIM // 2)
    fx_scale = jnp.concatenate([jnp.full((1, BRANCH), FX_D ** -0.5 * LOG2E, F32), jnp.ones((1, 2 * BRANCH), F32)], 1)
    p_fx = matmul(h, w["fx"], BF16, col_scale=fx_scale)

    y_a = hgrn2(p_hg, prm["hgrn_lb_logits"], prm["hgrn_norm"][l], l, batch, seq)

    c_q = rmsnorm(p_cq, prm["dsa_q_norm"][l], BF16)
    c_kv = rmsnorm(p_misc, prm["dsa_kv_norm"][l], BF16, width=DS_KV_LORA, col_block=0)
    k_idx = p_misc[:, MISC_IDX:MISC_IDX + DS_IDX_DIM].astype(BF16)
    w_q = jnp.concatenate([prm["dsa_w_uq"][l].astype(BF16), prm["dsa_w_iq"][l].astype(BF16)], axis=1)
    qcat = matmul(c_q, w_q, BF16)
    dsa_tq = min(ATTN_Q_ROWS, seq)
    q_lat = dsa_qlat(qcat, prm["dsa_w_uk"][l].astype(BF16), dsa_tq)
    y_b = dsa_attention(qcat, p_misc, q_lat, k_idx, c_kv, prm["dsa_w_uv"][l].astype(BF16), batch, seq,
                        tq=dsa_tq)

    y_c = mamba2(p_z, p_xbc, p_misc, prm["ssm_conv_w"][l], prm["ssm_conv_b"][l], prm["ssm_dt_bias"][l],
                 prm["ssm_a_log"][l], prm["ssm_d"][l], prm["ssm_norm"][l], batch, seq)

    y_d = fox_mixer(p_fx, p_misc, prm["fox_f_bias"][l], batch, seq)

    merged = gated_merge(h, prm["w_gate"], prm["w_branch"], l, (y_a, y_b, y_c, y_d))
    x = matmul(merged, prm["w_out_bf"], F32, residual=x, b_lead=(l,), vmem_mb=VMEM_BIG_MB)

    h2 = rmsnorm(x, prm["ffn_norm"][l], BF16)
    act = ffn_up(h2, prm["ffn_w_gate"], prm["ffn_w_up"], prm["ffn_conv"], l, seq)
    return matmul(act, prm["w_down_bf"], F32, DOWN_ROWS, DOWN_COLS, residual=x, b_lead=(l,), vmem_mb=VMEM_BIG_MB)


def kernel(x, attn_norm, ffn_norm, final_norm, w_in, hgrn_lb_logits, hgrn_norm, dsa_q_norm, dsa_kv_norm,
           dsa_w_uq, dsa_w_iq, dsa_w_uk, dsa_w_uv, ssm_conv_w, ssm_conv_b, ssm_dt_bias, ssm_a_log, ssm_d,
           ssm_norm, fox_f_bias, w_gate, w_branch, w_out, ffn_w_gate, ffn_w_up, ffn_conv, ffn_w_down):
    batch, seq, d = x.shape
    prm = dict(attn_norm=attn_norm, ffn_norm=ffn_norm, w_in=w_in, hgrn_lb_logits=hgrn_lb_logits,
               hgrn_norm=hgrn_norm, dsa_q_norm=dsa_q_norm, dsa_kv_norm=dsa_kv_norm, dsa_w_uq=dsa_w_uq,
               dsa_w_iq=dsa_w_iq, dsa_w_uk=dsa_w_uk, dsa_w_uv=dsa_w_uv, ssm_conv_w=ssm_conv_w,
               ssm_conv_b=ssm_conv_b, ssm_dt_bias=ssm_dt_bias, ssm_a_log=ssm_a_log, ssm_d=ssm_d,
               ssm_norm=ssm_norm, fox_f_bias=fox_f_bias, w_gate=w_gate, w_branch=w_branch, w_out=w_out,
               ffn_w_gate=ffn_w_gate, ffn_w_up=ffn_w_up, ffn_conv=ffn_conv, ffn_w_down=ffn_w_down)
    prm.update(w_in_bf=w_in.astype(BF16), w_out_bf=w_out.astype(BF16), w_down_bf=ffn_w_down.astype(BF16))
    xf = x.reshape(batch * seq, d)
    for l in range(DEPTH):
        xf = _layer(xf, l, batch, seq, prm)
    return rmsnorm(xf, final_norm, x.dtype).reshape(batch, seq, d)
```

```python
import functools

import jax
import jax.numpy as jnp
from jax import lax
from jax.experimental import pallas as pl
from jax.experimental.pallas import tpu as pltpu

F32, BF16, I32 = jnp.float32, jnp.bfloat16, jnp.int32
HIGHEST = lax.Precision.HIGHEST

D_MODEL = 4096
DEPTH = 2
BRANCH = 1024
HG_HEADS, HG_D, HG_CHUNK, HG_SUB = 8, 128, 64, 16
DS_HEADS, DS_HEAD_DIM, DS_Q_LORA, DS_KV_LORA = 8, 128, 768, 512
DS_IDX_HEADS, DS_IDX_DIM, DS_TOPK_MAX = 16, 64, 256
MB_HEADS, MB_P, MB_N, MB_GROUPS, MB_CONV, MB_CHUNK = 16, 64, 128, 2, 4, 128
MB_CONV_DIM = BRANCH + 2 * MB_GROUPS * MB_N
FX_HEADS, FX_D = 8, 128
D_FF = 11008
FFN_CONV = 3
EPS = 1e-6
LANES = 128
NEG_BIG = -1e30
LOG2E = 1.4426950408889634
FOX_SKIP_MARGIN = 160.0
FOX_NORM_SLACK = 1.02
INT_MIN = -(2 ** 31)

MISC_CKV = 0
MISC_IDX = 512
MISC_DT = 640
MISC_FF = 768
MISC_W = 896


VMEM_MB, VMEM_BIG_MB = 48, 56
PROJ_ROWS, PROJ_COLS = 1024, 1024
DOWN_ROWS, DOWN_COLS = 512, 512
MERGE_ROWS, MERGE_COLS = 1024, 512
FFN_ROWS, FFN_COLS = 2048, 256
ATTN_Q_ROWS, ATTN_KEYS = 256, 512
HG_ROWS, HG_HEADS_PER_STEP = 256, 8
NORM_ROWS, SCAN_ROWS = 256, 512


def _cp(sem, vmem_mb=VMEM_MB):
    return pltpu.CompilerParams(dimension_semantics=sem, vmem_limit_bytes=vmem_mb * 2 ** 20)


def _dot(a, b, precision=None):
    return jnp.dot(a, b, preferred_element_type=F32, precision=precision)


def _dot_nt(a, b):
    return lax.dot_general(a, b, (((1,), (1,)), ((), ())), preferred_element_type=F32)


def _dot_tn(a, b):
    return lax.dot_general(a, b, (((0,), (0,)), ((), ())), preferred_element_type=F32)


def _tril(n):
    r = lax.broadcasted_iota(I32, (n, n), 0)
    c = lax.broadcasted_iota(I32, (n, n), 1)
    return (r >= c).astype(F32)


def _silu(x):
    return x * jax.nn.sigmoid(x)


def _rmsnorm_body(x_ref, g_ref, o_ref):
    x = x_ref[...].astype(F32)
    ms = jnp.mean(x * x, axis=-1, keepdims=True)
    o_ref[...] = (x * lax.rsqrt(ms + EPS) * g_ref[...]).astype(o_ref.dtype)


def rmsnorm(x, gain, out_dtype, width=None, col_block=0, tm=NORM_ROWS):
    m = x.shape[0]
    width = x.shape[1] if width is None else width
    return pl.pallas_call(
        _rmsnorm_body,
        out_shape=jax.ShapeDtypeStruct((m, width), out_dtype),
        grid=(m // tm,),
        in_specs=[pl.BlockSpec((tm, width), lambda i: (i, col_block)),
                  pl.BlockSpec((1, width), lambda i: (0, 0))],
        out_specs=pl.BlockSpec((tm, width), lambda i: (i, 0)),
        compiler_params=_cp(("parallel",)),
        name="rmsnorm",
    )(x, gain.reshape(1, width).astype(F32))


def _mm_body(a_ref, b_ref, o_ref):
    o_ref[...] = _dot(a_ref[...], b_ref[...].astype(BF16)).astype(o_ref.dtype)


def _mm_add_body(a_ref, b_ref, r_ref, o_ref):
    o_ref[...] = (r_ref[...] + _dot(a_ref[...], b_ref[...].astype(BF16))).astype(o_ref.dtype)


def _mm_scale_body(a_ref, b_ref, s_ref, o_ref):
    o_ref[...] = (_dot(a_ref[...], b_ref[...].astype(BF16)) * s_ref[...]).astype(o_ref.dtype)


def matmul(a, b, out_dtype, tm=PROJ_ROWS, tn=PROJ_COLS, residual=None, col_scale=None, n=None, b_lead=(),
           vmem_mb=VMEM_MB):
    m, k = a.shape
    n = b.shape[-1] if n is None else n
    tm, tn = min(tm, m), min(tn, n)
    in_specs = [pl.BlockSpec((tm, k), lambda i, j: (i, 0)),
                pl.BlockSpec((None,) * len(b_lead) + (k, tn), lambda i, j: tuple(b_lead) + (0, j))]
    args, body = (a, b), _mm_body
    if residual is not None:
        in_specs.append(pl.BlockSpec((tm, tn), lambda i, j: (i, j)))
        args, body = (a, b, residual), _mm_add_body
    elif col_scale is not None:
        in_specs.append(pl.BlockSpec((1, tn), lambda i, j: (0, j)))
        args, body = (a, b, col_scale), _mm_scale_body
    return pl.pallas_call(
        body,
        out_shape=jax.ShapeDtypeStruct((m, n), out_dtype),
        grid=(m // tm, n // tn),
        in_specs=in_specs,
        out_specs=pl.BlockSpec((tm, tn), lambda i, j: (i, j)),
        compiler_params=_cp(("parallel", "arbitrary"), vmem_mb),
        name="matmul",
    )(*args)


def _hgrn_body(lbl_ref, gain_ref, q_ref, f_ref, i_ref, g_ref, o_ref, st_ref, *, layer, nchunks, hpb):
    c = pl.program_id(2)

    @pl.when(c == 0)
    def _():
        st_ref[...] = jnp.zeros_like(st_ref)

    logits = lbl_ref[...]
    e = jnp.exp(logits - jnp.max(logits, axis=0, keepdims=True))
    p = e / jnp.sum(e, axis=0, keepdims=True)
    lb_all = jnp.sum(p[0:layer + 1], axis=0, keepdims=True) - p[0:1]
    gain_all = gain_ref[...]
    tril = _tril(HG_CHUNK)
    row = lax.broadcasted_iota(I32, (HG_CHUNK, 1), 0)
    row_in_sub = row % HG_SUB
    nsub = HG_CHUNK // HG_SUB

    def chunk(ci, carry):
        sl = pl.ds(pl.multiple_of(ci * HG_CHUNK, HG_CHUNK), HG_CHUNK)
        for hh in range(hpb):
            hs = slice(hh * HG_D, (hh + 1) * HG_D)
            lb, gain = lb_all[:, hs], gain_all[:, hs]
            q = _silu(q_ref[sl, hs])
            v = _silu(i_ref[sl, hs])
            f = lb + (1.0 - lb) * jax.nn.sigmoid(f_ref[sl, hs])
            k = 1.0 - f
            b = _dot(tril, jnp.log(f) * LOG2E, HIGHEST)
            st = st_ref[hh]

            o = _dot_nt((q * jnp.exp2(b)).astype(BF16), st.astype(BF16))

            intra = jnp.zeros((HG_CHUNK, HG_D), F32)
            for d in range(HG_SUB):
                ks = k if d == 0 else pltpu.roll(k, d, 0)
                bs = b if d == 0 else pltpu.roll(b, d, 0)
                vs = v if d == 0 else pltpu.roll(v, d, 0)
                expo = jnp.where(row_in_sub >= d, b - bs, -jnp.inf)
                w = jnp.sum(q * ks * jnp.exp2(expo), axis=-1, keepdims=True)
                intra = intra + w * vs
            o = o + intra

            parts = [jnp.zeros((HG_SUB, HG_D), F32)]
            for si in range(1, nsub):
                lo = si * HG_SUB
                r = b[lo - 1:lo, :]
                qi = (q[lo:lo + HG_SUB] * jnp.exp2(b[lo:lo + HG_SUB] - r)).astype(BF16)
                kj = (k[0:lo] * jnp.exp2(r - b[0:lo])).astype(BF16)
                sc = _dot_nt(qi, kj)
                parts.append(_dot(sc.astype(BF16), v[0:lo].astype(BF16)))
            o = o + jnp.concatenate(parts, axis=0)

            b_last = b[HG_CHUNK - 1:HG_CHUNK, :]
            kd = (k * jnp.exp2(b_last - b)).astype(BF16)
            st_ref[hh] = st * jnp.exp2(b_last) + _dot_tn(v.astype(BF16), kd)

            og = o * jax.nn.sigmoid(g_ref[sl, hs])
            ms = jnp.mean(og * og, axis=-1, keepdims=True)
            o_ref[sl, hs] = (og * lax.rsqrt(ms + EPS) * gain).astype(o_ref.dtype)
        return carry

    lax.fori_loop(0, nchunks, chunk, 0, unroll=True)


def hgrn2(p_hg, lb_logits, norm_gain, layer, batch, seq, tt=HG_ROWS, hpb=HG_HEADS_PER_STEP):
    tt = min(tt, seq)
    nt = seq // tt
    hb = HG_HEADS // hpb
    w = hpb * HG_D

    def col(sec):
        return lambda b, h, c: (b * nt + c, sec * hb + h)

    return pl.pallas_call(
        functools.partial(_hgrn_body, layer=layer, nchunks=tt // HG_CHUNK, hpb=hpb),
        out_shape=jax.ShapeDtypeStruct((batch * seq, BRANCH), BF16),
        grid=(batch, hb, nt),
        in_specs=[pl.BlockSpec((DEPTH, w), lambda b, h, c: (0, h)),
                  pl.BlockSpec((1, w), lambda b, h, c: (0, h)),
                  pl.BlockSpec((tt, w), col(0)),
                  pl.BlockSpec((tt, w), col(1)),
                  pl.BlockSpec((tt, w), col(2)),
                  pl.BlockSpec((tt, w), col(3))],
        out_specs=pl.BlockSpec((tt, w), lambda b, h, c: (b * nt + c, h)),
        scratch_shapes=[pltpu.VMEM((hpb, HG_D, HG_D), F32)],
        compiler_params=_cp(("parallel", "parallel", "arbitrary")),
        name="hgrn2",
    )(lb_logits.astype(F32), norm_gain.reshape(1, BRANCH).astype(F32), p_hg, p_hg, p_hg, p_hg)


def _mamba_body(z_ref, xbc_ref, dt_ref, cw_ref, cb_ref, dtb_ref, alog_ref, dsk_ref, gain_ref,
                o_ref, prev_ref, st_ref):
    c = pl.program_id(1)
    L = MB_CHUNK

    @pl.when(c == 0)
    def _():
        prev_ref[...] = jnp.zeros_like(prev_ref)
        st_ref[...] = jnp.zeros_like(st_ref)

    x = xbc_ref[...]
    prev = prev_ref[...]
    row = lax.broadcasted_iota(I32, (L, 1), 0)
    cw = cw_ref[...]
    acc = x * cw[MB_CONV - 1:MB_CONV, :] + cb_ref[...]
    for j in range(1, MB_CONV):
        sh = jnp.where(row >= j, pltpu.roll(x, j, 0), pltpu.roll(prev, j, 0))
        acc = acc + sh * cw[MB_CONV - 1 - j:MB_CONV - j, :]
    prev_ref[...] = x
    xbc = _silu(acc)
    xs = xbc[:, 0:BRANCH]
    bm = xbc[:, BRANCH:BRANCH + MB_GROUPS * MB_N]
    cm = xbc[:, BRANCH + MB_GROUPS * MB_N:MB_CONV_DIM]

    raw = dt_ref[...] + dtb_ref[...]
    dt = jnp.maximum(raw, 0.0) + jnp.log1p(jnp.exp(-jnp.abs(raw)))
    a = -jnp.exp(alog_ref[...]) * dt
    tril = _tril(L)
    a_cs = _dot(tril, a, HIGHEST)
    a_cs_t = a_cs.T
    causal = tril > 0.5
    dsk = dsk_ref[...]

    hpg = MB_HEADS // MB_GROUPS
    ys = []
    for g in range(MB_GROUPS):
        bg = bm[:, g * MB_N:(g + 1) * MB_N]
        cg = cm[:, g * MB_N:(g + 1) * MB_N]
        cb = _dot_nt(cg.astype(BF16), bg.astype(BF16))
        for hh in range(hpg):
            h = g * hpg + hh
            acol = a_cs[:, h:h + 1]
            arow = a_cs_t[h:h + 1, :]
            lmat = jnp.exp(jnp.where(causal, acol - arow, -jnp.inf))
            xh = xs[:, h * MB_P:(h + 1) * MB_P]
            xdt = (xh * dt[:, h:h + 1]).astype(BF16)
            y = _dot((cb * lmat).astype(BF16), xdt)
            st = st_ref[h]
            y = y + _dot(cg.astype(BF16), st.astype(BF16)) * jnp.exp(acol)
            a_last = a_cs[L - 1:L, h:h + 1]
            bdec = (bg * jnp.exp(a_last - acol)).astype(BF16)
            st_ref[h] = st * jnp.exp(a_last) + _dot_tn(bdec, xdt)
            ys.append(y + dsk[:, h:h + 1] * xh)
    y = jnp.concatenate(ys, axis=1)
    y = y * _silu(z_ref[...])
    gw = BRANCH // MB_GROUPS
    outs = []
    for g in range(MB_GROUPS):
        yg = y[:, g * gw:(g + 1) * gw]
        ms = jnp.mean(yg * yg, axis=-1, keepdims=True)
        outs.append(yg * lax.rsqrt(ms + EPS))
    o_ref[...] = (jnp.concatenate(outs, axis=1) * gain_ref[...]).astype(o_ref.dtype)


def _pad_lanes(v, n=LANES):
    v = v.reshape(1, -1).astype(F32)
    return jnp.pad(v, ((0, 0), (0, n - v.shape[1])))


def mamba2(p_z, p_xbc, p_misc, conv_w, conv_b, dt_bias, a_log, d_skip, norm_gain, batch, seq):
    nc = seq // MB_CHUNK
    L = MB_CHUNK
    full = lambda shape: pl.BlockSpec(shape, lambda b, c: (0, 0))
    return pl.pallas_call(
        _mamba_body,
        out_shape=jax.ShapeDtypeStruct((batch * seq, BRANCH), BF16),
        grid=(batch, nc),
        in_specs=[pl.BlockSpec((L, BRANCH), lambda b, c: (b * nc + c, 0)),
                  pl.BlockSpec((L, MB_CONV_DIM), lambda b, c: (b * nc + c, 0)),
                  pl.BlockSpec((L, LANES), lambda b, c: (b * nc + c, MISC_DT // LANES)),
                  full((MB_CONV, MB_CONV_DIM)), full((1, MB_CONV_DIM)),
                  full((1, LANES)), full((1, LANES)), full((1, LANES)), full((1, BRANCH))],
        out_specs=pl.BlockSpec((L, BRANCH), lambda b, c: (b * nc + c, 0)),
        scratch_shapes=[pltpu.VMEM((L, MB_CONV_DIM), F32),
                        pltpu.VMEM((MB_HEADS, MB_N, MB_P), F32)],
        compiler_params=_cp(("parallel", "arbitrary")),
        name="mamba2",
    )(p_z, p_xbc, p_misc, conv_w.astype(F32), conv_b.reshape(1, -1).astype(F32),
      _pad_lanes(dt_bias), _pad_lanes(a_log), _pad_lanes(d_skip), norm_gain.reshape(1, BRANCH).astype(F32))


def _fox_cum_body(f_ref, bias_ref, qb_ref, kb_ref, cum_ref, carry_ref, *, tt):
    c = pl.program_id(1)

    @pl.when(c == 0)
    def _():
        carry_ref[...] = jnp.zeros_like(carry_ref)

    logf = jax.nn.log_sigmoid(f_ref[...] + bias_ref[...])
    cum = _dot(_tril(tt), logf, HIGHEST) + carry_ref[...]
    carry_ref[...] = cum[tt - 1:tt, :]

    c2 = cum * LOG2E
    cum_ref[...] = c2
    lane = lax.broadcasted_iota(I32, (1, LANES), 1)
    ones = jnp.where(lane < 6, 1.0, 0.0)
    for h in range(FX_HEADS):
        col = c2[:, h:h + 1]
        hi = col.astype(BF16).astype(F32)
        r1 = col - hi
        mid = r1.astype(BF16).astype(F32)
        lo = r1 - mid
        qb = jnp.where(lane == 0, hi, jnp.where(lane == 1, mid, jnp.where(lane == 2, lo, ones)))
        kb = jnp.where(lane == 3, -hi, jnp.where(lane == 4, -mid, jnp.where(lane == 5, -lo, ones)))
        qb_ref[:, h * LANES:(h + 1) * LANES] = qb.astype(BF16)
        kb_ref[:, h * LANES:(h + 1) * LANES] = kb.astype(BF16)


def fox_bias_columns(p_misc, f_bias, batch, seq, tt=SCAN_ROWS):
    tt = min(tt, seq)
    nt = seq // tt
    m = batch * seq
    out = jax.ShapeDtypeStruct((m, FX_HEADS * LANES), BF16)
    ospec = pl.BlockSpec((tt, FX_HEADS * LANES), lambda b, c: (b * nt + c, 0))
    return pl.pallas_call(
        functools.partial(_fox_cum_body, tt=tt),
        out_shape=(out, out, jax.ShapeDtypeStruct((m, LANES), F32)),
        grid=(batch, nt),
        in_specs=[pl.BlockSpec((tt, LANES), lambda b, c: (b * nt + c, MISC_FF // LANES)),
                  pl.BlockSpec((1, LANES), lambda b, c: (0, 0))],
        out_specs=(ospec, ospec, pl.BlockSpec((tt, LANES), lambda b, c: (b * nt + c, 0))),
        scratch_shapes=[pltpu.VMEM((1, LANES), F32)],
        compiler_params=_cp(("parallel", "arbitrary")),
        name="fox_bias_columns",
    )(p_misc, _pad_lanes(f_bias))


def _fox_norm_body(q_ref, k_ref, o_ref):
    c = lax.broadcasted_iota(I32, (BRANCH, LANES), 0) // FX_D
    l = lax.broadcasted_iota(I32, (BRANCH, LANES), 1)
    q = q_ref[...].astype(F32)
    k = k_ref[...].astype(F32)
    o_ref[...] = (_dot((q * q).astype(BF16), jnp.where(c == l, 1.0, 0.0).astype(BF16))
                  + _dot((k * k).astype(BF16), jnp.where(c + FX_HEADS == l, 1.0, 0.0).astype(BF16)))


def fox_row_norms(p_fx, tm=SCAN_ROWS):
    m = p_fx.shape[0]
    tm = min(tm, m)
    return pl.pallas_call(
        _fox_norm_body,
        out_shape=jax.ShapeDtypeStruct((m, LANES), F32),
        grid=(m // tm,),
        in_specs=[pl.BlockSpec((tm, BRANCH), lambda i: (i, 0)), pl.BlockSpec((tm, BRANCH), lambda i: (i, 1))],
        out_specs=pl.BlockSpec((tm, LANES), lambda i: (i, 0)),
        compiler_params=_cp(("parallel",)),
        name="fox_row_norms",
    )(p_fx, p_fx)


def _fox_bounds_body(nsq_ref, cum_ref, o_ref, *, tq, tk):
    seq = nsq_ref.shape[0]
    nq, nk = seq // tq, seq // tk
    nsq = nsq_ref[...]
    qn = jnp.sqrt(jnp.max(nsq.reshape(nq, tq, LANES), axis=1))
    kn = jnp.sqrt(jnp.max(nsq, axis=0, keepdims=True))
    kn = pltpu.roll(kn, LANES - FX_HEADS, 1)
    cum = cum_ref[...]
    cq_first = cum.reshape(nq, tq, LANES)[:, 0, :]
    ck_last = cum.reshape(nk, tk, LANES)[:, tk - 1, :]
    thr = cq_first + 2.0 * FOX_NORM_SLACK * qn * kn + FOX_SKIP_MARGIN
    start = jnp.zeros((nq, LANES), I32)
    for j in range(nk):
        start = start + jnp.where(ck_last[j:j + 1, :] > thr, 1, 0)
    o_ref[0] = start


def fox_first_blocks(nsq, cum, batch, seq, tq, tk):
    return pl.pallas_call(
        functools.partial(_fox_bounds_body, tq=tq, tk=tk),
        out_shape=jax.ShapeDtypeStruct((batch, seq // tq, LANES), I32),
        grid=(batch,),
        in_specs=[pl.BlockSpec((seq, LANES), lambda b: (b, 0)), pl.BlockSpec((seq, LANES), lambda b: (b, 0))],
        out_specs=pl.BlockSpec((1, seq // tq, LANES), lambda b: (b, 0, 0)),
        compiler_params=_cp(("parallel",)),
        name="fox_first_blocks",
    )(nsq, cum)


def _fox_body(first_ref, q_ref, qb_ref, k_ref, kb_ref, v_ref, o_ref, m_ref, l_ref, acc_ref, *, tq, tk, nq):
    b = pl.program_id(0)
    i = pl.program_id(1)
    j_diag = (i * tq + tq - 1) // tk
    m_ref[...] = jnp.full_like(m_ref, NEG_BIG)
    l_ref[...] = jnp.zeros_like(l_ref)
    acc_ref[...] = jnp.zeros_like(acc_ref)

    def head_block(h, j, masked):
        hs = slice(h * FX_D, (h + 1) * FX_D)
        ksl = pl.ds(pl.multiple_of(j * tk, tk), tk)
        qa = jnp.concatenate([q_ref[:, hs], qb_ref[:, hs]], axis=1)
        ka = jnp.concatenate([k_ref[ksl, hs], kb_ref[ksl, hs]], axis=1)
        s = _dot_nt(qa, ka)
        if masked:
            t_glob = i * tq + lax.broadcasted_iota(I32, (tq, 1), 0)
            s_glob = j * tk + lax.broadcasted_iota(I32, (1, tk), 1)
            s = jnp.where(s_glob <= t_glob, s, -jnp.inf)
        m_prev = m_ref[h][:, 0:1]
        m_new = jnp.maximum(m_prev, jnp.max(s, axis=1, keepdims=True))
        alpha = jnp.exp2(m_prev - m_new)
        p = jnp.exp2(s - m_new)
        l_new = alpha * l_ref[h][:, 0:1] + jnp.sum(p, axis=1, keepdims=True)
        acc_ref[h] = alpha * acc_ref[h] + _dot(p.astype(BF16), v_ref[ksl, hs])
        m_ref[h] = jnp.broadcast_to(m_new, (tq, LANES))
        l_ref[h] = jnp.broadcast_to(l_new, (tq, LANES))

    base = (b * nq + i) * FX_HEADS
    for h0 in range(0, FX_HEADS, 2):
        j0 = jnp.minimum(first_ref[base + h0], first_ref[base + h0 + 1])

        def body(j, carry, h0=h0):
            head_block(h0, j, False)
            head_block(h0 + 1, j, False)
            return carry

        lax.fori_loop(j0, j_diag, body, 0)
        head_block(h0, j_diag, True)
        head_block(h0 + 1, j_diag, True)

    for h in range(FX_HEADS):
        o_ref[:, h * FX_D:(h + 1) * FX_D] = (acc_ref[h] / l_ref[h][:, 0:1]).astype(o_ref.dtype)


def fox_attention(p_fx, qb, kb, first_blocks, batch, seq, tq=ATTN_Q_ROWS, tk=ATTN_KEYS):
    tq, tk = min(tq, seq), min(tk, seq)
    assert tk % tq == 0
    nq = seq // tq
    once = pl.Buffered(1)
    qspec = lambda colblk: pl.BlockSpec((tq, BRANCH), lambda b, i, f: (b * nq + i, colblk))
    kspec = lambda colblk: pl.BlockSpec((seq, BRANCH), lambda b, i, f: (b, colblk), pipeline_mode=once)
    return pl.pallas_call(
        functools.partial(_fox_body, tq=tq, tk=tk, nq=nq),
        out_shape=jax.ShapeDtypeStruct((batch * seq, BRANCH), BF16),
        grid_spec=pltpu.PrefetchScalarGridSpec(
            num_scalar_prefetch=1,
            grid=(batch, nq),
            in_specs=[qspec(0), qspec(0), kspec(1), kspec(0), kspec(2)],
            out_specs=pl.BlockSpec((tq, BRANCH), lambda b, i, f: (b * nq + i, 0)),
            scratch_shapes=[pltpu.VMEM((FX_HEADS, tq, LANES), F32),
                            pltpu.VMEM((FX_HEADS, tq, LANES), F32),
                            pltpu.VMEM((FX_HEADS, tq, FX_D), F32)]),
        compiler_params=_cp(("parallel", "arbitrary")),
        name="fox_attention",
    )(first_blocks, p_fx, qb, p_fx, kb, p_fx)


def fox_mixer(p_fx, p_misc, f_bias, batch, seq, tq=ATTN_Q_ROWS, tk=ATTN_KEYS):
    tq, tk = min(tq, seq), min(tk, seq)
    qb, kb, cum = fox_bias_columns(p_misc, f_bias, batch, seq)
    first = fox_first_blocks(fox_row_norms(p_fx), cum, batch, seq, tq, tk)
    first = first[:, :, :FX_HEADS].reshape(-1)
    return fox_attention(p_fx, qb, kb, first, batch, seq, tq, tk)


def _qlat_body(q_ref, wuk_ref, o_ref, *, tq):
    r = (_dot_nt(wuk_ref[...], q_ref[...]) * (DS_HEAD_DIM ** -0.5 * LOG2E)).astype(o_ref.dtype)
    for g in range(o_ref.shape[0]):
        o_ref[g] = r[:, g * tq:(g + 1) * tq]


def dsa_qlat(qcat, w_uk, tq, tm=PROJ_ROWS):
    m = qcat.shape[0]
    tm = min(tm, m)
    return pl.pallas_call(
        functools.partial(_qlat_body, tq=tq),
        out_shape=jax.ShapeDtypeStruct((m // tq, DS_KV_LORA, DS_HEADS * tq), BF16),
        grid=(m // tm, DS_HEADS),
        in_specs=[pl.BlockSpec((tm, DS_HEAD_DIM), lambda i, h: (i, h)),
                  pl.BlockSpec((DS_KV_LORA, DS_HEAD_DIM), lambda i, h: (0, h))],
        out_specs=pl.BlockSpec((tm // tq, DS_KV_LORA, tq), lambda i, h: (i, 0, h)),
        compiler_params=_cp(("parallel", "arbitrary")),
        name="dsa_qlat",
    )(qcat, w_uk)


def _dsa_body(qi_ref, misc_ref, ql_ref, kidx_ref, ckv_ref, ckvt_ref, wuv_ref, o_ref,
              keys_ref, thr_ref, s_ref, p_ref, m_ref, l_ref, alpha_ref, acc_ref, *, tq, tk, topk):
    seq_len = kidx_ref.shape[0]
    i = pl.program_id(1)
    nh = DS_HEADS
    cols = nh * tq
    nkb = (i * tq + tq - 1) // tk + 1
    t_glob = i * tq + lax.broadcasted_iota(I32, (1, tq), 1)
    krow = lax.broadcasted_iota(I32, (tk, 1), 0)

    hsel = (lax.broadcasted_iota(I32, (DS_IDX_HEADS, LANES), 1)
            == lax.broadcasted_iota(I32, (DS_IDX_HEADS, LANES), 0) + DS_IDX_DIM).astype(F32)
    w_t = lax.dot_general(hsel, misc_ref[...], (((1,), (1,)), ((), ())), preferred_element_type=F32,
                          precision=HIGHEST) * (DS_IDX_HEADS ** -0.5 * DS_IDX_DIM ** -0.5)
    qh = [qi_ref[:, h * DS_IDX_DIM:(h + 1) * DS_IDX_DIM] for h in range(DS_IDX_HEADS)]

    def score_body(j, carry):
        kb = kidx_ref[pl.ds(pl.multiple_of(j * tk, tk), tk), :]
        sc = jnp.zeros((tk, tq), F32)
        for h in range(DS_IDX_HEADS):
            sc = sc + jnp.maximum(_dot_nt(kb, qh[h]), 0.0) * w_t[h:h + 1, :]
        sc = jnp.where(sc == 0.0, 0.0, sc)
        sc = jnp.where(j * tk + krow <= t_glob, sc, -jnp.inf)
        bits = pltpu.bitcast(sc, I32)
        keys_ref[j] = jnp.where(bits < 0, bits ^ 0x7FFFFFFF, bits)
        return carry

    lax.fori_loop(0, nkb, score_body, 0)

    def count_rows(hit):
        return jnp.sum(jnp.where(hit, 1.0, 0.0).reshape(tk // 8, 8, tq), axis=0)

    def count_ge(cand):
        def body(jj, acc):
            j2 = jnp.minimum(2 * jj + 1, nkb - 1)
            both = jnp.where(2 * jj + 1 < nkb, 1.0, 0.0)
            return acc + count_rows(keys_ref[2 * jj] >= cand) + both * count_rows(keys_ref[j2] >= cand)
        acc = lax.fori_loop(0, (nkb + 1) // 2, body, jnp.zeros((8, tq), F32))
        return jnp.sum(acc, axis=0, keepdims=True)

    def top2_body(j, carry):
        m1, m2 = carry
        for g in range(tk // LANES):
            x = keys_ref[j, g * LANES:(g + 1) * LANES, :]
            m2 = jnp.maximum(m2, jnp.minimum(m1, x))
            m1 = jnp.maximum(m1, x)
        return m1, m2

    init = jnp.full((LANES, tq), INT_MIN, I32)
    m1, m2 = lax.fori_loop(0, nkb, top2_body, (init, init))
    lo = jnp.min(m2, axis=0, keepdims=True)
    hi = jnp.max(m1, axis=0, keepdims=True)

    kf = float(topk)

    def bisect(lo, hi, cnt):
        mid = (lo | hi) - ((lo ^ hi) >> 1)
        c = count_ge(mid)
        take = c >= kf
        return jnp.where(take, mid, lo), jnp.where(take, hi, mid - 1), jnp.where(take, c, cnt)

    passes_per_round = 3

    def search_cond(carry):
        lo, hi, cnt = carry
        return jnp.max(jnp.where(lo < hi, cnt, kf)) > kf

    def search_round(carry):
        lo, hi, cnt = carry
        for _ in range(passes_per_round):
            lo, hi, cnt = bisect(lo, hi, cnt)
        return lo, hi, cnt

    unknown = jnp.full((1, tq), 2.0 * kf + float(seq_len), F32)
    thr, _, cnt = lax.while_loop(search_cond, search_round, (lo, hi, unknown))
    thr_ref[...] = thr

    @pl.when(jnp.max(cnt) > kf)
    def _():
        def count_tied(x):
            def body(j, acc):
                return acc + count_rows((keys_ref[j] == thr) & (j * tk + krow <= x))
            acc = lax.fori_loop(0, nkb, body, jnp.zeros((8, tq), F32))
            return jnp.sum(acc, axis=0, keepdims=True)

        need = kf - count_ge(thr + 1)

        def cut_bit(bi, t):
            cand = t + lax.shift_left(jnp.int32(1), bi)
            return jnp.where(count_tied(cand - 1) < need, cand, t)

        nbits = max(1, (seq_len - 1).bit_length())
        cut = lax.fori_loop(0, nbits, lambda r, t: cut_bit(nbits - 1 - r, t), jnp.zeros((1, tq), I32))

        def demote(j, carry):
            kk = keys_ref[j]
            keys_ref[j] = jnp.where((kk == thr) & (j * tk + krow > cut), thr - 1, kk)
            return carry

        lax.fori_loop(0, nkb, demote, 0)

    m_ref[...] = jnp.full_like(m_ref, NEG_BIG)
    l_ref[...] = jnp.zeros_like(l_ref)
    acc_ref[...] = jnp.zeros_like(acc_ref)

    def attn_body(j, carry):
        ksl = pl.ds(pl.multiple_of(j * tk, tk), tk)
        s_ref[...] = _dot(ckv_ref[ksl, :], ql_ref[0])
        for c in range(cols // LANES):
            lsl = slice(c * LANES, (c + 1) * LANES)
            tsl = slice((c * LANES) % tq, (c * LANES) % tq + LANES)
            sel = (keys_ref[j, :, tsl] >= thr_ref[:, tsl]) & (j * tk + krow <= t_glob[:, tsl])
            s = jnp.where(sel, s_ref[:, lsl], -jnp.inf)
            m_prev = m_ref[:, lsl]
            m_new = jnp.maximum(m_prev, jnp.max(s, axis=0, keepdims=True))
            alpha = jnp.exp2(m_prev - m_new)
            p = jnp.exp2(s - m_new)
            l_ref[:, lsl] = alpha * l_ref[:, lsl] + jnp.sum(p, axis=0, keepdims=True)
            m_ref[:, lsl] = m_new
            alpha_ref[:, lsl] = alpha
            p_ref[:, lsl] = p.astype(BF16)
        pv = _dot(ckvt_ref[j], p_ref[...])
        acc_ref[...] = alpha_ref[...] * acc_ref[...] + pv
        return carry

    lax.fori_loop(0, nkb, attn_body, 0)

    for h in range(nh):
        hsl = slice(h * tq, (h + 1) * tq)
        o_lat = (acc_ref[:, hsl] / l_ref[:, hsl]).astype(BF16)
        o_ref[:, h * DS_HEAD_DIM:(h + 1) * DS_HEAD_DIM] = _dot_tn(
            o_lat, wuv_ref[:, h * DS_HEAD_DIM:(h + 1) * DS_HEAD_DIM]).astype(o_ref.dtype)


def dsa_attention(qcat, p_misc, q_lat, k_idx, c_kv, w_uv, batch, seq, tq=ATTN_Q_ROWS, tk=ATTN_KEYS):
    tq, tk = min(tq, seq), min(tk, seq)
    nq, nk = seq // tq, seq // tk
    cols = DS_HEADS * tq
    topk = min(DS_TOPK_MAX, seq // 4)
    assert tk >= topk and topk <= 2 * LANES and tq % LANES == 0 and tk % LANES == 0
    c_kv_t = c_kv.reshape(batch, nk, tk, DS_KV_LORA).transpose(0, 1, 3, 2).reshape(batch * nk, DS_KV_LORA, tk)
    return pl.pallas_call(
        functools.partial(_dsa_body, tq=tq, tk=tk, topk=topk),
        out_shape=jax.ShapeDtypeStruct((batch * seq, BRANCH), BF16),
        grid=(batch, nq),
        in_specs=[pl.BlockSpec((tq, DS_IDX_HEADS * DS_IDX_DIM), lambda b, i: (b * nq + i, 1)),
                  pl.BlockSpec((tq, LANES), lambda b, i: (b * nq + i, MISC_IDX // LANES)),
                  pl.BlockSpec((1, DS_KV_LORA, cols), lambda b, i: (b * nq + i, 0, 0)),
                  pl.BlockSpec((seq, DS_IDX_DIM), lambda b, i: (b, 0)),
                  pl.BlockSpec((seq, DS_KV_LORA), lambda b, i: (b, 0)),
                  pl.BlockSpec((nk, DS_KV_LORA, tk), lambda b, i: (b, 0, 0)),
                  pl.BlockSpec((DS_KV_LORA, BRANCH), lambda b, i: (0, 0))],
        out_specs=pl.BlockSpec((tq, BRANCH), lambda b, i: (b * nq + i, 0)),
        scratch_shapes=[pltpu.VMEM((nk, tk, tq), I32),
                        pltpu.VMEM((1, tq), I32),
                        pltpu.VMEM((tk, cols), F32),
                        pltpu.VMEM((tk, cols), BF16),
                        pltpu.VMEM((1, cols), F32),
                        pltpu.VMEM((1, cols), F32),
                        pltpu.VMEM((1, cols), F32),
                        pltpu.VMEM((DS_KV_LORA, cols), F32)],
        compiler_params=_cp(("parallel", "arbitrary"), VMEM_BIG_MB),
        name="dsa_attention",
    )(qcat, p_misc, q_lat, k_idx, c_kv, c_kv_t, w_uv)


def _merge_body(h_ref, wg_ref, wb_ref, y_ref, o_ref, acc_ref):
    n = pl.program_id(2)
    gate = jax.nn.sigmoid(_dot(h_ref[...], wg_ref[0].astype(BF16)))
    contrib = gate * _dot(y_ref[...], wb_ref[0].astype(BF16))
    acc_ref[...] = jnp.where(n == 0, 0.0, acc_ref[...]) + contrib

    @pl.when(n == pl.num_programs(2) - 1)
    def _():
        o_ref[...] = acc_ref[...].astype(o_ref.dtype)


def gated_merge(h, w_gate, w_branch, layer, ys, tm=MERGE_ROWS, tn=MERGE_COLS):
    m = h.shape[0]
    tm = min(tm, m)
    y = jnp.stack(ys)
    once = pl.Buffered(1)
    return pl.pallas_call(
        _merge_body,
        out_shape=jax.ShapeDtypeStruct((m, D_MODEL), BF16),
        grid=(m // tm, D_MODEL // tn, 4),
        in_specs=[pl.BlockSpec((tm, D_MODEL), lambda i, j, n: (i, 0), pipeline_mode=once),
                  pl.BlockSpec((None, 1, D_MODEL, tn), lambda i, j, n: (layer, n, 0, j)),
                  pl.BlockSpec((None, 1, BRANCH, tn), lambda i, j, n: (layer, n, 0, j)),
                  pl.BlockSpec((None, tm, BRANCH), lambda i, j, n: (n, i, 0))],
        out_specs=pl.BlockSpec((tm, tn), lambda i, j, n: (i, j)),
        scratch_shapes=[pltpu.VMEM((tm, tn), F32)],
        compiler_params=_cp(("parallel", "parallel", "arbitrary"), VMEM_BIG_MB),
        name="gated_merge",
    )(h, w_gate, w_branch, y)


def _ffn_up_body(h_ref, halo_ref, wg_ref, wu_ref, cw_ref, o_ref, *, tiles_per_seq):
    i = pl.program_id(0)
    wg = wg_ref[...].astype(BF16)
    g = _dot(h_ref[...], wg)
    u = _dot(h_ref[...], wu_ref[...].astype(BF16))
    gh = _dot(halo_ref[...], wg)
    gh = gh * jnp.where(i % tiles_per_seq == 0, 0.0, 1.0)
    cw = cw_ref[...]
    row8 = lax.broadcasted_iota(I32, (8, 1), 0)
    y = g * cw[FFN_CONV - 1:FFN_CONV, :]
    for j in range(1, FFN_CONV):
        rolled = pltpu.roll(g, j, 0)
        head = jnp.where(row8 < j, pltpu.roll(gh, j, 0), rolled[0:8])
        shifted = jnp.concatenate([head, rolled[8:]], axis=0)
        y = y + shifted * cw[FFN_CONV - 1 - j:FFN_CONV - j, :]
    o_ref[...] = (_silu(y) * u).astype(o_ref.dtype)


def ffn_up(h, w_gate, w_up, conv_w, layer, seq, tm=FFN_ROWS, tn=FFN_COLS):
    m = h.shape[0]
    tm = min(tm, seq)
    dff = w_gate.shape[-1]
    wspec = pl.BlockSpec((None, D_MODEL, tn), lambda i, j: (layer, 0, j))
    return pl.pallas_call(
        functools.partial(_ffn_up_body, tiles_per_seq=seq // tm),
        out_shape=jax.ShapeDtypeStruct((m, dff), BF16),
        grid=(m // tm, dff // tn),
        in_specs=[pl.BlockSpec((tm, D_MODEL), lambda i, j: (i, 0), pipeline_mode=pl.Buffered(1)),
                  pl.BlockSpec((8, D_MODEL), lambda i, j: (jnp.maximum(i * (tm // 8) - 1, 0), 0)),
                  wspec, wspec,
                  pl.BlockSpec((None, FFN_CONV, tn), lambda i, j: (layer, 0, j))],
        out_specs=pl.BlockSpec((tm, tn), lambda i, j: (i, j)),
        compiler_params=_cp(("parallel", "arbitrary")),
        name="ffn_up",
    )(h, h, w_gate, w_up, conv_w.astype(F32))


def _in_proj_weights(w, l):
    o = 4 * BRANCH
    secs = {}
    for name, width in (("cq", DS_Q_LORA), ("ckv", DS_KV_LORA), ("kidx", DS_IDX_DIM), ("widx", DS_IDX_HEADS),
                        ("z", BRANCH), ("xbc", MB_CONV_DIM), ("dt", MB_HEADS), ("fx", 3 * BRANCH),
                        ("ff", FX_HEADS)):
        secs[name] = w[l, :, o:o + width]
        o += width
    zeros = lambda n: jnp.zeros((w.shape[1], n), w.dtype)
    misc = jnp.concatenate([secs["ckv"], secs["kidx"], secs["widx"], zeros(LANES - DS_IDX_DIM - DS_IDX_HEADS),
                            secs["dt"], zeros(LANES - MB_HEADS), secs["ff"], zeros(LANES - FX_HEADS)], axis=1)
    return dict(cq=secs["cq"], misc=misc, z=secs["z"], xbc=secs["xbc"], fx=secs["fx"])


def _layer(x, l, batch, seq, prm):
    h = rmsnorm(x, prm["attn_norm"][l], BF16)
    w = _in_proj_weights(prm["w_in_bf"], l)
    p_hg = matmul(h, prm["w_in_bf"], F32, n=4 * BRANCH, b_lead=(l,))
    p_cq = matmul(h, w["cq"], F32, tn=DS_Q_LORA)
    p_misc = matmul(h, w["misc"], F32, tn=MISC_W)
    p_z = matmul(h, w["z"], F32)
    p_xbc = matmul(h, w["xbc"], F32, tn=MB_CONV_DIM // 2)
    fx_scale = jnp.concatenate([jnp.full((1, BRANCH), FX_D ** -0.5 * LOG2E, F32), jnp.ones((1, 2 * BRANCH), F32)], 1)
    p_fx = matmul(h, w["fx"], BF16, col_scale=fx_scale)

    y_a = hgrn2(p_hg, prm["hgrn_lb_logits"], prm["hgrn_norm"][l], l, batch, seq)

    c_q = rmsnorm(p_cq, prm["dsa_q_norm"][l], BF16)
    c_kv = rmsnorm(p_misc, prm["dsa_kv_norm"][l], BF16, width=DS_KV_LORA, col_block=0)
    k_idx = p_misc[:, MISC_IDX:MISC_IDX + DS_IDX_DIM].astype(BF16)
    w_q = jnp.concatenate([prm["dsa_w_uq"][l].astype(BF16), prm["dsa_w_iq"][l].astype(BF16)], axis=1)
    qcat = matmul(c_q, w_q, BF16)
    dsa_tq = min(ATTN_Q_ROWS, seq)
    q_lat = dsa_qlat(qcat, prm["dsa_w_uk"][l].astype(BF16), dsa_tq)
    y_b = dsa_attention(qcat, p_misc, q_lat, k_idx, c_kv, prm["dsa_w_uv"][l].astype(BF16), batch, seq,
                        tq=dsa_tq)

    y_c = mamba2(p_z, p_xbc, p_misc, prm["ssm_conv_w"][l], prm["ssm_conv_b"][l], prm["ssm_dt_bias"][l],
                 prm["ssm_a_log"][l], prm["ssm_d"][l], prm["ssm_norm"][l], batch, seq)

    y_d = fox_mixer(p_fx, p_misc, prm["fox_f_bias"][l], batch, seq)

    merged = gated_merge(h, prm["w_gate"], prm["w_branch"], l, (y_a, y_b, y_c, y_d))
    x = matmul(merged, prm["w_out_bf"], F32, residual=x, b_lead=(l,), vmem_mb=VMEM_BIG_MB)

    h2 = rmsnorm(x, prm["ffn_norm"][l], BF16)
    act = ffn_up(h2, prm["ffn_w_gate"], prm["ffn_w_up"], prm["ffn_conv"], l, seq)
    return matmul(act, prm["w_down_bf"], F32, DOWN_ROWS, DOWN_COLS, residual=x, b_lead=(l,), vmem_mb=VMEM_BIG_MB)


def kernel(x, attn_norm, ffn_norm, final_norm, w_in, hgrn_lb_logits, hgrn_norm, dsa_q_norm, dsa_kv_norm,
           dsa_w_uq, dsa_w_iq, dsa_w_uk, dsa_w_uv, ssm_conv_w, ssm_conv_b, ssm_dt_bias, ssm_a_log, ssm_d,
           ssm_norm, fox_f_bias, w_gate, w_branch, w_out, ffn_w_gate, ffn_w_up, ffn_conv, ffn_w_down):
    batch, seq, d = x.shape
    prm = dict(attn_norm=attn_norm, ffn_norm=ffn_norm, w_in=w_in, hgrn_lb_logits=hgrn_lb_logits,
               hgrn_norm=hgrn_norm, dsa_q_norm=dsa_q_norm, dsa_kv_norm=dsa_kv_norm, dsa_w_uq=dsa_w_uq,
               dsa_w_iq=dsa_w_iq, dsa_w_uk=dsa_w_uk, dsa_w_uv=dsa_w_uv, ssm_conv_w=ssm_conv_w,
               ssm_conv_b=ssm_conv_b, ssm_dt_bias=ssm_dt_bias, ssm_a_log=ssm_a_log, ssm_d=ssm_d,
               ssm_norm=ssm_norm, fox_f_bias=fox_f_bias, w_gate=w_gate, w_branch=w_branch, w_out=w_out,
               ffn_w_gate=ffn_w_gate, ffn_w_up=ffn_w_up, ffn_conv=ffn_conv, ffn_w_down=ffn_w_down)
    prm.update(w_in_bf=w_in.astype(BF16), w_out_bf=w_out.astype(BF16), w_down_bf=ffn_w_down.astype(BF16))
    xf = x.reshape(batch * seq, d)
    for l in range(DEPTH):
        xf = _layer(xf, l, batch, seq, prm)
    return rmsnorm(xf, final_norm, x.dtype).reshape(batch, seq, d)
```

```python
import functools

import jax
import jax.numpy as jnp
from jax import lax
from jax.experimental import pallas as pl
from jax.experimental.pallas import tpu as pltpu

F32, BF16, I32 = jnp.float32, jnp.bfloat16, jnp.int32
HIGHEST = lax.Precision.HIGHEST

D_MODEL = 4096
DEPTH = 2
BRANCH = 1024
HG_HEADS, HG_D, HG_CHUNK, HG_SUB = 8, 128, 64, 16
DS_HEADS, DS_HEAD_DIM, DS_Q_LORA, DS_KV_LORA = 8, 128, 768, 512
DS_IDX_HEADS, DS_IDX_DIM, DS_TOPK_MAX = 16, 64, 256
MB_HEADS, MB_P, MB_N, MB_GROUPS, MB_CONV, MB_CHUNK = 16, 64, 128, 2, 4, 128
MB_CONV_DIM = BRANCH + 2 * MB_GROUPS * MB_N
FX_HEADS, FX_D = 8, 128
D_FF = 11008
FFN_CONV = 3
EPS = 1e-6
LANES = 128
NEG_BIG = -1e30
LOG2E = 1.4426950408889634
FOX_SKIP_MARGIN = 160.0
FOX_NORM_SLACK = 1.02
INT_MIN = -(2 ** 31)

MISC_CKV = 0
MISC_IDX = 512
MISC_DT = 640
MISC_FF = 768
MISC_W = 896


VMEM_MB, VMEM_BIG_MB = 48, 56
PROJ_ROWS, PROJ_COLS = 1024, 1024
DOWN_ROWS, DOWN_COLS = 512, 512
MERGE_ROWS, MERGE_COLS = 1024, 512
FFN_ROWS, FFN_COLS = 2048, 256
ATTN_Q_ROWS, ATTN_KEYS = 256, 512
HG_ROWS, HG_HEADS_PER_STEP = 256, 8
NORM_ROWS, SCAN_ROWS = 256, 512


def _cp(sem, vmem_mb=VMEM_MB):
    return pltpu.CompilerParams(dimension_semantics=sem, vmem_limit_bytes=vmem_mb * 2 ** 20)


def _dot(a, b, precision=None):
    return jnp.dot(a, b, preferred_element_type=F32, precision=precision)


def _dot_nt(a, b):
    return lax.dot_general(a, b, (((1,), (1,)), ((), ())), preferred_element_type=F32)


def _dot_tn(a, b):
    return lax.dot_general(a, b, (((0,), (0,)), ((), ())), preferred_element_type=F32)


def _tril(n):
    r = lax.broadcasted_iota(I32, (n, n), 0)
    c = lax.broadcasted_iota(I32, (n, n), 1)
    return (r >= c).astype(F32)


def _silu(x):
    return x * jax.nn.sigmoid(x)


def _rmsnorm_body(x_ref, g_ref, o_ref):
    x = x_ref[...].astype(F32)
    ms = jnp.mean(x * x, axis=-1, keepdims=True)
    o_ref[...] = (x * lax.rsqrt(ms + EPS) * g_ref[...]).astype(o_ref.dtype)


def rmsnorm(x, gain, out_dtype, width=None, col_block=0, tm=NORM_ROWS):
    m = x.shape[0]
    width = x.shape[1] if width is None else width
    return pl.pallas_call(
        _rmsnorm_body,
        out_shape=jax.ShapeDtypeStruct((m, width), out_dtype),
        grid=(m // tm,),
        in_specs=[pl.BlockSpec((tm, width), lambda i: (i, col_block)),
                  pl.BlockSpec((1, width), lambda i: (0, 0))],
        out_specs=pl.BlockSpec((tm, width), lambda i: (i, 0)),
        compiler_params=_cp(("parallel",)),
        name="rmsnorm",
    )(x, gain.reshape(1, width).astype(F32))


def _mm_body(a_ref, b_ref, o_ref):
    o_ref[...] = _dot(a_ref[...], b_ref[...].astype(BF16)).astype(o_ref.dtype)


def _mm_add_body(a_ref, b_ref, r_ref, o_ref):
    o_ref[...] = (r_ref[...] + _dot(a_ref[...], b_ref[...].astype(BF16))).astype(o_ref.dtype)


def _mm_scale_body(a_ref, b_ref, s_ref, o_ref):
    o_ref[...] = (_dot(a_ref[...], b_ref[...].astype(BF16)) * s_ref[...]).astype(o_ref.dtype)


def matmul(a, b, out_dtype, tm=PROJ_ROWS, tn=PROJ_COLS, residual=None, col_scale=None, n=None, b_lead=(),
           vmem_mb=VMEM_MB):
    m, k = a.shape
    n = b.shape[-1] if n is None else n
    tm, tn = min(tm, m), min(tn, n)
    in_specs = [pl.BlockSpec((tm, k), lambda i, j: (i, 0)),
                pl.BlockSpec((None,) * len(b_lead) + (k, tn), lambda i, j: tuple(b_lead) + (0, j))]
    args, body = (a, b), _mm_body
    if residual is not None:
        in_specs.append(pl.BlockSpec((tm, tn), lambda i, j: (i, j)))
        args, body = (a, b, residual), _mm_add_body
    elif col_scale is not None:
        in_specs.append(pl.BlockSpec((1, tn), lambda i, j: (0, j)))
        args, body = (a, b, col_scale), _mm_scale_body
    return pl.pallas_call(
        body,
        out_shape=jax.ShapeDtypeStruct((m, n), out_dtype),
        grid=(m // tm, n // tn),
        in_specs=in_specs,
        out_specs=pl.BlockSpec((tm, tn), lambda i, j: (i, j)),
        compiler_params=_cp(("parallel", "arbitrary"), vmem_mb),
        name="matmul",
    )(*args)


def _hgrn_body(lbl_ref, gain_ref, q_ref, f_ref, i_ref, g_ref, o_ref, st_ref, *, layer, nchunks, hpb):
    c = pl.program_id(2)

    @pl.when(c == 0)
    def _():
        st_ref[...] = jnp.zeros_like(st_ref)

    logits = lbl_ref[...]
    e = jnp.exp(logits - jnp.max(logits, axis=0, keepdims=True))
    p = e / jnp.sum(e, axis=0, keepdims=True)
    lb_all = jnp.sum(p[0:layer + 1], axis=0, keepdims=True) - p[0:1]
    gain_all = gain_ref[...]
    tril = _tril(HG_CHUNK)
    row = lax.broadcasted_iota(I32, (HG_CHUNK, 1), 0)
    row_in_sub = row % HG_SUB
    nsub = HG_CHUNK // HG_SUB

    def chunk(ci, carry):
        sl = pl.ds(pl.multiple_of(ci * HG_CHUNK, HG_CHUNK), HG_CHUNK)
        for hh in range(hpb):
            hs = slice(hh * HG_D, (hh + 1) * HG_D)
            lb, gain = lb_all[:, hs], gain_all[:, hs]
            q = _silu(q_ref[sl, hs])
            v = _silu(i_ref[sl, hs])
            f = lb + (1.0 - lb) * jax.nn.sigmoid(f_ref[sl, hs])
            k = 1.0 - f
            b = _dot(tril, jnp.log(f) * LOG2E, HIGHEST)
            st = st_ref[hh]

            o = _dot_nt((q * jnp.exp2(b)).astype(BF16), st.astype(BF16))

            intra = jnp.zeros((HG_CHUNK, HG_D), F32)
            for d in range(HG_SUB):
                ks = k if d == 0 else pltpu.roll(k, d, 0)
                bs = b if d == 0 else pltpu.roll(b, d, 0)
                vs = v if d == 0 else pltpu.roll(v, d, 0)
                expo = jnp.where(row_in_sub >= d, b - bs, -jnp.inf)
                w = jnp.sum(q * ks * jnp.exp2(expo), axis=-1, keepdims=True)
                intra = intra + w * vs
            o = o + intra

            parts = [jnp.zeros((HG_SUB, HG_D), F32)]
            for si in range(1, nsub):
                lo = si * HG_SUB
                r = b[lo - 1:lo, :]
                qi = (q[lo:lo + HG_SUB] * jnp.exp2(b[lo:lo + HG_SUB] - r)).astype(BF16)
                kj = (k[0:lo] * jnp.exp2(r - b[0:lo])).astype(BF16)
                sc = _dot_nt(qi, kj)
                parts.append(_dot(sc.astype(BF16), v[0:lo].astype(BF16)))
            o = o + jnp.concatenate(parts, axis=0)

            b_last = b[HG_CHUNK - 1:HG_CHUNK, :]
            kd = (k * jnp.exp2(b_last - b)).astype(BF16)
            st_ref[hh] = st * jnp.exp2(b_last) + _dot_tn(v.astype(BF16), kd)

            og = o * jax.nn.sigmoid(g_ref[sl, hs])
            ms = jnp.mean(og * og, axis=-1, keepdims=True)
            o_ref[sl, hs] = (og * lax.rsqrt(ms + EPS) * gain).astype(o_ref.dtype)
        return carry

    lax.fori_loop(0, nchunks, chunk, 0, unroll=True)


def hgrn2(p_hg, lb_logits, norm_gain, layer, batch, seq, tt=HG_ROWS, hpb=HG_HEADS_PER_STEP):
    tt = min(tt, seq)
    nt = seq // tt
    hb = HG_HEADS // hpb
    w = hpb * HG_D

    def col(sec):
        return lambda b, h, c: (b * nt + c, sec * hb + h)

    return pl.pallas_call(
        functools.partial(_hgrn_body, layer=layer, nchunks=tt // HG_CHUNK, hpb=hpb),
        out_shape=jax.ShapeDtypeStruct((batch * seq, BRANCH), BF16),
        grid=(batch, hb, nt),
        in_specs=[pl.BlockSpec((DEPTH, w), lambda b, h, c: (0, h)),
                  pl.BlockSpec((1, w), lambda b, h, c: (0, h)),
                  pl.BlockSpec((tt, w), col(0)),
                  pl.BlockSpec((tt, w), col(1)),
                  pl.BlockSpec((tt, w), col(2)),
                  pl.BlockSpec((tt, w), col(3))],
        out_specs=pl.BlockSpec((tt, w), lambda b, h, c: (b * nt + c, h)),
        scratch_shapes=[pltpu.VMEM((hpb, HG_D, HG_D), F32)],
        compiler_params=_cp(("parallel", "parallel", "arbitrary")),
        name="hgrn2",
    )(lb_logits.astype(F32), norm_gain.reshape(1, BRANCH).astype(F32), p_hg, p_hg, p_hg, p_hg)


def _mamba_body(z_ref, xbc_ref, dt_ref, cw_ref, cb_ref, dtb_ref, alog_ref, dsk_ref, gain_ref,
                o_ref, prev_ref, st_ref):
    c = pl.program_id(1)
    L = MB_CHUNK

    @pl.when(c == 0)
    def _():
        prev_ref[...] = jnp.zeros_like(prev_ref)
        st_ref[...] = jnp.zeros_like(st_ref)

    x = xbc_ref[...]
    prev = prev_ref[...]
    row = lax.broadcasted_iota(I32, (L, 1), 0)
    cw = cw_ref[...]
    acc = x * cw[MB_CONV - 1:MB_CONV, :] + cb_ref[...]
    for j in range(1, MB_CONV):
        sh = jnp.where(row >= j, pltpu.roll(x, j, 0), pltpu.roll(prev, j, 0))
        acc = acc + sh * cw[MB_CONV - 1 - j:MB_CONV - j, :]
    prev_ref[...] = x
    xbc = _silu(acc)
    xs = xbc[:, 0:BRANCH]
    bm = xbc[:, BRANCH:BRANCH + MB_GROUPS * MB_N]
    cm = xbc[:, BRANCH + MB_GROUPS * MB_N:MB_CONV_DIM]

    raw = dt_ref[...] + dtb_ref[...]
    dt = jnp.maximum(raw, 0.0) + jnp.log1p(jnp.exp(-jnp.abs(raw)))
    a = -jnp.exp(alog_ref[...]) * dt
    tril = _tril(L)
    a_cs = _dot(tril, a, HIGHEST)
    a_cs_t = a_cs.T
    causal = tril > 0.5
    dsk = dsk_ref[...]

    hpg = MB_HEADS // MB_GROUPS
    ys = []
    for g in range(MB_GROUPS):
        bg = bm[:, g * MB_N:(g + 1) * MB_N]
        cg = cm[:, g * MB_N:(g + 1) * MB_N]
        cb = _dot_nt(cg.astype(BF16), bg.astype(BF16))
        for hh in range(hpg):
            h = g * hpg + hh
            acol = a_cs[:, h:h + 1]
            arow = a_cs_t[h:h + 1, :]
            lmat = jnp.exp(jnp.where(causal, acol - arow, -jnp.inf))
            xh = xs[:, h * MB_P:(h + 1) * MB_P]
            xdt = (xh * dt[:, h:h + 1]).astype(BF16)
            y = _dot((cb * lmat).astype(BF16), xdt)
            st = st_ref[h]
            y = y + _dot(cg.astype(BF16), st.astype(BF16)) * jnp.exp(acol)
            a_last = a_cs[L - 1:L, h:h + 1]
            bdec = (bg * jnp.exp(a_last - acol)).astype(BF16)
            st_ref[h] = st * jnp.exp(a_last) + _dot_tn(bdec, xdt)
            ys.append(y + dsk[:, h:h + 1] * xh)
    y = jnp.concatenate(ys, axis=1)
    y = y * _silu(z_ref[...])
    gw = BRANCH // MB_GROUPS
    outs = []
    for g in range(MB_GROUPS):
        yg = y[:, g * gw:(g + 1) * gw]
        ms = jnp.mean(yg * yg, axis=-1, keepdims=True)
        outs.append(yg * lax.rsqrt(ms + EPS))
    o_ref[...] = (jnp.concatenate(outs, axis=1) * gain_ref[...]).astype(o_ref.dtype)


def _pad_lanes(v, n=LANES):
    v = v.reshape(1, -1).astype(F32)
    return jnp.pad(v, ((0, 0), (0, n - v.shape[1])))


def mamba2(p_z, p_xbc, p_misc, conv_w, conv_b, dt_bias, a_log, d_skip, norm_gain, batch, seq):
    nc = seq // MB_CHUNK
    L = MB_CHUNK
    full = lambda shape: pl.BlockSpec(shape, lambda b, c: (0, 0))
    return pl.pallas_call(
        _mamba_body,
        out_shape=jax.ShapeDtypeStruct((batch * seq, BRANCH), BF16),
        grid=(batch, nc),
        in_specs=[pl.BlockSpec((L, BRANCH), lambda b, c: (b * nc + c, 0)),
                  pl.BlockSpec((L, MB_CONV_DIM), lambda b, c: (b * nc + c, 0)),
                  pl.BlockSpec((L, LANES), lambda b, c: (b * nc + c, MISC_DT // LANES)),
                  full((MB_CONV, MB_CONV_DIM)), full((1, MB_CONV_DIM)),
                  full((1, LANES)), full((1, LANES)), full((1, LANES)), full((1, BRANCH))],
        out_specs=pl.BlockSpec((L, BRANCH), lambda b, c: (b * nc + c, 0)),
        scratch_shapes=[pltpu.VMEM((L, MB_CONV_DIM), F32),
                        pltpu.VMEM((MB_HEADS, MB_N, MB_P), F32)],
        compiler_params=_cp(("parallel", "arbitrary")),
        name="mamba2",
    )(p_z, p_xbc, p_misc, conv_w.astype(F32), conv_b.reshape(1, -1).astype(F32),
      _pad_lanes(dt_bias), _pad_lanes(a_log), _pad_lanes(d_skip), norm_gain.reshape(1, BRANCH).astype(F32))


def _fox_cum_body(f_ref, bias_ref, qb_ref, kb_ref, cum_ref, carry_ref, *, tt):
    c = pl.program_id(1)

    @pl.when(c == 0)
    def _():
        carry_ref[...] = jnp.zeros_like(carry_ref)

    logf = jax.nn.log_sigmoid(f_ref[...] + bias_ref[...])
    cum = _dot(_tril(tt), logf, HIGHEST) + carry_ref[...]
    carry_ref[...] = cum[tt - 1:tt, :]

    c2 = cum * LOG2E
    cum_ref[...] = c2
    lane = lax.broadcasted_iota(I32, (1, LANES), 1)
    ones = jnp.where(lane < 6, 1.0, 0.0)
    for h in range(FX_HEADS):
        col = c2[:, h:h + 1]
        hi = col.astype(BF16).astype(F32)
        r1 = col - hi
        mid = r1.astype(BF16).astype(F32)
        lo = r1 - mid
        qb = jnp.where(lane == 0, hi, jnp.where(lane == 1, mid, jnp.where(lane == 2, lo, ones)))
        kb = jnp.where(lane == 3, -hi, jnp.where(lane == 4, -mid, jnp.where(lane == 5, -lo, ones)))
        qb_ref[:, h * LANES:(h + 1) * LANES] = qb.astype(BF16)
        kb_ref[:, h * LANES:(h + 1) * LANES] = kb.astype(BF16)


def fox_bias_columns(p_misc, f_bias, batch, seq, tt=SCAN_ROWS):
    tt = min(tt, seq)
    nt = seq // tt
    m = batch * seq
    out = jax.ShapeDtypeStruct((m, FX_HEADS * LANES), BF16)
    ospec = pl.BlockSpec((tt, FX_HEADS * LANES), lambda b, c: (b * nt + c, 0))
    return pl.pallas_call(
        functools.partial(_fox_cum_body, tt=tt),
        out_shape=(out, out, jax.ShapeDtypeStruct((m, LANES), F32)),
        grid=(batch, nt),
        in_specs=[pl.BlockSpec((tt, LANES), lambda b, c: (b * nt + c, MISC_FF // LANES)),
                  pl.BlockSpec((1, LANES), lambda b, c: (0, 0))],
        out_specs=(ospec, ospec, pl.BlockSpec((tt, LANES), lambda b, c: (b * nt + c, 0))),
        scratch_shapes=[pltpu.VMEM((1, LANES), F32)],
        compiler_params=_cp(("parallel", "arbitrary")),
        name="fox_bias_columns",
    )(p_misc, _pad_lanes(f_bias))


def _fox_norm_body(q_ref, k_ref, o_ref):
    c = lax.broadcasted_iota(I32, (BRANCH, LANES), 0) // FX_D
    l = lax.broadcasted_iota(I32, (BRANCH, LANES), 1)
    q = q_ref[...].astype(F32)
    k = k_ref[...].astype(F32)
    o_ref[...] = (_dot((q * q).astype(BF16), jnp.where(c == l, 1.0, 0.0).astype(BF16))
                  + _dot((k * k).astype(BF16), jnp.where(c + FX_HEADS == l, 1.0, 0.0).astype(BF16)))


def fox_row_norms(p_fx, tm=SCAN_ROWS):
    m = p_fx.shape[0]
    tm = min(tm, m)
    return pl.pallas_call(
        _fox_norm_body,
        out_shape=jax.ShapeDtypeStruct((m, LANES), F32),
        grid=(m // tm,),
        in_specs=[pl.BlockSpec((tm, BRANCH), lambda i: (i, 0)), pl.BlockSpec((tm, BRANCH), lambda i: (i, 1))],
        out_specs=pl.BlockSpec((tm, LANES), lambda i: (i, 0)),
        compiler_params=_cp(("parallel",)),
        name="fox_row_norms",
    )(p_fx, p_fx)


def _fox_bounds_body(nsq_ref, cum_ref, o_ref, *, tq, tk):
    seq = nsq_ref.shape[0]
    nq, nk = seq // tq, seq // tk
    nsq = nsq_ref[...]
    qn = jnp.sqrt(jnp.max(nsq.reshape(nq, tq, LANES), axis=1))
    kn = jnp.sqrt(jnp.max(nsq, axis=0, keepdims=True))
    kn = pltpu.roll(kn, LANES - FX_HEADS, 1)
    cum = cum_ref[...]
    cq_first = cum.reshape(nq, tq, LANES)[:, 0, :]
    ck_last = cum.reshape(nk, tk, LANES)[:, tk - 1, :]
    thr = cq_first + 2.0 * FOX_NORM_SLACK * qn * kn + FOX_SKIP_MARGIN
    start = jnp.zeros((nq, LANES), I32)
    for j in range(nk):
        start = start + jnp.where(ck_last[j:j + 1, :] > thr, 1, 0)
    o_ref[0] = start


def fox_first_blocks(nsq, cum, batch, seq, tq, tk):
    return pl.pallas_call(
        functools.partial(_fox_bounds_body, tq=tq, tk=tk),
        out_shape=jax.ShapeDtypeStruct((batch, seq // tq, LANES), I32),
        grid=(batch,),
        in_specs=[pl.BlockSpec((seq, LANES), lambda b: (b, 0)), pl.BlockSpec((seq, LANES), lambda b: (b, 0))],
        out_specs=pl.BlockSpec((1, seq // tq, LANES), lambda b: (b, 0, 0)),
        compiler_params=_cp(("parallel",)),
        name="fox_first_blocks",
    )(nsq, cum)


def _fox_body(first_ref, q_ref, qb_ref, k_ref, kb_ref, v_ref, o_ref, m_ref, l_ref, acc_ref, *, tq, tk, nq):
    b = pl.program_id(0)
    i = pl.program_id(1)
    j_diag = (i * tq + tq - 1) // tk
    m_ref[...] = jnp.full_like(m_ref, NEG_BIG)
    l_ref[...] = jnp.zeros_like(l_ref)
    acc_ref[...] = jnp.zeros_like(acc_ref)

    def head_block(h, j, masked):
        hs = slice(h * FX_D, (h + 1) * FX_D)
        ksl = pl.ds(pl.multiple_of(j * tk, tk), tk)
        qa = jnp.concatenate([q_ref[:, hs], qb_ref[:, hs]], axis=1)
        ka = jnp.concatenate([k_ref[ksl, hs], kb_ref[ksl, hs]], axis=1)
        s = _dot_nt(qa, ka)
        if masked:
            t_glob = i * tq + lax.broadcasted_iota(I32, (tq, 1), 0)
            s_glob = j * tk + lax.broadcasted_iota(I32, (1, tk), 1)
            s = jnp.where(s_glob <= t_glob, s, -jnp.inf)
        m_prev = m_ref[h][:, 0:1]
        m_new = jnp.maximum(m_prev, jnp.max(s, axis=1, keepdims=True))
        alpha = jnp.exp2(m_prev - m_new)
        p = jnp.exp2(s - m_new)
        l_new = alpha * l_ref[h][:, 0:1] + jnp.sum(p, axis=1, keepdims=True)
        acc_ref[h] = alpha * acc_ref[h] + _dot(p.astype(BF16), v_ref[ksl, hs])
        m_ref[h] = jnp.broadcast_to(m_new, (tq, LANES))
        l_ref[h] = jnp.broadcast_to(l_new, (tq, LANES))

    base = (b * nq + i) * FX_HEADS
    for h0 in range(0, FX_HEADS, 2):
        j0 = jnp.minimum(first_ref[base + h0], first_ref[base + h0 + 1])

        def body(j, carry, h0=h0):
            head_block(h0, j, False)
            head_block(h0 + 1, j, False)
            return carry

        lax.fori_loop(j0, j_diag, body, 0)
        head_block(h0, j_diag, True)
        head_block(h0 + 1, j_diag, True)

    for h in range(FX_HEADS):
        o_ref[:, h * FX_D:(h + 1) * FX_D] = (acc_ref[h] / l_ref[h][:, 0:1]).astype(o_ref.dtype)


def fox_attention(p_fx, qb, kb, first_blocks, batch, seq, tq=ATTN_Q_ROWS, tk=ATTN_KEYS):
    tq, tk = min(tq, seq), min(tk, seq)
    assert tk % tq == 0
    nq = seq // tq
    once = pl.Buffered(1)
    qspec = lambda colblk: pl.BlockSpec((tq, BRANCH), lambda b, i, f: (b * nq + i, colblk))
    kspec = lambda colblk: pl.BlockSpec((seq, BRANCH), lambda b, i, f: (b, colblk), pipeline_mode=once)
    return pl.pallas_call(
        functools.partial(_fox_body, tq=tq, tk=tk, nq=nq),
        out_shape=jax.ShapeDtypeStruct((batch * seq, BRANCH), BF16),
        grid_spec=pltpu.PrefetchScalarGridSpec(
            num_scalar_prefetch=1,
            grid=(batch, nq),
            in_specs=[qspec(0), qspec(0), kspec(1), kspec(0), kspec(2)],
            out_specs=pl.BlockSpec((tq, BRANCH), lambda b, i, f: (b * nq + i, 0)),
            scratch_shapes=[pltpu.VMEM((FX_HEADS, tq, LANES), F32),
                            pltpu.VMEM((FX_HEADS, tq, LANES), F32),
                            pltpu.VMEM((FX_HEADS, tq, FX_D), F32)]),
        compiler_params=_cp(("parallel", "arbitrary")),
        name="fox_attention",
    )(first_blocks, p_fx, qb, p_fx, kb, p_fx)


def fox_mixer(p_fx, p_misc, f_bias, batch, seq, tq=ATTN_Q_ROWS, tk=ATTN_KEYS):
    tq, tk = min(tq, seq), min(tk, seq)
    qb, kb, cum = fox_bias_columns(p_misc, f_bias, batch, seq)
    first = fox_first_blocks(fox_row_norms(p_fx), cum, batch, seq, tq, tk)
    first = first[:, :, :FX_HEADS].reshape(-1)
    return fox_attention(p_fx, qb, kb, first, batch, seq, tq, tk)


def _qlat_body(q_ref, wuk_ref, o_ref, *, tq):
    r = (_dot_nt(wuk_ref[...], q_ref[...]) * (DS_HEAD_DIM ** -0.5 * LOG2E)).astype(o_ref.dtype)
    for g in range(o_ref.shape[0]):
        o_ref[g] = r[:, g * tq:(g + 1) * tq]


def dsa_qlat(qcat, w_uk, tq, tm=PROJ_ROWS):
    m = qcat.shape[0]
    tm = min(tm, m)
    return pl.pallas_call(
        functools.partial(_qlat_body, tq=tq),
        out_shape=jax.ShapeDtypeStruct((m // tq, DS_KV_LORA, DS_HEADS * tq), BF16),
        grid=(m // tm, DS_HEADS),
        in_specs=[pl.BlockSpec((tm, DS_HEAD_DIM), lambda i, h: (i, h)),
                  pl.BlockSpec((DS_KV_LORA, DS_HEAD_DIM), lambda i, h: (0, h))],
        out_specs=pl.BlockSpec((tm // tq, DS_KV_LORA, tq), lambda i, h: (i, 0, h)),
        compiler_params=_cp(("parallel", "arbitrary")),
        name="dsa_qlat",
    )(qcat, w_uk)


def _dsa_body(qi_ref, misc_ref, ql_ref, kidx_ref, ckv_ref, ckvt_ref, wuv_ref, o_ref,
              keys_ref, thr_ref, s_ref, p_ref, m_ref, l_ref, alpha_ref, acc_ref, *, tq, tk, topk):
    seq_len = kidx_ref.shape[0]
    i = pl.program_id(1)
    nh = DS_HEADS
    cols = nh * tq
    nkb = (i * tq + tq - 1) // tk + 1
    t_glob = i * tq + lax.broadcasted_iota(I32, (1, tq), 1)
    krow = lax.broadcasted_iota(I32, (tk, 1), 0)

    hsel = (lax.broadcasted_iota(I32, (DS_IDX_HEADS, LANES), 1)
            == lax.broadcasted_iota(I32, (DS_IDX_HEADS, LANES), 0) + DS_IDX_DIM).astype(F32)
    w_t = lax.dot_general(hsel, misc_ref[...], (((1,), (1,)), ((), ())), preferred_element_type=F32,
                          precision=HIGHEST) * (DS_IDX_HEADS ** -0.5 * DS_IDX_DIM ** -0.5)
    qh = [qi_ref[:, h * DS_IDX_DIM:(h + 1) * DS_IDX_DIM] for h in range(DS_IDX_HEADS)]

    def score_body(j, carry):
        kb = kidx_ref[pl.ds(pl.multiple_of(j * tk, tk), tk), :]
        sc = jnp.zeros((tk, tq), F32)
        for h in range(DS_IDX_HEADS):
            sc = sc + jnp.maximum(_dot_nt(kb, qh[h]), 0.0) * w_t[h:h + 1, :]
        sc = jnp.where(sc == 0.0, 0.0, sc)
        sc = jnp.where(j * tk + krow <= t_glob, sc, -jnp.inf)
        bits = pltpu.bitcast(sc, I32)
        keys_ref[j] = jnp.where(bits < 0, bits ^ 0x7FFFFFFF, bits)
        return carry

    lax.fori_loop(0, nkb, score_body, 0)

    def count_rows(hit):
        return jnp.sum(jnp.where(hit, 1.0, 0.0).reshape(tk // 8, 8, tq), axis=0)

    def count_ge(cand):
        def body(jj, acc):
            j2 = jnp.minimum(2 * jj + 1, nkb - 1)
            both = jnp.where(2 * jj + 1 < nkb, 1.0, 0.0)
            return acc + count_rows(keys_ref[2 * jj] >= cand) + both * count_rows(keys_ref[j2] >= cand)
        acc = lax.fori_loop(0, (nkb + 1) // 2, body, jnp.zeros((8, tq), F32))
        return jnp.sum(acc, axis=0, keepdims=True)

    def top2_body(j, carry):
        m1, m2 = carry
        for g in range(tk // LANES):
            x = keys_ref[j, g * LANES:(g + 1) * LANES, :]
            m2 = jnp.maximum(m2, jnp.minimum(m1, x))
            m1 = jnp.maximum(m1, x)
        return m1, m2

    init = jnp.full((LANES, tq), INT_MIN, I32)
    m1, m2 = lax.fori_loop(0, nkb, top2_body, (init, init))
    lo = jnp.min(m2, axis=0, keepdims=True)
    hi = jnp.max(m1, axis=0, keepdims=True)

    kf = float(topk)

    def bisect(lo, hi, cnt):
        mid = (lo | hi) - ((lo ^ hi) >> 1)
        c = count_ge(mid)
        take = c >= kf
        return jnp.where(take, mid, lo), jnp.where(take, hi, mid - 1), jnp.where(take, c, cnt)

    passes_per_round = 3

    def search_cond(carry):
        lo, hi, cnt = carry
        return jnp.max(jnp.where(lo < hi, cnt, kf)) > kf

    def search_round(carry):
        lo, hi, cnt = carry
        for _ in range(passes_per_round):
            lo, hi, cnt = bisect(lo, hi, cnt)
        return lo, hi, cnt

    unknown = jnp.full((1, tq), 2.0 * kf + float(seq_len), F32)
    thr, _, cnt = lax.while_loop(search_cond, search_round, (lo, hi, unknown))
    thr_ref[...] = thr

    @pl.when(jnp.max(cnt) > kf)
    def _():
        def count_tied(x):
            def body(j, acc):
                return acc + count_rows((keys_ref[j] == thr) & (j * tk + krow <= x))
            acc = lax.fori_loop(0, nkb, body, jnp.zeros((8, tq), F32))
            return jnp.sum(acc, axis=0, keepdims=True)

        need = kf - count_ge(thr + 1)

        def cut_bit(bi, t):
            cand = t + lax.shift_left(jnp.int32(1), bi)
            return jnp.where(count_tied(cand - 1) < need, cand, t)

        nbits = max(1, (seq_len - 1).bit_length())
        cut = lax.fori_loop(0, nbits, lambda r, t: cut_bit(nbits - 1 - r, t), jnp.zeros((1, tq), I32))

        def demote(j, carry):
            kk = keys_ref[j]
            keys_ref[j] = jnp.where((kk == thr) & (j * tk + krow > cut), thr - 1, kk)
            return carry

        lax.fori_loop(0, nkb, demote, 0)

    m_ref[...] = jnp.full_like(m_ref, NEG_BIG)
    l_ref[...] = jnp.zeros_like(l_ref)
    acc_ref[...] = jnp.zeros_like(acc_ref)

    def attn_body(j, carry):
        ksl = pl.ds(pl.multiple_of(j * tk, tk), tk)
        s_ref[...] = _dot(ckv_ref[ksl, :], ql_ref[0])
        for c in range(cols // LANES):
            lsl = slice(c * LANES, (c + 1) * LANES)
            tsl = slice((c * LANES) % tq, (c * LANES) % tq + LANES)
            sel = (keys_ref[j, :, tsl] >= thr_ref[:, tsl]) & (j * tk + krow <= t_glob[:, tsl])
            s = jnp.where(sel, s_ref[:, lsl], -jnp.inf)
            m_prev = m_ref[:, lsl]
            m_new = jnp.maximum(m_prev, jnp.max(s, axis=0, keepdims=True))
            alpha = jnp.exp2(m_prev - m_new)
            p = jnp.exp2(s - m_new)
            l_ref[:, lsl] = alpha * l_ref[:, lsl] + jnp.sum(p, axis=0, keepdims=True)
            m_ref[:, lsl] = m_new
            alpha_ref[:, lsl] = alpha
            p_ref[:, lsl] = p.astype(BF16)
        pv = _dot(ckvt_ref[j], p_ref[...])
        acc_ref[...] = alpha_ref[...] * acc_ref[...] + pv
        return carry

    lax.fori_loop(0, nkb, attn_body, 0)

    for h in range(nh):
        hsl = slice(h * tq, (h + 1) * tq)
        o_lat = (acc_ref[:, hsl] / l_ref[:, hsl]).astype(BF16)
        o_ref[:, h * DS_HEAD_DIM:(h + 1) * DS_HEAD_DIM] = _dot_tn(
            o_lat, wuv_ref[:, h * DS_HEAD_DIM:(h + 1) * DS_HEAD_DIM]).astype(o_ref.dtype)


def dsa_attention(qcat, p_misc, q_lat, k_idx, c_kv, w_uv, batch, seq, tq=ATTN_Q_ROWS, tk=ATTN_KEYS):
    tq, tk = min(tq, seq), min(tk, seq)
    nq, nk = seq // tq, seq // tk
    cols = DS_HEADS * tq
    topk = min(DS_TOPK_MAX, seq // 4)
    assert tk >= topk and topk <= 2 * LANES and tq % LANES == 0 and tk % LANES == 0
    c_kv_t = c_kv.reshape(batch, nk, tk, DS_KV_LORA).transpose(0, 1, 3, 2).reshape(batch * nk, DS_KV_LORA, tk)
    return pl.pallas_call(
        functools.partial(_dsa_body, tq=tq, tk=tk, topk=topk),
        out_shape=jax.ShapeDtypeStruct((batch * seq, BRANCH), BF16),
        grid=(batch, nq),
        in_specs=[pl.BlockSpec((tq, DS_IDX_HEADS * DS_IDX_DIM), lambda b, i: (b * nq + i, 1)),
                  pl.BlockSpec((tq, LANES), lambda b, i: (b * nq + i, MISC_IDX // LANES)),
                  pl.BlockSpec((1, DS_KV_LORA, cols), lambda b, i: (b * nq + i, 0, 0)),
                  pl.BlockSpec((seq, DS_IDX_DIM), lambda b, i: (b, 0)),
                  pl.BlockSpec((seq, DS_KV_LORA), lambda b, i: (b, 0)),
                  pl.BlockSpec((nk, DS_KV_LORA, tk), lambda b, i: (b, 0, 0)),
                  pl.BlockSpec((DS_KV_LORA, BRANCH), lambda b, i: (0, 0))],
        out_specs=pl.BlockSpec((tq, BRANCH), lambda b, i: (b * nq + i, 0)),
        scratch_shapes=[pltpu.VMEM((nk, tk, tq), I32),
                        pltpu.VMEM((1, tq), I32),
                        pltpu.VMEM((tk, cols), F32),
                        pltpu.VMEM((tk, cols), BF16),
                        pltpu.VMEM((1, cols), F32),
                        pltpu.VMEM((1, cols), F32),
                        pltpu.VMEM((1, cols), F32),
                        pltpu.VMEM((DS_KV_LORA, cols), F32)],
        compiler_params=_cp(("parallel", "arbitrary"), VMEM_BIG_MB),
        name="dsa_attention",
    )(qcat, p_misc, q_lat, k_idx, c_kv, c_kv_t, w_uv)


def _merge_body(h_ref, wg_ref, wb_ref, ya_ref, yb_ref, yc_ref, yd_ref, o_ref, y_ref, acc_ref):
    j = pl.program_id(1)
    n = pl.program_id(2)

    @pl.when((j == 0) & (n == 0))
    def _():
        for idx, src in enumerate((ya_ref, yb_ref, yc_ref, yd_ref)):
            y_ref[idx] = src[...]

    gate = jax.nn.sigmoid(_dot(h_ref[...], wg_ref[0].astype(BF16)))
    contrib = gate * _dot(y_ref[n], wb_ref[0].astype(BF16))
    acc_ref[...] = jnp.where(n == 0, 0.0, acc_ref[...]) + contrib

    @pl.when(n == pl.num_programs(2) - 1)
    def _():
        o_ref[...] = acc_ref[...].astype(o_ref.dtype)


def gated_merge(h, w_gate, w_branch, layer, ys, tm=MERGE_ROWS, tn=MERGE_COLS):
    m = h.shape[0]
    tm = min(tm, m)
    ymap = lambda i, j, n: (i, 0)
    once = pl.Buffered(1)
    return pl.pallas_call(
        _merge_body,
        out_shape=jax.ShapeDtypeStruct((m, D_MODEL), BF16),
        grid=(m // tm, D_MODEL // tn, 4),
        in_specs=[pl.BlockSpec((tm, D_MODEL), lambda i, j, n: (i, 0), pipeline_mode=once),
                  pl.BlockSpec((None, 1, D_MODEL, tn), lambda i, j, n: (layer, n, 0, j)),
                  pl.BlockSpec((None, 1, BRANCH, tn), lambda i, j, n: (layer, n, 0, j))]
                 + [pl.BlockSpec((tm, BRANCH), ymap, pipeline_mode=once) for _ in range(4)],
        out_specs=pl.BlockSpec((tm, tn), lambda i, j, n: (i, j)),
        scratch_shapes=[pltpu.VMEM((4, tm, BRANCH), BF16),
                        pltpu.VMEM((tm, tn), F32)],
        compiler_params=_cp(("parallel", "arbitrary", "arbitrary"), VMEM_BIG_MB),
        name="gated_merge",
    )(h, w_gate, w_branch, *ys)


def _ffn_up_body(h_ref, halo_ref, wg_ref, wu_ref, cw_ref, o_ref, *, tiles_per_seq):
    i = pl.program_id(0)
    wg = wg_ref[...].astype(BF16)
    g = _dot(h_ref[...], wg)
    u = _dot(h_ref[...], wu_ref[...].astype(BF16))
    gh = _dot(halo_ref[...], wg)
    gh = gh * jnp.where(i % tiles_per_seq == 0, 0.0, 1.0)
    cw = cw_ref[...]
    row8 = lax.broadcasted_iota(I32, (8, 1), 0)
    y = g * cw[FFN_CONV - 1:FFN_CONV, :]
    for j in range(1, FFN_CONV):
        rolled = pltpu.roll(g, j, 0)
        head = jnp.where(row8 < j, pltpu.roll(gh, j, 0), rolled[0:8])
        shifted = jnp.concatenate([head, rolled[8:]], axis=0)
        y = y + shifted * cw[FFN_CONV - 1 - j:FFN_CONV - j, :]
    o_ref[...] = (_silu(y) * u).astype(o_ref.dtype)


def ffn_up(h, w_gate, w_up, conv_w, layer, seq, tm=FFN_ROWS, tn=FFN_COLS):
    m = h.shape[0]
    tm = min(tm, seq)
    dff = w_gate.shape[-1]
    wspec = pl.BlockSpec((None, D_MODEL, tn), lambda i, j: (layer, 0, j))
    return pl.pallas_call(
        functools.partial(_ffn_up_body, tiles_per_seq=seq // tm),
        out_shape=jax.ShapeDtypeStruct((m, dff), BF16),
        grid=(m // tm, dff // tn),
        in_specs=[pl.BlockSpec((tm, D_MODEL), lambda i, j: (i, 0), pipeline_mode=pl.Buffered(1)),
                  pl.BlockSpec((8, D_MODEL), lambda i, j: (jnp.maximum(i * (tm // 8) - 1, 0), 0)),
                  wspec, wspec,
                  pl.BlockSpec((None, FFN_CONV, tn), lambda i, j: (layer, 0, j))],
        out_specs=pl.BlockSpec((tm, tn), lambda i, j: (i, j)),
        compiler_params=_cp(("parallel", "arbitrary")),
        name="ffn_up",
    )(h, h, w_gate, w_up, conv_w.astype(F32))


def _in_proj_weights(w, l):
    o = 4 * BRANCH
    secs = {}
    for name, width in (("cq", DS_Q_LORA), ("ckv", DS_KV_LORA), ("kidx", DS_IDX_DIM), ("widx", DS_IDX_HEADS),
                        ("z", BRANCH), ("xbc", MB_CONV_DIM), ("dt", MB_HEADS), ("fx", 3 * BRANCH),
                        ("ff", FX_HEADS)):
        secs[name] = w[l, :, o:o + width]
        o += width
    zeros = lambda n: jnp.zeros((w.shape[1], n), w.dtype)
    misc = jnp.concatenate([secs["ckv"], secs["kidx"], secs["widx"], zeros(LANES - DS_IDX_DIM - DS_IDX_HEADS),
                            secs["dt"], zeros(LANES - MB_HEADS), secs["ff"], zeros(LANES - FX_HEADS)], axis=1)
    return dict(cq=secs["cq"], misc=misc, z=secs["z"], xbc=secs["xbc"], fx=secs["fx"])


def _layer(x, l, batch, seq, prm):
    h = rmsnorm(x, prm["attn_norm"][l], BF16)
    w = _in_proj_weights(prm["w_in_bf"], l)
    p_hg = matmul(h, prm["w_in_bf"], F32, n=4 * BRANCH, b_lead=(l,))
    p_cq = matmul(h, w["cq"], F32, tn=DS_Q_LORA)
    p_misc = matmul(h, w["misc"], F32, tn=MISC_W)
    p_z = matmul(h, w["z"], F32)
    p_xbc = matmul(h, w["xbc"], F32, tn=MB_CONV_DIM // 2)
    fx_scale = jnp.concatenate([jnp.full((1, BRANCH), FX_D ** -0.5 * LOG2E, F32), jnp.ones((1, 2 * BRANCH), F32)], 1)
    p_fx = matmul(h, w["fx"], BF16, col_scale=fx_scale)

    y_a = hgrn2(p_hg, prm["hgrn_lb_logits"], prm["hgrn_norm"][l], l, batch, seq)

    c_q = rmsnorm(p_cq, prm["dsa_q_norm"][l], BF16)
    c_kv = rmsnorm(p_misc, prm["dsa_kv_norm"][l], BF16, width=DS_KV_LORA, col_block=0)
    k_idx = p_misc[:, MISC_IDX:MISC_IDX + DS_IDX_DIM].astype(BF16)
    w_q = jnp.concatenate([prm["dsa_w_uq"][l].astype(BF16), prm["dsa_w_iq"][l].astype(BF16)], axis=1)
    qcat = matmul(c_q, w_q, BF16)
    dsa_tq = min(ATTN_Q_ROWS, seq)
    q_lat = dsa_qlat(qcat, prm["dsa_w_uk"][l].astype(BF16), dsa_tq)
    y_b = dsa_attention(qcat, p_misc, q_lat, k_idx, c_kv, prm["dsa_w_uv"][l].astype(BF16), batch, seq,
                        tq=dsa_tq)

    y_c = mamba2(p_z, p_xbc, p_misc, prm["ssm_conv_w"][l], prm["ssm_conv_b"][l], prm["ssm_dt_bias"][l],
                 prm["ssm_a_log"][l], prm["ssm_d"][l], prm["ssm_norm"][l], batch, seq)

    y_d = fox_mixer(p_fx, p_misc, prm["fox_f_bias"][l], batch, seq)

    merged = gated_merge(h, prm["w_gate"], prm["w_branch"], l, (y_a, y_b, y_c, y_d))
    x = matmul(merged, prm["w_out_bf"], F32, residual=x, b_lead=(l,), vmem_mb=VMEM_BIG_MB)

    h2 = rmsnorm(x, prm["ffn_norm"][l], BF16)
    act = ffn_up(h2, prm["ffn_w_gate"], prm["ffn_w_up"], prm["ffn_conv"], l, seq)
    return matmul(act, prm["w_down_bf"], F32, DOWN_ROWS, DOWN_COLS, residual=x, b_lead=(l,), vmem_mb=VMEM_BIG_MB)


def kernel(x, attn_norm, ffn_norm, final_norm, w_in, hgrn_lb_logits, hgrn_norm, dsa_q_norm, dsa_kv_norm,
           dsa_w_uq, dsa_w_iq, dsa_w_uk, dsa_w_uv, ssm_conv_w, ssm_conv_b, ssm_dt_bias, ssm_a_log, ssm_d,
           ssm_norm, fox_f_bias, w_gate, w_branch, w_out, ffn_w_gate, ffn_w_up, ffn_conv, ffn_w_down):
    batch, seq, d = x.shape
    prm = dict(attn_norm=attn_norm, ffn_norm=ffn_norm, w_in=w_in, hgrn_lb_logits=hgrn_lb_logits,
               hgrn_norm=hgrn_norm, dsa_q_norm=dsa_q_norm, dsa_kv_norm=dsa_kv_norm, dsa_w_uq=dsa_w_uq,
               dsa_w_iq=dsa_w_iq, dsa_w_uk=dsa_w_uk, dsa_w_uv=dsa_w_uv, ssm_conv_w=ssm_conv_w,
               ssm_conv_b=ssm_conv_b, ssm_dt_bias=ssm_dt_bias, ssm_a_log=ssm_a_log, ssm_d=ssm_d,
               ssm_norm=ssm_norm, fox_f_bias=fox_f_bias, w_gate=w_gate, w_branch=w_branch, w_out=w_out,
               ffn_w_gate=ffn_w_gate, ffn_w_up=ffn_w_up, ffn_conv=ffn_conv, ffn_w_down=ffn_w_down)
    prm.update(w_in_bf=w_in.astype(BF16), w_out_bf=w_out.astype(BF16), w_down_bf=ffn_w_down.astype(BF16))
    xf = x.reshape(batch * seq, d)
    for l in range(DEPTH):
        xf = _layer(xf, l, batch, seq, prm)
    return rmsnorm(xf, final_norm, x.dtype).reshape(batch, seq, d)
```
